```python
import math
import jax, jax.numpy as jnp
from jax import lax
import numpy as np

D_MODEL = 2048
BATCH = 2
SEQ = 16384
DEPTH = 2

N_MIXERS = 2
BLOCK = 128
SB_HEADS = 16
SB_HEAD_DIM = D_MODEL // SB_HEADS
DIL_PATTERNS = ((128, 1), (512, 4), (2048, 16))
DIL_GROUPS = len(DIL_PATTERNS)
DIL_GROUP_HEADS = 8
DIL_HEAD_DIM = 128
DIL_WIDTH = DIL_GROUP_HEADS * DIL_HEAD_DIM
DIL_HEADS = DIL_GROUPS * DIL_GROUP_HEADS
REL_BUCKETS = 32
REL_MAX_DISTANCE = 2048
D_FF = 4096
CONV_WIDTH = 3
RMS_EPS = 1e-6
NEG_INF = -1e30
N_SB_LAYERS = (DEPTH + 1) // 2
N_DIL_LAYERS = DEPTH // 2

kernel_name = "hybrid_stickbreak_dilated_convffn"


def rms_norm(x, g):
    xf = x.astype(jnp.float32)
    y = xf * lax.rsqrt(jnp.mean(xf * xf, axis=-1, keepdims=True) + RMS_EPS)
    return (y * g.astype(jnp.float32)).astype(x.dtype)


def strict_lower(n):
    return jnp.asarray(np.tril(np.ones((n, n), np.float32), -1))


def stick_breaking_attention(q, k, v):
    B, H, S, hd = q.shape
    nblk = S // BLOCK
    scale = hd ** -0.5
    tri_in = strict_lower(BLOCK)
    outs = []
    for i in range(nblk):
        nk = i + 1
        qi = q[:, :, i * BLOCK:(i + 1) * BLOCK]
        kk = k[:, :, :nk * BLOCK].reshape(B, H, nk, BLOCK, hd)
        vv = v[:, :, :nk * BLOCK].reshape(B, H, nk, BLOCK, hd)
        z = jnp.einsum('bhqd,bhnkd->bhqnk', qi, kk).astype(jnp.float32) * scale
        qpos = i * BLOCK + jnp.arange(BLOCK)
        kpos = jnp.arange(nk * BLOCK).reshape(nk, BLOCK)
        causal = kpos[None] < qpos[:, None, None]
        l1m = jnp.where(causal, jax.nn.log_sigmoid(-z), 0.0)
        within = jnp.einsum('bhqnj,js->bhqns', l1m, tri_in)
        later = jnp.einsum('bhqm,mn->bhqn', jnp.sum(l1m, axis=-1), strict_lower(nk))
        a = jnp.where(causal, jnp.exp(jax.nn.log_sigmoid(z) + within + later[..., None]), 0.0)
        outs.append(jnp.einsum('bhqnk,bhnkd->bhqd', a.astype(v.dtype), vv))
    return jnp.concatenate(outs, axis=2)


def stick_breaking_mixer(h, w_qkv, w_o):
    B, S, _ = h.shape
    qkv = (h @ w_qkv).reshape(B, S, 3, SB_HEADS, SB_HEAD_DIM).transpose(2, 0, 3, 1, 4)
    o = stick_breaking_attention(qkv[0], qkv[1], qkv[2])
    o = o.transpose(0, 2, 1, 3).reshape(B, S, SB_HEADS * SB_HEAD_DIM).astype(h.dtype)
    return o @ w_o


def t5_causal_bucket(n):
    max_exact = REL_BUCKETS // 2
    nf = jnp.maximum(n, 1).astype(jnp.float32)
    large = max_exact + (jnp.log(nf / max_exact) / math.log(REL_MAX_DISTANCE / max_exact)
                         * (REL_BUCKETS - max_exact)).astype(jnp.int32)
    large = jnp.minimum(large, REL_BUCKETS - 1)
    return jnp.where(n < max_exact, n, large)


def dilated_group_attention(q, k, v, rel_table, span, dil):
    B, S, H, hd = q.shape
    unit = dil * BLOCK
    S_pad = -(-S // unit) * unit
    M = S_pad // dil
    nb = M // BLOCK
    scale = hd ** -0.5

    def split(t):
        t = jnp.pad(t, ((0, 0), (0, S_pad - S), (0, 0), (0, 0)))
        t = t.reshape(B, M, dil, H, hd).transpose(0, 2, 3, 1, 4)
        return t.reshape(B, dil, H, nb, BLOCK, hd)

    def with_prev(t):
        prev = jnp.pad(t, ((0, 0), (0, 0), (0, 0), (1, 0), (0, 0), (0, 0)))[:, :, :, :-1]
        return jnp.concatenate([prev, t], axis=4)

    qs = split(q)
    ks, vs = with_prev(split(k)), with_prev(split(v))

    qi = jnp.arange(BLOCK)[:, None]
    kc = jnp.arange(2 * BLOCK)[None, :]
    rel = BLOCK + qi - kc
    band = (rel >= 0) & (rel <= span)
    first_ok = (jnp.arange(nb)[:, None, None] > 0) | (kc[None] >= BLOCK)
    mask = band[None] & first_ok
    bucket = t5_causal_bucket(jnp.maximum(rel, 0) * dil)
    bias = rel_table.astype(jnp.float32)[bucket].transpose(2, 0, 1)

    s = jnp.einsum('brhnqd,brhnkd->brhnqk', qs, ks).astype(jnp.float32) * scale
    s = jnp.where(mask, s + bias[:, None], NEG_INF)
    m = jnp.max(s, axis=-1, keepdims=True)
    p = jnp.exp(s - m)
    l = jnp.sum(p, axis=-1, keepdims=True)
    o = jnp.einsum('brhnqk,brhnkd->brhnqd', p.astype(vs.dtype), vs).astype(jnp.float32) / l
    lse = (m + jnp.log(l))[..., 0]

    o = o.reshape(B, dil, H, M, hd).transpose(0, 3, 1, 2, 4).reshape(B, S_pad, H, hd)[:, :S]
    lse = lse.reshape(B, dil, H, M).transpose(0, 3, 1, 2).reshape(B, S_pad, H)[:, :S]
    return o, lse


def dilated_mixer(h, w_qkv, w_o, rel_bias):
    B, S, _ = h.shape
    qkv = (h @ w_qkv).reshape(B, S, DIL_GROUPS, 3, DIL_GROUP_HEADS, DIL_HEAD_DIM)
    outs, lses = [], []
    for g, (window, dil) in enumerate(DIL_PATTERNS):
        table = rel_bias[:, g * DIL_GROUP_HEADS:(g + 1) * DIL_GROUP_HEADS]
        o, lse = dilated_group_attention(qkv[:, :, g, 0], qkv[:, :, g, 1], qkv[:, :, g, 2],
                                         table, window // dil, dil)
        outs.append(o)
        lses.append(lse)
    wts = jax.nn.softmax(jnp.stack(lses, axis=0), axis=0)
    o = jnp.sum(wts[..., None] * jnp.stack(outs, axis=0), axis=0)
    return o.reshape(B, S, DIL_WIDTH).astype(h.dtype) @ w_o


def conv_ffn(h, w_up, conv_w, conv_b, w_down):
    S = h.shape[1]
    gate, val = jnp.split(h @ w_up, 2, axis=-1)
    gp = jnp.pad(gate, ((0, 0), (CONV_WIDTH - 1, 0), (0, 0)))
    conv = conv_b + sum(gp[:, j:j + S] * conv_w[j] for j in range(CONV_WIDTH))
    return (jax.nn.silu(conv) * val) @ w_down


def setup_inputs(seed: int = 0) -> dict:
    key = jax.random.key(seed)
    ks = jax.random.split(key, 16)
    f32 = jnp.float32
    D = D_MODEL

    def w(k, shape, fan_in):
        return jax.random.normal(k, shape, f32) * fan_in ** -0.5

    return {
        "x": jax.random.normal(ks[0], (BATCH, SEQ, D), f32),
        "ln_mix": 1.0 + 0.01 * jax.random.normal(ks[1], (DEPTH, D), f32),
        "ln_ffn": 1.0 + 0.01 * jax.random.normal(ks[2], (DEPTH, D), f32),
        "ln_f": 1.0 + 0.01 * jax.random.normal(ks[3], (D,), f32),
        "w_qkv_sb": w(ks[4], (N_SB_LAYERS, D, 3 * SB_HEADS * SB_HEAD_DIM), D),
        "w_o_sb": w(ks[5], (N_SB_LAYERS, SB_HEADS * SB_HEAD_DIM, D), SB_HEADS * SB_HEAD_DIM),
        "w_qkv_dil": w(ks[6], (N_DIL_LAYERS, D, DIL_GROUPS * 3 * DIL_WIDTH), D),
        "w_o_dil": w(ks[7], (N_DIL_LAYERS, DIL_WIDTH, D), DIL_WIDTH),
        "rel_bias": 0.1 * jax.random.normal(ks[8], (REL_BUCKETS, DIL_HEADS), f32),
        "w_up": w(ks[9], (DEPTH, D, 2 * D_FF), D),
        "conv_w": w(ks[10], (DEPTH, CONV_WIDTH, D_FF), CONV_WIDTH),
        "conv_b": 0.01 * jax.random.normal(ks[11], (DEPTH, D_FF), f32),
        "w_down": w(ks[12], (DEPTH, D_FF, D), D_FF),
    }


def reference(x, ln_mix, ln_ffn, ln_f, w_qkv_sb, w_o_sb, w_qkv_dil, w_o_dil, rel_bias,
              w_up, conv_w, conv_b, w_down):
    h = x
    for i in range(DEPTH):
        hn = rms_norm(h, ln_mix[i])
        j = i // N_MIXERS
        if i % N_MIXERS == 0:
            h = h + stick_breaking_mixer(hn, w_qkv_sb[j], w_o_sb[j])
        else:
            h = h + dilated_mixer(hn, w_qkv_dil[j], w_o_dil[j], rel_bias)
        hn = rms_norm(h, ln_ffn[i])
        h = h + conv_ffn(hn, w_up[i], conv_w[i], conv_b[i], w_down[i])
    return rms_norm(h, ln_f)
```

```python
import functools
import math

import jax
import jax.numpy as jnp
import numpy as np
from jax import lax
from jax.experimental import pallas as pl
from jax.experimental.pallas import tpu as pltpu

BLOCK = 128
SB_HEADS = 16
HEAD_DIM = 128
DIL_PATTERNS = ((128, 1), (512, 4), (2048, 16))
DIL_GROUPS = len(DIL_PATTERNS)
DIL_GROUP_HEADS = 8
DIL_WIDTH = DIL_GROUP_HEADS * HEAD_DIM
REL_BUCKETS = 32
REL_MAX_DISTANCE = 2048
CONV_WIDTH = 3
RMS_EPS = 1e-6
NEG_INF = -1e30
N_MIXERS = 2

SB_EXIT_LOG = -104.0

CONV_HALO = 16
V7X_VMEM_BYTES = 64 * 1024 * 1024
VMEM_LIMIT = V7X_VMEM_BYTES * 7 // 8

BF16 = jnp.bfloat16
F32 = jnp.float32


def _params(semantics):
    return pltpu.CompilerParams(dimension_semantics=semantics, vmem_limit_bytes=VMEM_LIMIT)


def _rms_rows(x, g):
    ms = jnp.mean(x * x, axis=-1, keepdims=True)
    return x * lax.rsqrt(ms + RMS_EPS) * g


def _norm_into(x_ref, g_ref, dst_ref, dst_row0, chunk):
    g = g_ref[...]

    def body(c, carry):
        r = pl.multiple_of(c * chunk, chunk)
        y = _rms_rows(x_ref[pl.ds(r, chunk), :], g)
        dst_ref[pl.ds(dst_row0 + r, chunk), :] = y.astype(dst_ref.dtype)
        return carry

    lax.fori_loop(0, x_ref.shape[0] // chunk, body, 0)


def _norm_matmul_kernel(x_ref, g_ref, w_ref, o_ref, hn_ref):
    @pl.when(pl.program_id(1) == 0)
    def _():
        _norm_into(x_ref, g_ref, hn_ref, 0, 128)

    o_ref[...] = jnp.dot(hn_ref[...], w_ref[...], preferred_element_type=F32).astype(o_ref.dtype)


def _norm_matmul(x, g, w, tm, tn):
    M, K = x.shape
    N = w.shape[1]
    return pl.pallas_call(
        _norm_matmul_kernel,
        grid=(M // tm, N // tn),
        in_specs=[
            pl.BlockSpec((tm, K), lambda i, j: (i, 0)),
            pl.BlockSpec((1, K), lambda i, j: (0, 0)),
            pl.BlockSpec((K, tn), lambda i, j: (0, j)),
        ],
        out_specs=pl.BlockSpec((tm, tn), lambda i, j: (i, j)),
        out_shape=jax.ShapeDtypeStruct((M, N), BF16),
        scratch_shapes=[pltpu.VMEM((tm, K), BF16)],
        compiler_params=_params(("parallel", "arbitrary")),
        name="norm_matmul",
    )(x, g.reshape(1, K), w)


def _matmul_res_kernel(a_ref, w_ref, r_ref, o_ref):
    o_ref[...] = r_ref[...] + jnp.dot(a_ref[...], w_ref[...], preferred_element_type=F32)


def _matmul_res(a, w, res, tm, tn):
    M, K = a.shape
    N = w.shape[1]
    return pl.pallas_call(
        _matmul_res_kernel,
        grid=(M // tm, N // tn),
        in_specs=[
            pl.BlockSpec((tm, K), lambda i, j: (i, 0)),
            pl.BlockSpec((K, tn), lambda i, j: (0, j)),
            pl.BlockSpec((tm, tn), lambda i, j: (i, j)),
        ],
        out_specs=pl.BlockSpec((tm, tn), lambda i, j: (i, j)),
        out_shape=jax.ShapeDtypeStruct((M, N), F32),
        compiler_params=_params(("parallel", "parallel")),
        name="matmul_res",
    )(a, w, res)


def _sb_kernel(q_ref, k_ref, v_ref, o_ref, acc_ref, later_ref, *, sub_blocks):
    scale = HEAD_DIM ** -0.5
    row = lax.broadcasted_iota(jnp.int32, (BLOCK, BLOCK), 0)
    col = lax.broadcasted_iota(jnp.int32, (BLOCK, BLOCK), 1)
    causal = col < row
    suffix = (row > col).astype(BF16)

    def one_query_block(jb, carry):
        qb = pl.program_id(2) * sub_blocks + jb
        q = q_ref[pl.ds(pl.multiple_of(jb * BLOCK, BLOCK), BLOCK), :]
        acc_ref[...] = jnp.zeros_like(acc_ref)
        later_ref[...] = jnp.zeros_like(later_ref)

        def key_block(kb, masked):
            start = pl.multiple_of(kb * BLOCK, BLOCK)
            k = k_ref[pl.ds(start, BLOCK), :]
            v = v_ref[pl.ds(start, BLOCK), :]
            z = lax.dot_general(q, k, (((1,), (1,)), ((), ())), preferred_element_type=F32) * scale
            log1m = -(jnp.maximum(z, 0.0) + jnp.log1p(jnp.exp(-jnp.abs(z))))
            log_beta = z + log1m
            if masked:
                log1m = jnp.where(causal, log1m, 0.0)
            within = jnp.dot(log1m.astype(BF16), suffix, preferred_element_type=F32)
            a = jnp.exp(log_beta + within + later_ref[...])
            if masked:
                a = jnp.where(causal, a, 0.0)
            acc_ref[...] += jnp.dot(a.astype(BF16), v, preferred_element_type=F32)
            later_ref[...] += jnp.sum(log1m, axis=-1, keepdims=True)

        key_block(qb, True)

        def cond(c):
            kb, worst = c
            return jnp.logical_and(kb >= 0, worst > SB_EXIT_LOG)

        def body(c):
            kb, _ = c
            key_block(kb, False)
            return kb - 1, jnp.max(later_ref[...])

        lax.while_loop(cond, body, (qb - 1, jnp.max(later_ref[...])))
        o_ref[pl.ds(pl.multiple_of(jb * BLOCK, BLOCK), BLOCK), :] = acc_ref[...].astype(o_ref.dtype)
        return carry

    lax.fori_loop(0, sub_blocks, one_query_block, 0)


def _stick_breaking(qkv, B, S, tq):
    H = SB_HEADS
    nq = S // tq
    return pl.pallas_call(
        functools.partial(_sb_kernel, sub_blocks=tq // BLOCK),
        grid=(B, H, nq),
        in_specs=[
            pl.BlockSpec((tq, HEAD_DIM), lambda b, h, i: (b * nq + i, h)),
            pl.BlockSpec((S, HEAD_DIM), lambda b, h, i: (b, H + h)),
            pl.BlockSpec((S, HEAD_DIM), lambda b, h, i: (b, 2 * H + h)),
        ],
        out_specs=pl.BlockSpec((tq, HEAD_DIM), lambda b, h, i: (b * nq + i, h)),
        out_shape=jax.ShapeDtypeStruct((B * S, H * HEAD_DIM), BF16),
        scratch_shapes=[pltpu.VMEM((BLOCK, HEAD_DIM), F32), pltpu.VMEM((BLOCK, 1), F32)],
        compiler_params=_params(("parallel", "parallel", "arbitrary")),
        name="stick_breaking",
    )(qkv, qkv, qkv)


def _rel_bucket_table(dil):
    qi = jnp.arange(BLOCK)[:, None]
    kc = jnp.arange(2 * BLOCK)[None, :]
    n = jnp.maximum(BLOCK + qi - kc, 0) * dil
    max_exact = REL_BUCKETS // 2
    nf = jnp.maximum(n, 1).astype(F32)
    large = max_exact + (jnp.log(nf / max_exact) / math.log(REL_MAX_DISTANCE / max_exact)
                         * (REL_BUCKETS - max_exact)).astype(jnp.int32)
    large = jnp.minimum(large, REL_BUCKETS - 1)
    return jnp.where(n < max_exact, n, large).astype(jnp.int32)


def _dil_kernel(bucket_ref, table_ref, q_ref, kc_ref, kp_ref, vc_ref, vp_ref, o_ref, lse_ref, bias_ref,
                *, sub_blocks):
    scale = HEAD_DIM ** -0.5
    first_step = jnp.logical_and(jnp.logical_and(pl.program_id(0) == 0, pl.program_id(1) == 0),
                                 pl.program_id(2) == 0)

    @pl.when(first_step)
    def _():
        bucket = bucket_ref[...]
        for h in range(DIL_GROUP_HEADS):
            b = jnp.zeros(bucket.shape, F32)
            for t in range(REL_BUCKETS):
                b = jnp.where(bucket == t, table_ref[t, h], b)
            bias_ref[h] = b

    row = lax.broadcasted_iota(jnp.int32, (BLOCK, BLOCK), 0)
    col = lax.broadcasted_iota(jnp.int32, (BLOCK, BLOCK), 1)
    lane = lax.broadcasted_iota(jnp.int32, (BLOCK, 128), 1)
    band_cur = col <= row
    band_prev = col >= row
    has_prev_tile = pl.program_id(2) > 0
    contract_last = (((1,), (1,)), ((), ()))

    for j in range(sub_blocks):
        rows = slice(j * BLOCK, (j + 1) * BLOCK)
        stats = jnp.zeros((BLOCK, 128), F32)
        for h in range(DIL_GROUP_HEADS):
            cols = slice(h * HEAD_DIM, (h + 1) * HEAD_DIM)
            q = q_ref[rows, cols]
            k_cur, v_cur = kc_ref[rows, cols], vc_ref[rows, cols]
            if j == 0:
                k_prev, v_prev = kp_ref[:, cols], vp_ref[:, cols]
                mask_prev = jnp.logical_and(band_prev, has_prev_tile)
            else:
                prev_rows = slice((j - 1) * BLOCK, j * BLOCK)
                k_prev, v_prev = kc_ref[prev_rows, cols], vc_ref[prev_rows, cols]
                mask_prev = band_prev
            s_prev = lax.dot_general(q, k_prev, contract_last, preferred_element_type=F32) * scale
            s_cur = lax.dot_general(q, k_cur, contract_last, preferred_element_type=F32) * scale
            s_prev = jnp.where(mask_prev, s_prev + bias_ref[h, :, 0:BLOCK], NEG_INF)
            s_cur = jnp.where(band_cur, s_cur + bias_ref[h, :, BLOCK:2 * BLOCK], NEG_INF)
            m = jnp.maximum(jnp.max(s_prev, axis=-1, keepdims=True), jnp.max(s_cur, axis=-1, keepdims=True))
            p_prev = jnp.exp(s_prev - m)
            p_cur = jnp.exp(s_cur - m)
            l = jnp.sum(p_prev, axis=-1, keepdims=True) + jnp.sum(p_cur, axis=-1, keepdims=True)
            o = (jnp.dot(p_prev.astype(BF16), v_prev, preferred_element_type=F32)
                 + jnp.dot(p_cur.astype(BF16), v_cur, preferred_element_type=F32))
            o_ref[rows, cols] = (o / l).astype(o_ref.dtype)
            stats = jnp.where(lane == h, m + jnp.log(l), stats)
        lse_ref[rows, :] = stats


def _dilated_group(qkv, rel_table, B, S, g, dil, tm):
    M = S // dil
    tm = min(tm, M)
    nm = M // tm
    sub = tm // BLOCK
    ncol = DIL_GROUPS * 3
    qkv_v = qkv.reshape(B * M, dil * ncol * DIL_WIDTH)

    def cur(c):
        return pl.BlockSpec((tm, DIL_WIDTH), lambda b, r, i: (b * nm + i, r * ncol + g * 3 + c))

    def prev(c):
        return pl.BlockSpec((BLOCK, DIL_WIDTH),
                            lambda b, r, i: (b * (M // BLOCK) + jnp.maximum(i * sub - 1, 0), r * ncol + g * 3 + c))

    o, lse = pl.pallas_call(
        functools.partial(_dil_kernel, sub_blocks=sub),
        grid=(B, dil, nm),
        in_specs=[
            pl.BlockSpec((BLOCK, 2 * BLOCK), lambda b, r, i: (0, 0)),
            pl.BlockSpec(memory_space=pltpu.SMEM),
            cur(0), cur(1), prev(1), cur(2), prev(2),
        ],
        out_specs=[
            pl.BlockSpec((tm, DIL_WIDTH), lambda b, r, i: (b * nm + i, r)),
            pl.BlockSpec((tm, 128), lambda b, r, i: (b * nm + i, r)),
        ],
        out_shape=[
            jax.ShapeDtypeStruct((B * M, dil * DIL_WIDTH), BF16),
            jax.ShapeDtypeStruct((B * M, dil * 128), F32),
        ],
        scratch_shapes=[pltpu.VMEM((DIL_GROUP_HEADS, BLOCK, 2 * BLOCK), F32)],
        compiler_params=_params(("arbitrary", "arbitrary", "arbitrary")),
        name=f"dilated_group{g}",
    )(_rel_bucket_table(dil), rel_table, qkv_v, qkv_v, qkv_v, qkv_v, qkv_v)
    return o.reshape(B * S, DIL_WIDTH), lse.reshape(B * S, 128)


def _combine_matmul_res_kernel(o0_ref, o1_ref, o2_ref, l0_ref, l1_ref, l2_ref, w_ref, r_ref, out_ref, comb_ref):
    @pl.when(pl.program_id(1) == 0)
    def _():
        l0, l1, l2 = l0_ref[...], l1_ref[...], l2_ref[...]
        mx = jnp.maximum(jnp.maximum(l0, l1), l2)
        e0, e1, e2 = jnp.exp(l0 - mx), jnp.exp(l1 - mx), jnp.exp(l2 - mx)
        den = e0 + e1 + e2
        w0, w1, w2 = e0 / den, e1 / den, e2 / den
        for h in range(DIL_GROUP_HEADS):
            cols = slice(h * HEAD_DIM, (h + 1) * HEAD_DIM)
            c = (w0[:, h:h + 1] * o0_ref[:, cols].astype(F32)
                 + w1[:, h:h + 1] * o1_ref[:, cols].astype(F32)
                 + w2[:, h:h + 1] * o2_ref[:, cols].astype(F32))
            comb_ref[:, cols] = c.astype(comb_ref.dtype)

    out_ref[...] = r_ref[...] + jnp.dot(comb_ref[...], w_ref[...], preferred_element_type=F32)


def _combine_matmul_res(os, lses, w, res, tm, tn):
    M, K = os[0].shape
    N = w.shape[1]
    o_spec = pl.BlockSpec((tm, K), lambda i, j: (i, 0))
    l_spec = pl.BlockSpec((tm, 128), lambda i, j: (i, 0))
    return pl.pallas_call(
        _combine_matmul_res_kernel,
        grid=(M // tm, N // tn),
        in_specs=[o_spec, o_spec, o_spec, l_spec, l_spec, l_spec,
                  pl.BlockSpec((K, tn), lambda i, j: (0, j)),
                  pl.BlockSpec((tm, tn), lambda i, j: (i, j))],
        out_specs=pl.BlockSpec((tm, tn), lambda i, j: (i, j)),
        out_shape=jax.ShapeDtypeStruct((M, N), F32),
        scratch_shapes=[pltpu.VMEM((tm, K), BF16)],
        compiler_params=_params(("parallel", "arbitrary")),
        name="combine_matmul_res",
    )(*os, *lses, w, res)


def _ffn_up_kernel(x_ref, xp_ref, g_ref, wg_ref, wv_ref, cw_ref, cb_ref, o_ref, hn_ref, gate_ref,
                   *, tiles_per_seq):
    tm = x_ref.shape[0]

    @pl.when(pl.program_id(1) == 0)
    def _():
        halo = _rms_rows(xp_ref[...], g_ref[...])
        halo = jnp.where(pl.program_id(0) % tiles_per_seq == 0, 0.0, halo)
        hn_ref[0:CONV_HALO, :] = halo.astype(hn_ref.dtype)
        _norm_into(x_ref, g_ref, hn_ref, CONV_HALO, 128)

    gate_ref[...] = jnp.dot(hn_ref[...], wg_ref[...], preferred_element_type=F32)
    val = jnp.dot(hn_ref[CONV_HALO:, :], wv_ref[...], preferred_element_type=F32)
    conv = cb_ref[...]
    for j in range(CONV_WIDTH):
        off = CONV_HALO - (CONV_WIDTH - 1 - j)
        conv = conv + gate_ref[off:off + tm, :] * cw_ref[j:j + 1, :]
    act = conv / (1.0 + jnp.exp(-conv)) * val
    o_ref[...] = act.astype(o_ref.dtype)


def _ffn_up(x, g, w_up, conv_w, conv_b, S, tm, tf):
    M, K = x.shape
    d_ff = w_up.shape[1] // 2
    nf = d_ff // tf
    halo_blocks_per_tile = tm // CONV_HALO
    return pl.pallas_call(
        functools.partial(_ffn_up_kernel, tiles_per_seq=S // tm),
        grid=(M // tm, nf),
        in_specs=[
            pl.BlockSpec((tm, K), lambda i, j: (i, 0)),
            pl.BlockSpec((CONV_HALO, K), lambda i, j: (jnp.maximum(i * halo_blocks_per_tile - 1, 0), 0)),
            pl.BlockSpec((1, K), lambda i, j: (0, 0)),
            pl.BlockSpec((K, tf), lambda i, j: (0, j)),
            pl.BlockSpec((K, tf), lambda i, j: (0, nf + j)),
            pl.BlockSpec((CONV_WIDTH, tf), lambda i, j: (0, j)),
            pl.BlockSpec((1, tf), lambda i, j: (0, j)),
        ],
        out_specs=pl.BlockSpec((tm, tf), lambda i, j: (i, j)),
        out_shape=jax.ShapeDtypeStruct((M, d_ff), BF16),
        scratch_shapes=[pltpu.VMEM((tm + CONV_HALO, K), BF16), pltpu.VMEM((tm + CONV_HALO, tf), F32)],
        compiler_params=_params(("parallel", "arbitrary")),
        name="ffn_up",
    )(x, x, g.reshape(1, K), w_up, w_up, conv_w, conv_b.reshape(1, d_ff))


def _rmsnorm_kernel(x_ref, g_ref, o_ref):
    o_ref[...] = _rms_rows(x_ref[...], g_ref[...])


def _rmsnorm(x, g, tm):
    M, K = x.shape
    return pl.pallas_call(
        _rmsnorm_kernel,
        grid=(M // tm,),
        in_specs=[pl.BlockSpec((tm, K), lambda i: (i, 0)), pl.BlockSpec((1, K), lambda i: (0, 0))],
        out_specs=pl.BlockSpec((tm, K), lambda i: (i, 0)),
        out_shape=jax.ShapeDtypeStruct((M, K), F32),
        compiler_params=_params(("parallel",)),
        name="final_rmsnorm",
    )(x, g.reshape(1, K))


def kernel(x, ln_mix, ln_ffn, ln_f, w_qkv_sb, w_o_sb, w_qkv_dil, w_o_dil, rel_bias, w_up, conv_w, conv_b, w_down):
    B, S, D = x.shape
    depth = ln_mix.shape[0]
    h = x.reshape(B * S, D)
    for i in range(depth):
        j = i // N_MIXERS
        if i % N_MIXERS == 0:
            qkv = _norm_matmul(h, ln_mix[i], w_qkv_sb[j].astype(BF16), tm=1024, tn=1024)
            o = _stick_breaking(qkv, B, S, tq=512)
            h = _matmul_res(o, w_o_sb[j].astype(BF16), h, tm=1024, tn=1024)
        else:
            qkv = _norm_matmul(h, ln_mix[i], w_qkv_dil[j].astype(BF16), tm=1024, tn=1024)
            os, lses = [], []
            for g, (_, dil) in enumerate(DIL_PATTERNS):
                table = rel_bias[:, g * DIL_GROUP_HEADS:(g + 1) * DIL_GROUP_HEADS]
                o, lse = _dilated_group(qkv, table, B, S, g, dil, tm=512)
                os.append(o)
                lses.append(lse)
            h = _combine_matmul_res(os, lses, w_o_dil[j].astype(BF16), h, tm=512, tn=1024)
        act = _ffn_up(h, ln_ffn[i], w_up[i].astype(BF16), conv_w[i], conv_b[i], S, tm=1024, tf=512)
        h = _matmul_res(act, w_down[i].astype(BF16), h, tm=1024, tn=512)
    return _rmsnorm(h, ln_f, tm=512).reshape(B, S, D)
```

```python
import functools
import math

import jax
import jax.numpy as jnp
import numpy as np
from jax import lax
from jax.experimental import pallas as pl
from jax.experimental.pallas import tpu as pltpu

BLOCK = 128
SB_HEADS = 16
HEAD_DIM = 128
DIL_PATTERNS = ((128, 1), (512, 4), (2048, 16))
DIL_GROUPS = len(DIL_PATTERNS)
DIL_GROUP_HEADS = 8
DIL_WIDTH = DIL_GROUP_HEADS * HEAD_DIM
REL_BUCKETS = 32
REL_MAX_DISTANCE = 2048
CONV_WIDTH = 3
RMS_EPS = 1e-6
NEG_INF = -1e30
N_MIXERS = 2

SB_EXIT_LOG = -104.0
SB_STATIC_DEPTH = 3

DIL_TILE = 1024
DIL_RES = 16
DIL_CHUNK = DIL_TILE // DIL_RES
CONV_HALO = 16
V7X_VMEM_BYTES = 64 * 1024 * 1024
VMEM_LIMIT = V7X_VMEM_BYTES * 7 // 8

BF16 = jnp.bfloat16
F32 = jnp.float32


def _params(semantics):
    return pltpu.CompilerParams(dimension_semantics=semantics, vmem_limit_bytes=VMEM_LIMIT)


def _rms_rows(x, g):
    ms = jnp.mean(x * x, axis=-1, keepdims=True)
    return x * lax.rsqrt(ms + RMS_EPS) * g


def _norm_into(x_ref, g_ref, dst_ref, dst_row0, chunk):
    g = g_ref[...]

    def body(c, carry):
        r = pl.multiple_of(c * chunk, chunk)
        y = _rms_rows(x_ref[pl.ds(r, chunk), :], g)
        dst_ref[pl.ds(dst_row0 + r, chunk), :] = y.astype(dst_ref.dtype)
        return carry

    lax.fori_loop(0, x_ref.shape[0] // chunk, body, 0)


def _norm_matmul_kernel(x_ref, g_ref, w_ref, o_ref, hn_ref):
    @pl.when(pl.program_id(1) == 0)
    def _():
        _norm_into(x_ref, g_ref, hn_ref, 0, 128)

    o_ref[...] = jnp.dot(hn_ref[...], w_ref[...], preferred_element_type=F32).astype(o_ref.dtype)


def _norm_matmul(x, g, w, tm, tn):
    M, K = x.shape
    N = w.shape[1]
    return pl.pallas_call(
        _norm_matmul_kernel,
        grid=(M // tm, N // tn),
        in_specs=[
            pl.BlockSpec((tm, K), lambda i, j: (i, 0)),
            pl.BlockSpec((1, K), lambda i, j: (0, 0)),
            pl.BlockSpec((K, tn), lambda i, j: (0, j)),
        ],
        out_specs=pl.BlockSpec((tm, tn), lambda i, j: (i, j)),
        out_shape=jax.ShapeDtypeStruct((M, N), BF16),
        scratch_shapes=[pltpu.VMEM((tm, K), BF16)],
        compiler_params=_params(("parallel", "arbitrary")),
        name="norm_matmul",
    )(x, g.reshape(1, K), w)


def _matmul_res_kernel(a_ref, w_ref, r_ref, o_ref):
    o_ref[...] = r_ref[...] + jnp.dot(a_ref[...], w_ref[...], preferred_element_type=F32)


def _matmul_res(a, w, res, tm, tn):
    M, K = a.shape
    N = w.shape[1]
    return pl.pallas_call(
        _matmul_res_kernel,
        grid=(M // tm, N // tn),
        in_specs=[
            pl.BlockSpec((tm, K), lambda i, j: (i, 0)),
            pl.BlockSpec((K, tn), lambda i, j: (0, j)),
            pl.BlockSpec((tm, tn), lambda i, j: (i, j)),
        ],
        out_specs=pl.BlockSpec((tm, tn), lambda i, j: (i, j)),
        out_shape=jax.ShapeDtypeStruct((M, N), F32),
        compiler_params=_params(("parallel", "parallel")),
        name="matmul_res",
    )(a, w, res)


def _sb_kernel(q_ref, k_ref, v_ref, o_ref, acc_ref, later_ref, *, chains):
    scale = HEAD_DIM ** -0.5

    def causal_mask(n_rows):
        row = lax.broadcasted_iota(jnp.int32, (n_rows, BLOCK), 0)
        col = lax.broadcasted_iota(jnp.int32, (n_rows, BLOCK), 1)
        return jnp.logical_or(row >= BLOCK, col < row)

    causal = {n: causal_mask(n * BLOCK) for n in range(1, SB_STATIC_DEPTH + 1)}
    suffix = (lax.broadcasted_iota(jnp.int32, (BLOCK, BLOCK), 0)
              > lax.broadcasted_iota(jnp.int32, (BLOCK, BLOCK), 1)).astype(BF16)
    base = pl.program_id(2) * chains
    contract_last = (((1,), (1,)), ((), ()))

    def key_block(rows, kb, mask):
        start = pl.multiple_of(kb * BLOCK, BLOCK)
        k = k_ref[pl.ds(start, BLOCK), :]
        v = v_ref[pl.ds(start, BLOCK), :]
        z = lax.dot_general(q_ref[rows, :], k, contract_last, preferred_element_type=F32) * scale
        log1m = -(jnp.maximum(z, 0.0) + jnp.log1p(jnp.exp(-jnp.abs(z))))
        log_beta = z + log1m
        if mask is not None:
            log1m = jnp.where(mask, log1m, 0.0)
        within = jnp.dot(log1m.astype(BF16), suffix, preferred_element_type=F32)
        a = jnp.exp(log_beta + within + later_ref[rows, :])
        if mask is not None:
            a = jnp.where(mask, a, 0.0)
        acc_ref[rows, :] += jnp.dot(a.astype(BF16), v, preferred_element_type=F32)
        later_ref[rows, :] += jnp.sum(log1m, axis=-1, keepdims=True)

    def retire_if_no_keys(rows, kb):
        later_ref[rows, :] = jnp.where(kb >= 0, later_ref[rows, :], NEG_INF)

    acc_ref[...] = jnp.zeros_like(acc_ref)
    later_ref[...] = jnp.zeros_like(later_ref)
    for first in range(chains - 1, -SB_STATIC_DEPTH, -1):
        lo, hi = max(first, 0), min(first + SB_STATIC_DEPTH - 1, chains - 1)
        rows = slice(lo * BLOCK, (hi + 1) * BLOCK)
        kb = base + first
        if first >= 0:
            key_block(rows, kb, causal[hi - lo + 1])
        else:
            retire_if_no_keys(rows, kb)
            key_block(rows, jnp.maximum(kb, 0), None)

    def cond(carry):
        _, worst = carry
        return worst > SB_EXIT_LOG

    def body(carry):
        delta, _ = carry
        for c in range(chains):
            rows = slice(c * BLOCK, (c + 1) * BLOCK)
            kb = base + c - delta
            retire_if_no_keys(rows, kb)
            key_block(rows, jnp.maximum(kb, 0), None)
        return delta + 1, jnp.max(later_ref[...])

    lax.while_loop(cond, body, (jnp.int32(SB_STATIC_DEPTH), jnp.max(later_ref[...])))
    o_ref[...] = acc_ref[...].astype(o_ref.dtype)


def _stick_breaking(qkv, B, S, tq):
    H = SB_HEADS
    nq = S // tq
    return pl.pallas_call(
        functools.partial(_sb_kernel, chains=tq // BLOCK),
        grid=(B, H, nq),
        in_specs=[
            pl.BlockSpec((tq, HEAD_DIM), lambda b, h, i: (b * nq + i, h)),
            pl.BlockSpec((S, HEAD_DIM), lambda b, h, i: (b, H + h)),
            pl.BlockSpec((S, HEAD_DIM), lambda b, h, i: (b, 2 * H + h)),
        ],
        out_specs=pl.BlockSpec((tq, HEAD_DIM), lambda b, h, i: (b * nq + i, h)),
        out_shape=jax.ShapeDtypeStruct((B * S, H * HEAD_DIM), BF16),
        scratch_shapes=[pltpu.VMEM((tq, HEAD_DIM), F32), pltpu.VMEM((tq, 1), F32)],
        compiler_params=_params(("parallel", "parallel", "arbitrary")),
        name="stick_breaking",
    )(qkv, qkv, qkv)


def _residue_major_permutation():
    i = np.arange(DIL_TILE)
    p = np.zeros((DIL_TILE, DIL_TILE), np.float32)
    p[i, (i % DIL_CHUNK) * DIL_RES + i // DIL_CHUNK] = 1.0
    return p


def _norm_matmul_perm_kernel(x_ref, g_ref, p_ref, w_ref, o_ref, hn_ref, hnp_ref, *, natural_tiles, chunk):
    j = pl.program_id(1)

    @pl.when(j == 0)
    def _():
        _norm_into(x_ref, g_ref, hn_ref, 0, 128)
        for c in range(hn_ref.shape[1] // chunk):
            cols = slice(c * chunk, (c + 1) * chunk)
            hnp_ref[:, cols] = jnp.dot(p_ref[...], hn_ref[:, cols],
                                       preferred_element_type=F32).astype(hnp_ref.dtype)

    @pl.when(j < natural_tiles)
    def _():
        o_ref[...] = jnp.dot(hn_ref[...], w_ref[...], preferred_element_type=F32).astype(o_ref.dtype)

    @pl.when(j >= natural_tiles)
    def _():
        o_ref[...] = jnp.dot(hnp_ref[...], w_ref[...], preferred_element_type=F32).astype(o_ref.dtype)


def _norm_matmul_perm(x, g, w, perm, natural_cols, tn):
    M, K = x.shape
    N = w.shape[1]
    tm = DIL_TILE
    return pl.pallas_call(
        functools.partial(_norm_matmul_perm_kernel, natural_tiles=natural_cols // tn, chunk=512),
        grid=(M // tm, N // tn),
        in_specs=[
            pl.BlockSpec((tm, K), lambda i, j: (i, 0)),
            pl.BlockSpec((1, K), lambda i, j: (0, 0)),
            pl.BlockSpec((tm, tm), lambda i, j: (0, 0)),
            pl.BlockSpec((K, tn), lambda i, j: (0, j)),
        ],
        out_specs=pl.BlockSpec((tm, tn), lambda i, j: (i, j)),
        out_shape=jax.ShapeDtypeStruct((M, N), BF16),
        scratch_shapes=[pltpu.VMEM((tm, K), BF16), pltpu.VMEM((tm, K), BF16)],
        compiler_params=_params(("parallel", "arbitrary")),
        name="norm_matmul_perm",
    )(x, g.reshape(1, K), perm, w)


def _rel_bucket_table(dil, order):
    order = np.asarray(order)
    qi = order[:, None]
    kc = np.concatenate([order, BLOCK + order])[None, :]
    rel = BLOCK + qi - kc
    n = jnp.asarray(np.maximum(rel, 0) * dil)
    max_exact = REL_BUCKETS // 2
    nf = jnp.maximum(n, 1).astype(F32)
    large = max_exact + (jnp.log(nf / max_exact) / math.log(REL_MAX_DISTANCE / max_exact)
                         * (REL_BUCKETS - max_exact)).astype(jnp.int32)
    large = jnp.minimum(large, REL_BUCKETS - 1)
    bucket = jnp.where(n < max_exact, n, large).astype(jnp.int32)
    return jnp.where(jnp.asarray((rel >= 0) & (rel <= BLOCK)), bucket, -1)


class _NaturalRows:
    @staticmethod
    def load(ref, j, cols):
        return ref[j * BLOCK:(j + 1) * BLOCK, cols]

    @staticmethod
    def load_prev(ref, cols):
        return ref[:, cols]

    @staticmethod
    def store(ref, j, cols, val):
        ref[j * BLOCK:(j + 1) * BLOCK, cols] = val


class _Residue16Rows:
    per = BLOCK // DIL_CHUNK

    @classmethod
    def load(cls, ref, j, cols):
        return jnp.concatenate([ref[cls.per * j + t, :, cols] for t in range(cls.per)], axis=0)

    @classmethod
    def load_prev(cls, ref, cols):
        return jnp.concatenate([ref[t, :, cols] for t in range(cls.per)], axis=0)

    @classmethod
    def store(cls, ref, j, cols, val):
        for t in range(cls.per):
            ref[cls.per * j + t, :, cols] = val[t * DIL_CHUNK:(t + 1) * DIL_CHUNK]


class _Residue4Rows:
    sub = BLOCK // 4
    per_tile = DIL_CHUNK // sub
    order = [4 * m + a for a in range(4) for m in range(BLOCK // 4)]

    @classmethod
    def _rows(cls, j):
        return j // cls.per_tile, slice((j % cls.per_tile) * cls.sub, (j % cls.per_tile + 1) * cls.sub)

    @classmethod
    def load(cls, ref, j, cols):
        t, ms = cls._rows(j)
        return jnp.concatenate([ref[t, a, ms, cols] for a in range(4)], axis=0)

    @classmethod
    def load_prev(cls, ref, cols):
        return cls.load(ref, cls.per_tile - 1, cols)

    @classmethod
    def store(cls, ref, j, cols, val):
        t, ms = cls._rows(j)
        for a in range(4):
            ref[t, a, ms, cols] = val[a * cls.sub:(a + 1) * cls.sub]


def _dil_kernel(bucket_ref, table_ref, q_ref, kc_ref, kp_ref, vc_ref, vp_ref, o_ref, lse_ref, bias_ref,
                *, sub_blocks, rows):
    scale = HEAD_DIM ** -0.5
    first_step = jnp.logical_and(jnp.logical_and(pl.program_id(0) == 0, pl.program_id(1) == 0),
                                 pl.program_id(2) == 0)

    @pl.when(first_step)
    def _():
        bucket = bucket_ref[...]
        for h in range(DIL_GROUP_HEADS):
            b = jnp.full(bucket.shape, NEG_INF, F32)
            for t in range(REL_BUCKETS):
                b = jnp.where(bucket == t, table_ref[t, h], b)
            bias_ref[h] = b

    lane = lax.broadcasted_iota(jnp.int32, (BLOCK, 128), 1)
    has_prev_tile = pl.program_id(2) > 0
    contract_last = (((1,), (1,)), ((), ()))
    all_lanes = slice(None)

    for j in range(sub_blocks):
        stats = jnp.zeros((BLOCK, 128), F32)
        for h in range(DIL_GROUP_HEADS):
            cols = slice(h * HEAD_DIM, (h + 1) * HEAD_DIM)
            q = rows.load(q_ref, j, cols)
            k_cur, v_cur = rows.load(kc_ref, j, cols), rows.load(vc_ref, j, cols)
            if j == 0:
                k_prev, v_prev = rows.load_prev(kp_ref, cols), rows.load_prev(vp_ref, cols)
            else:
                k_prev, v_prev = rows.load(kc_ref, j - 1, cols), rows.load(vc_ref, j - 1, cols)
            s_prev = (lax.dot_general(q, k_prev, contract_last, preferred_element_type=F32) * scale
                      + bias_ref[h, :, 0:BLOCK])
            s_cur = (lax.dot_general(q, k_cur, contract_last, preferred_element_type=F32) * scale
                     + bias_ref[h, :, BLOCK:2 * BLOCK])
            if j == 0:
                s_prev = jnp.where(has_prev_tile, s_prev, NEG_INF)
            m = jnp.maximum(jnp.max(s_prev, axis=-1, keepdims=True), jnp.max(s_cur, axis=-1, keepdims=True))
            p_prev = jnp.exp(s_prev - m)
            p_cur = jnp.exp(s_cur - m)
            l = jnp.sum(p_prev, axis=-1, keepdims=True) + jnp.sum(p_cur, axis=-1, keepdims=True)
            o = (jnp.dot(p_prev.astype(BF16), v_prev, preferred_element_type=F32)
                 + jnp.dot(p_cur.astype(BF16), v_cur, preferred_element_type=F32))
            rows.store(o_ref, j, cols, (o / l).astype(o_ref.dtype))
            stats = jnp.where(lane == h, m + jnp.log(l), stats)
        rows.store(lse_ref, j, all_lanes, stats)


def _dilated_group(qkv, rel_table, B, S, g, dil):
    W, C = DIL_WIDTH, qkv.shape[1]
    tiles = S // DIL_TILE
    col0 = g * 3
    if dil == 1:
        rows, n_res = _NaturalRows, 1
        blocks = min(4, S // BLOCK)
        steps = S // (blocks * BLOCK)
        views = [(B * S, C), (B * S, W), (B * S, 128)]
        cur_shape, prev_shape = (blocks * BLOCK,), (BLOCK,)
        cur_idx = lambda b, r, i: (b * steps + i,)
        prev_idx = lambda b, r, i: (b * (S // BLOCK) + jnp.maximum(i * blocks - 1, 0),)
    elif dil == DIL_RES:
        rows, n_res = _Residue16Rows, DIL_RES
        per = _Residue16Rows.per
        t_step = min(4 * per, tiles)
        blocks, steps = t_step // per, tiles // t_step
        views = [(B * tiles, DIL_RES, DIL_CHUNK, c) for c in (C, W, 128)]
        cur_shape, prev_shape = (t_step, None, DIL_CHUNK), (per, None, DIL_CHUNK)
        cur_idx = lambda b, r, i: (b * steps + i, r, 0)
        prev_idx = lambda b, r, i: (b * (tiles // per) + jnp.maximum(i * blocks - 1, 0), r, 0)
    else:
        assert dil == 4 and DIL_RES == 16
        rows, n_res = _Residue4Rows, 4
        per_tile = _Residue4Rows.per_tile
        t_step = min(2, tiles)
        blocks, steps = t_step * per_tile, tiles // t_step
        views = [(B * tiles, 4, 4, DIL_CHUNK, c) for c in (C, W, 128)]
        cur_shape, prev_shape = (t_step, 4, None, DIL_CHUNK), (1, 4, None, DIL_CHUNK)
        cur_idx = lambda b, r, i: (b * steps + i, 0, r, 0)
        prev_idx = lambda b, r, i: (b * tiles + jnp.maximum(i * t_step - 1, 0), 0, r, 0)

    def spec(shape, idx, width, col):
        return pl.BlockSpec(shape + (width,), lambda b, r, i: idx(b, r, i) + (col,))

    qkv_v = qkv.reshape(views[0])
    order = getattr(rows, "order", list(range(BLOCK)))
    o, lse = pl.pallas_call(
        functools.partial(_dil_kernel, sub_blocks=blocks, rows=rows),
        grid=(B, n_res, steps),
        in_specs=[
            pl.BlockSpec((BLOCK, 2 * BLOCK), lambda b, r, i: (0, 0)),
            pl.BlockSpec(memory_space=pltpu.SMEM),
            spec(cur_shape, cur_idx, W, col0),
            spec(cur_shape, cur_idx, W, col0 + 1), spec(prev_shape, prev_idx, W, col0 + 1),
            spec(cur_shape, cur_idx, W, col0 + 2), spec(prev_shape, prev_idx, W, col0 + 2),
        ],
        out_specs=[spec(cur_shape, cur_idx, W, 0), spec(cur_shape, cur_idx, 128, 0)],
        out_shape=[jax.ShapeDtypeStruct(views[1], BF16), jax.ShapeDtypeStruct(views[2], F32)],
        scratch_shapes=[pltpu.VMEM((DIL_GROUP_HEADS, BLOCK, 2 * BLOCK), F32)],
        compiler_params=_params(("arbitrary", "arbitrary", "arbitrary")),
        name=f"dilated_group{g}",
    )(_rel_bucket_table(dil, order), rel_table, qkv_v, qkv_v, qkv_v, qkv_v, qkv_v)
    return o.reshape(B * S, W), lse.reshape(B * S, 128)


def _unpermute_f32(pt, x):
    hi = x.astype(BF16)
    rest = x - hi.astype(F32)
    mid = rest.astype(BF16)
    lo = (rest - mid.astype(F32)).astype(BF16)
    return (jnp.dot(pt, hi, preferred_element_type=F32) + jnp.dot(pt, mid, preferred_element_type=F32)
            + jnp.dot(pt, lo, preferred_element_type=F32))


def _combine_matmul_res_kernel(o0_ref, o1_ref, o2_ref, l0_ref, l1_ref, l2_ref, pt_ref, w_ref, r_ref, out_ref,
                               comb_ref):
    @pl.when(pl.program_id(1) == 0)
    def _():
        pt = pt_ref[...]
        l0 = l0_ref[...]
        l1 = _unpermute_f32(pt, l1_ref[...])
        l2 = _unpermute_f32(pt, l2_ref[...])
        mx = jnp.maximum(jnp.maximum(l0, l1), l2)
        e0, e1, e2 = jnp.exp(l0 - mx), jnp.exp(l1 - mx), jnp.exp(l2 - mx)
        den = e0 + e1 + e2
        w0, w1, w2 = e0 / den, e1 / den, e2 / den
        pair = 2 * HEAD_DIM
        for hp in range(DIL_GROUP_HEADS // 2):
            pcols = slice(hp * pair, (hp + 1) * pair)
            o1 = jnp.dot(pt, o1_ref[:, pcols], preferred_element_type=F32)
            o2 = jnp.dot(pt, o2_ref[:, pcols], preferred_element_type=F32)
            for k in range(2):
                h = 2 * hp + k
                cols = slice(h * HEAD_DIM, (h + 1) * HEAD_DIM)
                sub = slice(k * HEAD_DIM, (k + 1) * HEAD_DIM)
                c = (w0[:, h:h + 1] * o0_ref[:, cols].astype(F32)
                     + w1[:, h:h + 1] * o1[:, sub] + w2[:, h:h + 1] * o2[:, sub])
                comb_ref[:, cols] = c.astype(comb_ref.dtype)

    out_ref[...] = r_ref[...] + jnp.dot(comb_ref[...], w_ref[...], preferred_element_type=F32)


def _combine_matmul_res(os, lses, unperm, w, res, tn):
    M, K = os[0].shape
    N = w.shape[1]
    tm = DIL_TILE
    o_spec = pl.BlockSpec((tm, K), lambda i, j: (i, 0))
    l_spec = pl.BlockSpec((tm, 128), lambda i, j: (i, 0))
    return pl.pallas_call(
        _combine_matmul_res_kernel,
        grid=(M // tm, N // tn),
        in_specs=[o_spec, o_spec, o_spec, l_spec, l_spec, l_spec,
                  pl.BlockSpec((tm, tm), lambda i, j: (0, 0)),
                  pl.BlockSpec((K, tn), lambda i, j: (0, j)),
                  pl.BlockSpec((tm, tn), lambda i, j: (i, j))],
        out_specs=pl.BlockSpec((tm, tn), lambda i, j: (i, j)),
        out_shape=jax.ShapeDtypeStruct((M, N), F32),
        scratch_shapes=[pltpu.VMEM((tm, K), BF16)],
        compiler_params=_params(("parallel", "arbitrary")),
        name="combine_matmul_res",
    )(*os, *lses, unperm, w, res)


def _ffn_up_kernel(x_ref, xp_ref, g_ref, wg_ref, wv_ref, cw_ref, cb_ref, o_ref, hn_ref, gate_ref,
                   *, tiles_per_seq):
    tm = x_ref.shape[0]

    @pl.when(pl.program_id(1) == 0)
    def _():
        halo = _rms_rows(xp_ref[...], g_ref[...])
        halo = jnp.where(pl.program_id(0) % tiles_per_seq == 0, 0.0, halo)
        hn_ref[0:CONV_HALO, :] = halo.astype(hn_ref.dtype)
        _norm_into(x_ref, g_ref, hn_ref, CONV_HALO, 128)

    gate_ref[...] = jnp.dot(hn_ref[...], wg_ref[...], preferred_element_type=F32)
    val = jnp.dot(hn_ref[CONV_HALO:, :], wv_ref[...], preferred_element_type=F32)
    conv = cb_ref[...]
    for j in range(CONV_WIDTH):
        off = CONV_HALO - (CONV_WIDTH - 1 - j)
        conv = conv + gate_ref[off:off + tm, :] * cw_ref[j:j + 1, :]
    act = conv / (1.0 + jnp.exp(-conv)) * val
    o_ref[...] = act.astype(o_ref.dtype)


def _ffn_up(x, g, w_up, conv_w, conv_b, S, tm, tf):
    M, K = x.shape
    d_ff = w_up.shape[1] // 2
    nf = d_ff // tf
    halo_blocks_per_tile = tm // CONV_HALO
    return pl.pallas_call(
        functools.partial(_ffn_up_kernel, tiles_per_seq=S // tm),
        grid=(M // tm, nf),
        in_specs=[
            pl.BlockSpec((tm, K), lambda i, j: (i, 0)),
            pl.BlockSpec((CONV_HALO, K), lambda i, j: (jnp.maximum(i * halo_blocks_per_tile - 1, 0), 0)),
            pl.BlockSpec((1, K), lambda i, j: (0, 0)),
            pl.BlockSpec((K, tf), lambda i, j: (0, j)),
            pl.BlockSpec((K, tf), lambda i, j: (0, nf + j)),
            pl.BlockSpec((CONV_WIDTH, tf), lambda i, j: (0, j)),
            pl.BlockSpec((1, tf), lambda i, j: (0, j)),
        ],
        out_specs=pl.BlockSpec((tm, tf), lambda i, j: (i, j)),
        out_shape=jax.ShapeDtypeStruct((M, d_ff), BF16),
        scratch_shapes=[pltpu.VMEM((tm + CONV_HALO, K), BF16), pltpu.VMEM((tm + CONV_HALO, tf), F32)],
        compiler_params=_params(("parallel", "arbitrary")),
        name="ffn_up",
    )(x, x, g.reshape(1, K), w_up, w_up, conv_w, conv_b.reshape(1, d_ff))


def _rmsnorm_kernel(x_ref, g_ref, o_ref):
    o_ref[...] = _rms_rows(x_ref[...], g_ref[...])


def _rmsnorm(x, g, tm):
    M, K = x.shape
    return pl.pallas_call(
        _rmsnorm_kernel,
        grid=(M // tm,),
        in_specs=[pl.BlockSpec((tm, K), lambda i: (i, 0)), pl.BlockSpec((1, K), lambda i: (0, 0))],
        out_specs=pl.BlockSpec((tm, K), lambda i: (i, 0)),
        out_shape=jax.ShapeDtypeStruct((M, K), F32),
        compiler_params=_params(("parallel",)),
        name="final_rmsnorm",
    )(x, g.reshape(1, K))


def kernel(x, ln_mix, ln_ffn, ln_f, w_qkv_sb, w_o_sb, w_qkv_dil, w_o_dil, rel_bias, w_up, conv_w, conv_b, w_down):
    B, S, D = x.shape
    depth = ln_mix.shape[0]
    h = x.reshape(B * S, D)
    for i in range(depth):
        j = i // N_MIXERS
        if i % N_MIXERS == 0:
            qkv = _norm_matmul(h, ln_mix[i], w_qkv_sb[j].astype(BF16), tm=1024, tn=1024)
            o = _stick_breaking(qkv, B, S, tq=min(2048, S))
            h = _matmul_res(o, w_o_sb[j].astype(BF16), h, tm=1024, tn=1024)
        else:
            perm = _residue_major_permutation()
            qkv = _norm_matmul_perm(h, ln_mix[i], w_qkv_dil[j].astype(BF16), jnp.asarray(perm, BF16),
                                    natural_cols=3 * DIL_WIDTH, tn=1024)
            os, lses = [], []
            for g, (window, dil) in enumerate(DIL_PATTERNS):
                assert window // dil == BLOCK and S % (dil * BLOCK) == 0
                table = rel_bias[:, g * DIL_GROUP_HEADS:(g + 1) * DIL_GROUP_HEADS]
                o, lse = _dilated_group(qkv, table, B, S, g, dil)
                os.append(o)
                lses.append(lse)
            h = _combine_matmul_res(os, lses, jnp.asarray(perm.T, BF16), w_o_dil[j].astype(BF16), h, tn=512)
        act = _ffn_up(h, ln_ffn[i], w_up[i].astype(BF16), conv_w[i], conv_b[i], S, tm=1024, tf=512)
        h = _matmul_res(act, w_down[i].astype(BF16), h, tm=1024, tn=512)
    return _rmsnorm(h, ln_f, tm=512).reshape(B, S, D)
```

```python
import functools
import math

import jax
import jax.numpy as jnp
import numpy as np
from jax import lax
from jax.experimental import pallas as pl
from jax.experimental.pallas import tpu as pltpu

BLOCK = 128
SB_HEADS = 16
HEAD_DIM = 128
DIL_PATTERNS = ((128, 1), (512, 4), (2048, 16))
DIL_GROUPS = len(DIL_PATTERNS)
DIL_GROUP_HEADS = 8
DIL_WIDTH = DIL_GROUP_HEADS * HEAD_DIM
REL_BUCKETS = 32
REL_MAX_DISTANCE = 2048
CONV_WIDTH = 3
RMS_EPS = 1e-6
NEG_INF = -1e30
N_MIXERS = 2

SB_EXIT_LOG = -104.0
SB_STATIC_DEPTH = 3

DIL_TILE = 1024
DIL_RES = 16
DIL_CHUNK = DIL_TILE // DIL_RES
CONV_HALO = 16
V7X_VMEM_BYTES = 64 * 1024 * 1024
VMEM_LIMIT = V7X_VMEM_BYTES * 7 // 8

BF16 = jnp.bfloat16
F32 = jnp.float32


def _params(semantics):
    return pltpu.CompilerParams(dimension_semantics=semantics, vmem_limit_bytes=VMEM_LIMIT)


def _rms_rows(x, g):
    ms = jnp.mean(x * x, axis=-1, keepdims=True)
    return x * lax.rsqrt(ms + RMS_EPS) * g


def _norm_into(x_ref, g_ref, dst_ref, dst_row0, chunk):
    g = g_ref[...]

    def body(c, carry):
        r = pl.multiple_of(c * chunk, chunk)
        y = _rms_rows(x_ref[pl.ds(r, chunk), :], g)
        dst_ref[pl.ds(dst_row0 + r, chunk), :] = y.astype(dst_ref.dtype)
        return carry

    lax.fori_loop(0, x_ref.shape[0] // chunk, body, 0)


def _norm_matmul_kernel(x_ref, g_ref, w_ref, o_ref, hn_ref):
    @pl.when(pl.program_id(1) == 0)
    def _():
        _norm_into(x_ref, g_ref, hn_ref, 0, 128)

    o_ref[...] = jnp.dot(hn_ref[...], w_ref[...], preferred_element_type=F32).astype(o_ref.dtype)


def _norm_matmul(x, g, w, tm, tn):
    M, K = x.shape
    N = w.shape[1]
    return pl.pallas_call(
        _norm_matmul_kernel,
        grid=(M // tm, N // tn),
        in_specs=[
            pl.BlockSpec((tm, K), lambda i, j: (i, 0)),
            pl.BlockSpec((1, K), lambda i, j: (0, 0)),
            pl.BlockSpec((K, tn), lambda i, j: (0, j)),
        ],
        out_specs=pl.BlockSpec((tm, tn), lambda i, j: (i, j)),
        out_shape=jax.ShapeDtypeStruct((M, N), BF16),
        scratch_shapes=[pltpu.VMEM((tm, K), BF16)],
        compiler_params=_params(("parallel", "arbitrary")),
        name="norm_matmul",
    )(x, g.reshape(1, K), w)


def _matmul_res_kernel(a_ref, w_ref, r_ref, o_ref):
    o_ref[...] = r_ref[...] + jnp.dot(a_ref[...], w_ref[...], preferred_element_type=F32)


def _matmul_res_norm_kernel(a_ref, w_ref, r_ref, g_ref, o_ref):
    y = r_ref[...] + jnp.dot(a_ref[...], w_ref[...], preferred_element_type=F32)
    o_ref[...] = _rms_rows(y, g_ref[...])


def _matmul_res(a, w, res, tm, norm_gain=None):
    M, K = a.shape
    N = w.shape[1]
    in_specs = [
        pl.BlockSpec((tm, K), lambda i: (i, 0)),
        pl.BlockSpec((K, N), lambda i: (0, 0), pipeline_mode=pl.Buffered(1)),
        pl.BlockSpec((tm, N), lambda i: (i, 0)),
    ]
    args = [a, w, res]
    if norm_gain is not None:
        in_specs.append(pl.BlockSpec((1, N), lambda i: (0, 0)))
        args.append(norm_gain.reshape(1, N))
    return pl.pallas_call(
        _matmul_res_kernel if norm_gain is None else _matmul_res_norm_kernel,
        grid=(M // tm,),
        in_specs=in_specs,
        out_specs=pl.BlockSpec((tm, N), lambda i: (i, 0)),
        out_shape=jax.ShapeDtypeStruct((M, N), F32),
        compiler_params=_params(("parallel",)),
        name="matmul_res" if norm_gain is None else "matmul_res_norm",
    )(*args)


def _sb_kernel(q_ref, k_ref, v_ref, o_ref, acc_ref, later_ref, *, chains):
    scale = HEAD_DIM ** -0.5

    def causal_mask(n_rows):
        row = lax.broadcasted_iota(jnp.int32, (n_rows, BLOCK), 0)
        col = lax.broadcasted_iota(jnp.int32, (n_rows, BLOCK), 1)
        return jnp.logical_or(row >= BLOCK, col < row)

    causal = {n: causal_mask(n * BLOCK) for n in range(1, SB_STATIC_DEPTH + 1)}
    suffix = (lax.broadcasted_iota(jnp.int32, (BLOCK, BLOCK), 0)
              > lax.broadcasted_iota(jnp.int32, (BLOCK, BLOCK), 1)).astype(BF16)
    base = pl.program_id(2) * chains
    contract_last = (((1,), (1,)), ((), ()))

    def key_block(rows, kb, mask):
        start = pl.multiple_of(kb * BLOCK, BLOCK)
        k = k_ref[pl.ds(start, BLOCK), :]
        v = v_ref[pl.ds(start, BLOCK), :]
        z = lax.dot_general(q_ref[rows, :], k, contract_last, preferred_element_type=F32) * scale
        log1m = -(jnp.maximum(z, 0.0) + jnp.log1p(jnp.exp(-jnp.abs(z))))
        log_beta = z + log1m
        if mask is not None:
            log1m = jnp.where(mask, log1m, 0.0)
        within = jnp.dot(log1m.astype(BF16), suffix, preferred_element_type=F32)
        a = jnp.exp(log_beta + within + later_ref[rows, :])
        if mask is not None:
            a = jnp.where(mask, a, 0.0)
        acc_ref[rows, :] += jnp.dot(a.astype(BF16), v, preferred_element_type=F32)
        later_ref[rows, :] += jnp.sum(log1m, axis=-1, keepdims=True)

    def retire_if_no_keys(rows, kb):
        later_ref[rows, :] = jnp.where(kb >= 0, later_ref[rows, :], NEG_INF)

    acc_ref[...] = jnp.zeros_like(acc_ref)
    later_ref[...] = jnp.zeros_like(later_ref)
    for first in range(chains - 1, -SB_STATIC_DEPTH, -1):
        lo, hi = max(first, 0), min(first + SB_STATIC_DEPTH - 1, chains - 1)
        rows = slice(lo * BLOCK, (hi + 1) * BLOCK)
        kb = base + first
        if first >= 0:
            key_block(rows, kb, causal[hi - lo + 1])
        else:
            retire_if_no_keys(rows, kb)
            key_block(rows, jnp.maximum(kb, 0), None)

    def cond(carry):
        _, worst = carry
        return worst > SB_EXIT_LOG

    def body(carry):
        delta, _ = carry
        for c in range(chains):
            rows = slice(c * BLOCK, (c + 1) * BLOCK)
            kb = base + c - delta
            retire_if_no_keys(rows, kb)
            key_block(rows, jnp.maximum(kb, 0), None)
        return delta + 1, jnp.max(later_ref[...])

    lax.while_loop(cond, body, (jnp.int32(SB_STATIC_DEPTH), jnp.max(later_ref[...])))
    o_ref[...] = acc_ref[...].astype(o_ref.dtype)


def _stick_breaking(qkv, B, S, tq):
    H = SB_HEADS
    nq = S // tq
    return pl.pallas_call(
        functools.partial(_sb_kernel, chains=tq // BLOCK),
        grid=(B, H, nq),
        in_specs=[
            pl.BlockSpec((tq, HEAD_DIM), lambda b, h, i: (b * nq + i, h)),
            pl.BlockSpec((S, HEAD_DIM), lambda b, h, i: (b, H + h)),
            pl.BlockSpec((S, HEAD_DIM), lambda b, h, i: (b, 2 * H + h)),
        ],
        out_specs=pl.BlockSpec((tq, HEAD_DIM), lambda b, h, i: (b * nq + i, h)),
        out_shape=jax.ShapeDtypeStruct((B * S, H * HEAD_DIM), BF16),
        scratch_shapes=[pltpu.VMEM((tq, HEAD_DIM), F32), pltpu.VMEM((tq, 1), F32)],
        compiler_params=_params(("parallel", "parallel", "arbitrary")),
        name="stick_breaking",
    )(qkv, qkv, qkv)


def _residue_major_permutation():
    i = np.arange(DIL_TILE)
    p = np.zeros((DIL_TILE, DIL_TILE), np.float32)
    p[i, (i % DIL_CHUNK) * DIL_RES + i // DIL_CHUNK] = 1.0
    return p


def _norm_matmul_perm_kernel(x_ref, g_ref, p_ref, w_ref, o_ref, hn_ref, hnp_ref, *, natural_tiles, chunk):
    j = pl.program_id(1)

    @pl.when(j == 0)
    def _():
        _norm_into(x_ref, g_ref, hn_ref, 0, 128)
        for c in range(hn_ref.shape[1] // chunk):
            cols = slice(c * chunk, (c + 1) * chunk)
            hnp_ref[:, cols] = jnp.dot(p_ref[...], hn_ref[:, cols],
                                       preferred_element_type=F32).astype(hnp_ref.dtype)

    @pl.when(j < natural_tiles)
    def _():
        o_ref[...] = jnp.dot(hn_ref[...], w_ref[...], preferred_element_type=F32).astype(o_ref.dtype)

    @pl.when(j >= natural_tiles)
    def _():
        o_ref[...] = jnp.dot(hnp_ref[...], w_ref[...], preferred_element_type=F32).astype(o_ref.dtype)


def _norm_matmul_perm(x, g, w, perm, natural_cols, tn):
    M, K = x.shape
    N = w.shape[1]
    tm = DIL_TILE
    return pl.pallas_call(
        functools.partial(_norm_matmul_perm_kernel, natural_tiles=natural_cols // tn, chunk=512),
        grid=(M // tm, N // tn),
        in_specs=[
            pl.BlockSpec((tm, K), lambda i, j: (i, 0)),
            pl.BlockSpec((1, K), lambda i, j: (0, 0)),
            pl.BlockSpec((tm, tm), lambda i, j: (0, 0)),
            pl.BlockSpec((K, tn), lambda i, j: (0, j)),
        ],
        out_specs=pl.BlockSpec((tm, tn), lambda i, j: (i, j)),
        out_shape=jax.ShapeDtypeStruct((M, N), BF16),
        scratch_shapes=[pltpu.VMEM((tm, K), BF16), pltpu.VMEM((tm, K), BF16)],
        compiler_params=_params(("parallel", "arbitrary")),
        name="norm_matmul_perm",
    )(x, g.reshape(1, K), perm, w)


def _rel_bucket_table(dil, order):
    order = np.asarray(order)
    qi = order[:, None]
    kc = np.concatenate([order, BLOCK + order])[None, :]
    rel = BLOCK + qi - kc
    n = jnp.asarray(np.maximum(rel, 0) * dil)
    max_exact = REL_BUCKETS // 2
    nf = jnp.maximum(n, 1).astype(F32)
    large = max_exact + (jnp.log(nf / max_exact) / math.log(REL_MAX_DISTANCE / max_exact)
                         * (REL_BUCKETS - max_exact)).astype(jnp.int32)
    large = jnp.minimum(large, REL_BUCKETS - 1)
    bucket = jnp.where(n < max_exact, n, large).astype(jnp.int32)
    return jnp.where(jnp.asarray((rel >= 0) & (rel <= BLOCK)), bucket, -1)


class _NaturalRows:
    @staticmethod
    def load(ref, j, cols):
        return ref[j * BLOCK:(j + 1) * BLOCK, cols]

    @staticmethod
    def load_prev(ref, cols):
        return ref[:, cols]

    @staticmethod
    def store(ref, j, cols, val):
        ref[j * BLOCK:(j + 1) * BLOCK, cols] = val


class _Residue16Rows:
    per = BLOCK // DIL_CHUNK

    @classmethod
    def load(cls, ref, j, cols):
        return jnp.concatenate([ref[cls.per * j + t, :, cols] for t in range(cls.per)], axis=0)

    @classmethod
    def load_prev(cls, ref, cols):
        return jnp.concatenate([ref[t, :, cols] for t in range(cls.per)], axis=0)

    @classmethod
    def store(cls, ref, j, cols, val):
        for t in range(cls.per):
            ref[cls.per * j + t, :, cols] = val[t * DIL_CHUNK:(t + 1) * DIL_CHUNK]


class _Residue4Rows:
    sub = BLOCK // 4
    per_tile = DIL_CHUNK // sub
    order = [4 * m + a for a in range(4) for m in range(BLOCK // 4)]

    @classmethod
    def _rows(cls, j):
        return j // cls.per_tile, slice((j % cls.per_tile) * cls.sub, (j % cls.per_tile + 1) * cls.sub)

    @classmethod
    def load(cls, ref, j, cols):
        t, ms = cls._rows(j)
        return jnp.concatenate([ref[t, a, ms, cols] for a in range(4)], axis=0)

    @classmethod
    def load_prev(cls, ref, cols):
        return cls.load(ref, cls.per_tile - 1, cols)

    @classmethod
    def store(cls, ref, j, cols, val):
        t, ms = cls._rows(j)
        for a in range(4):
            ref[t, a, ms, cols] = val[a * cls.sub:(a + 1) * cls.sub]


def _dil_kernel(bucket_ref, table_ref, q_ref, kc_ref, kp_ref, vc_ref, vp_ref, o_ref, lse_ref, bias_ref,
                *, sub_blocks, rows):
    scale = HEAD_DIM ** -0.5
    first_step = jnp.logical_and(jnp.logical_and(pl.program_id(0) == 0, pl.program_id(1) == 0),
                                 pl.program_id(2) == 0)

    @pl.when(first_step)
    def _():
        bucket = bucket_ref[...]
        for h in range(DIL_GROUP_HEADS):
            b = jnp.full(bucket.shape, NEG_INF, F32)
            for t in range(REL_BUCKETS):
                b = jnp.where(bucket == t, table_ref[t, h], b)
            bias_ref[h] = b

    lane = lax.broadcasted_iota(jnp.int32, (BLOCK, 128), 1)
    keep_keys = jnp.logical_or(lax.broadcasted_iota(jnp.int32, (BLOCK, 2 * BLOCK), 1) >= BLOCK,
                               pl.program_id(2) > 0)
    ones = jnp.ones((2 * BLOCK, HEAD_DIM), BF16)
    contract_last = (((1,), (1,)), ((), ()))
    all_lanes = slice(None)

    def prev_and_cur(cur_ref, prev_ref, j, cols):
        prev = rows.load_prev(prev_ref, cols) if j == 0 else rows.load(cur_ref, j - 1, cols)
        return jnp.concatenate([prev, rows.load(cur_ref, j, cols)], axis=0)

    for j in range(sub_blocks):
        stats = jnp.zeros((BLOCK, 128), F32)
        for h in range(DIL_GROUP_HEADS):
            cols = slice(h * HEAD_DIM, (h + 1) * HEAD_DIM)
            q = rows.load(q_ref, j, cols)
            k = prev_and_cur(kc_ref, kp_ref, j, cols)
            v = prev_and_cur(vc_ref, vp_ref, j, cols)
            s = lax.dot_general(q, k, contract_last, preferred_element_type=F32) * scale + bias_ref[h]
            if j == 0:
                s = jnp.where(keep_keys, s, NEG_INF)
            m = jnp.max(s, axis=-1, keepdims=True)
            p = jnp.exp(s - m).astype(BF16)
            ol = jnp.dot(p, jnp.concatenate([v, ones], axis=1), preferred_element_type=F32)
            o, l = ol[:, :HEAD_DIM], ol[:, HEAD_DIM:]
            rows.store(o_ref, j, cols, (o / l).astype(o_ref.dtype))
            stats = jnp.where(lane == h, m + jnp.log(l), stats)
        rows.store(lse_ref, j, all_lanes, stats)


def _dilated_group(qkv, rel_table, B, S, g, dil):
    W, C = DIL_WIDTH, qkv.shape[1]
    tiles = S // DIL_TILE
    col0 = g * 3
    if dil == 1:
        rows, n_res = _NaturalRows, 1
        blocks = min(4, S // BLOCK)
        steps = S // (blocks * BLOCK)
        views = [(B * S, C), (B * S, W), (B * S, 128)]
        cur_shape, prev_shape = (blocks * BLOCK,), (BLOCK,)
        cur_idx = lambda b, r, i: (b * steps + i,)
        prev_idx = lambda b, r, i: (b * (S // BLOCK) + jnp.maximum(i * blocks - 1, 0),)
    elif dil == DIL_RES:
        rows, n_res = _Residue16Rows, DIL_RES
        per = _Residue16Rows.per
        t_step = min(4 * per, tiles)
        blocks, steps = t_step // per, tiles // t_step
        views = [(B * tiles, DIL_RES, DIL_CHUNK, c) for c in (C, W, 128)]
        cur_shape, prev_shape = (t_step, None, DIL_CHUNK), (per, None, DIL_CHUNK)
        cur_idx = lambda b, r, i: (b * steps + i, r, 0)
        prev_idx = lambda b, r, i: (b * (tiles // per) + jnp.maximum(i * blocks - 1, 0), r, 0)
    else:
        assert dil == 4 and DIL_RES == 16
        rows, n_res = _Residue4Rows, 4
        per_tile = _Residue4Rows.per_tile
        t_step = min(2, tiles)
        blocks, steps = t_step * per_tile, tiles // t_step
        views = [(B * tiles, 4, 4, DIL_CHUNK, c) for c in (C, W, 128)]
        cur_shape, prev_shape = (t_step, 4, None, DIL_CHUNK), (1, 4, None, DIL_CHUNK)
        cur_idx = lambda b, r, i: (b * steps + i, 0, r, 0)
        prev_idx = lambda b, r, i: (b * tiles + jnp.maximum(i * t_step - 1, 0), 0, r, 0)

    def spec(shape, idx, width, col):
        return pl.BlockSpec(shape + (width,), lambda b, r, i: idx(b, r, i) + (col,))

    qkv_v = qkv.reshape(views[0])
    order = getattr(rows, "order", list(range(BLOCK)))
    o, lse = pl.pallas_call(
        functools.partial(_dil_kernel, sub_blocks=blocks, rows=rows),
        grid=(B, n_res, steps),
        in_specs=[
            pl.BlockSpec((BLOCK, 2 * BLOCK), lambda b, r, i: (0, 0)),
            pl.BlockSpec(memory_space=pltpu.SMEM),
            spec(cur_shape, cur_idx, W, col0),
            spec(cur_shape, cur_idx, W, col0 + 1), spec(prev_shape, prev_idx, W, col0 + 1),
            spec(cur_shape, cur_idx, W, col0 + 2), spec(prev_shape, prev_idx, W, col0 + 2),
        ],
        out_specs=[spec(cur_shape, cur_idx, W, 0), spec(cur_shape, cur_idx, 128, 0)],
        out_shape=[jax.ShapeDtypeStruct(views[1], BF16), jax.ShapeDtypeStruct(views[2], F32)],
        scratch_shapes=[pltpu.VMEM((DIL_GROUP_HEADS, BLOCK, 2 * BLOCK), F32)],
        compiler_params=_params(("arbitrary", "arbitrary", "arbitrary")),
        name=f"dilated_group{g}",
    )(_rel_bucket_table(dil, order), rel_table, qkv_v, qkv_v, qkv_v, qkv_v, qkv_v)
    return o.reshape(B * S, W), lse.reshape(B * S, 128)


def _unpermute_f32(pt, x):
    hi = x.astype(BF16)
    rest = x - hi.astype(F32)
    mid = rest.astype(BF16)
    lo = (rest - mid.astype(F32)).astype(BF16)
    return (jnp.dot(pt, hi, preferred_element_type=F32) + jnp.dot(pt, mid, preferred_element_type=F32)
            + jnp.dot(pt, lo, preferred_element_type=F32))


def _combine_matmul_res_kernel(o0_ref, o1_ref, o2_ref, l0_ref, l1_ref, l2_ref, pt_ref, w_ref, r_ref, out_ref,
                               comb_ref):
    @pl.when(pl.program_id(1) == 0)
    def _():
        pt = pt_ref[...]
        l0 = l0_ref[...]
        l1 = _unpermute_f32(pt, l1_ref[...])
        l2 = _unpermute_f32(pt, l2_ref[...])
        mx = jnp.maximum(jnp.maximum(l0, l1), l2)
        e0, e1, e2 = jnp.exp(l0 - mx), jnp.exp(l1 - mx), jnp.exp(l2 - mx)
        den = e0 + e1 + e2
        w0, w1, w2 = e0 / den, e1 / den, e2 / den
        pair = 2 * HEAD_DIM
        for hp in range(DIL_GROUP_HEADS // 2):
            pcols = slice(hp * pair, (hp + 1) * pair)
            o1 = jnp.dot(pt, o1_ref[:, pcols], preferred_element_type=F32)
            o2 = jnp.dot(pt, o2_ref[:, pcols], preferred_element_type=F32)
            for k in range(2):
                h = 2 * hp + k
                cols = slice(h * HEAD_DIM, (h + 1) * HEAD_DIM)
                sub = slice(k * HEAD_DIM, (k + 1) * HEAD_DIM)
                c = (w0[:, h:h + 1] * o0_ref[:, cols].astype(F32)
                     + w1[:, h:h + 1] * o1[:, sub] + w2[:, h:h + 1] * o2[:, sub])
                comb_ref[:, cols] = c.astype(comb_ref.dtype)

    out_ref[...] = r_ref[...] + jnp.dot(comb_ref[...], w_ref[...], preferred_element_type=F32)


def _combine_matmul_res(os, lses, unperm, w, res, tn):
    M, K = os[0].shape
    N = w.shape[1]
    tm = DIL_TILE
    o_spec = pl.BlockSpec((tm, K), lambda i, j: (i, 0))
    l_spec = pl.BlockSpec((tm, 128), lambda i, j: (i, 0))
    return pl.pallas_call(
        _combine_matmul_res_kernel,
        grid=(M // tm, N // tn),
        in_specs=[o_spec, o_spec, o_spec, l_spec, l_spec, l_spec,
                  pl.BlockSpec((tm, tm), lambda i, j: (0, 0)),
                  pl.BlockSpec((K, tn), lambda i, j: (0, j)),
                  pl.BlockSpec((tm, tn), lambda i, j: (i, j))],
        out_specs=pl.BlockSpec((tm, tn), lambda i, j: (i, j)),
        out_shape=jax.ShapeDtypeStruct((M, N), F32),
        scratch_shapes=[pltpu.VMEM((tm, K), BF16)],
        compiler_params=_params(("parallel", "arbitrary")),
        name="combine_matmul_res",
    )(*os, *lses, unperm, w, res)


def _ffn_up_kernel(x_ref, xp_ref, g_ref, wg_ref, wv_ref, cw_ref, cb_ref, o_ref, hn_ref, gate_ref,
                   *, tiles_per_seq):
    tm = x_ref.shape[0]

    @pl.when(pl.program_id(1) == 0)
    def _():
        halo = _rms_rows(xp_ref[...], g_ref[...])
        halo = jnp.where(pl.program_id(0) % tiles_per_seq == 0, 0.0, halo)
        hn_ref[0:CONV_HALO, :] = halo.astype(hn_ref.dtype)
        _norm_into(x_ref, g_ref, hn_ref, CONV_HALO, 128)

    gate_ref[...] = jnp.dot(hn_ref[...], wg_ref[...], preferred_element_type=F32)
    val = jnp.dot(hn_ref[CONV_HALO:, :], wv_ref[...], preferred_element_type=F32)
    conv = cb_ref[...]
    for j in range(CONV_WIDTH):
        off = CONV_HALO - (CONV_WIDTH - 1 - j)
        conv = conv + gate_ref[off:off + tm, :] * cw_ref[j:j + 1, :]
    act = conv / (1.0 + jnp.exp(-conv)) * val
    o_ref[...] = act.astype(o_ref.dtype)


def _ffn_up(x, g, w_up, conv_w, conv_b, S, tm, tf):
    M, K = x.shape
    d_ff = w_up.shape[1] // 2
    nf = d_ff // tf
    halo_blocks_per_tile = tm // CONV_HALO
    return pl.pallas_call(
        functools.partial(_ffn_up_kernel, tiles_per_seq=S // tm),
        grid=(M // tm, nf),
        in_specs=[
            pl.BlockSpec((tm, K), lambda i, j: (i, 0)),
            pl.BlockSpec((CONV_HALO, K), lambda i, j: (jnp.maximum(i * halo_blocks_per_tile - 1, 0), 0)),
            pl.BlockSpec((1, K), lambda i, j: (0, 0)),
            pl.BlockSpec((K, tf), lambda i, j: (0, j)),
            pl.BlockSpec((K, tf), lambda i, j: (0, nf + j)),
            pl.BlockSpec((CONV_WIDTH, tf), lambda i, j: (0, j)),
            pl.BlockSpec((1, tf), lambda i, j: (0, j)),
        ],
        out_specs=pl.BlockSpec((tm, tf), lambda i, j: (i, j)),
        out_shape=jax.ShapeDtypeStruct((M, d_ff), BF16),
        scratch_shapes=[pltpu.VMEM((tm + CONV_HALO, K), BF16), pltpu.VMEM((tm + CONV_HALO, tf), F32)],
        compiler_params=_params(("parallel", "arbitrary")),
        name="ffn_up",
    )(x, x, g.reshape(1, K), w_up, w_up, conv_w, conv_b.reshape(1, d_ff))


def kernel(x, ln_mix, ln_ffn, ln_f, w_qkv_sb, w_o_sb, w_qkv_dil, w_o_dil, rel_bias, w_up, conv_w, conv_b, w_down):
    B, S, D = x.shape
    depth = ln_mix.shape[0]
    h = x.reshape(B * S, D)
    for i in range(depth):
        j = i // N_MIXERS
        if i % N_MIXERS == 0:
            qkv = _norm_matmul(h, ln_mix[i], w_qkv_sb[j].astype(BF16), tm=1024, tn=1024)
            o = _stick_breaking(qkv, B, S, tq=min(2048, S))
            h = _matmul_res(o, w_o_sb[j].astype(BF16), h, tm=512)
        else:
            perm = _residue_major_permutation()
            qkv = _norm_matmul_perm(h, ln_mix[i], w_qkv_dil[j].astype(BF16), jnp.asarray(perm, BF16),
                                    natural_cols=3 * DIL_WIDTH, tn=1024)
            os, lses = [], []
            for g, (window, dil) in enumerate(DIL_PATTERNS):
                assert window // dil == BLOCK and S % (dil * BLOCK) == 0
                table = rel_bias[:, g * DIL_GROUP_HEADS:(g + 1) * DIL_GROUP_HEADS]
                o, lse = _dilated_group(qkv, table, B, S, g, dil)
                os.append(o)
                lses.append(lse)
            h = _combine_matmul_res(os, lses, jnp.asarray(perm.T, BF16), w_o_dil[j].astype(BF16), h, tn=512)
        act = _ffn_up(h, ln_ffn[i], w_up[i].astype(BF16), conv_w[i], conv_b[i], S, tm=1024, tf=512)
        h = _matmul_res(act, w_down[i].astype(BF16), h, tm=512, norm_gain=ln_f if i == depth - 1 else None)
    return h.reshape(B, S, D)
```

```python
import functools
import math

import jax
import jax.numpy as jnp
import numpy as np
from jax import lax
from jax.experimental import pallas as pl
from jax.experimental.pallas import tpu as pltpu

BLOCK = 128
SB_HEADS = 16
HEAD_DIM = 128
DIL_PATTERNS = ((128, 1), (512, 4), (2048, 16))
DIL_GROUPS = len(DIL_PATTERNS)
DIL_GROUP_HEADS = 8
DIL_WIDTH = DIL_GROUP_HEADS * HEAD_DIM
REL_BUCKETS = 32
REL_MAX_DISTANCE = 2048
CONV_WIDTH = 3
RMS_EPS = 1e-6
NEG_INF = -1e30
LOG2E = math.log2(math.e)
N_MIXERS = 2

SB_EXIT_LOG = -104.0
SB_STATIC_DEPTH = 3

DIL_TILE = 1024
DIL_RES = 16
DIL_CHUNK = DIL_TILE // DIL_RES
CONV_HALO = 16
V7X_VMEM_BYTES = 64 * 1024 * 1024
VMEM_LIMIT = V7X_VMEM_BYTES * 7 // 8

BF16 = jnp.bfloat16
F32 = jnp.float32


def _params(semantics):
    return pltpu.CompilerParams(dimension_semantics=semantics, vmem_limit_bytes=VMEM_LIMIT)


def _rms_rows(x, g):
    ms = jnp.mean(x * x, axis=-1, keepdims=True)
    return x * lax.rsqrt(ms + RMS_EPS) * g


def _norm_into(x_ref, g_ref, dst_ref, dst_row0, chunk):
    g = g_ref[...]

    def body(c, carry):
        r = pl.multiple_of(c * chunk, chunk)
        y = _rms_rows(x_ref[pl.ds(r, chunk), :], g)
        dst_ref[pl.ds(dst_row0 + r, chunk), :] = y.astype(dst_ref.dtype)
        return carry

    lax.fori_loop(0, x_ref.shape[0] // chunk, body, 0)


def _norm_matmul_kernel(x_ref, g_ref, w_ref, o_ref, hn_ref):
    @pl.when(pl.program_id(1) == 0)
    def _():
        _norm_into(x_ref, g_ref, hn_ref, 0, 128)

    o_ref[...] = jnp.dot(hn_ref[...], w_ref[...], preferred_element_type=F32).astype(o_ref.dtype)


def _norm_matmul(x, g, w, tm, tn):
    M, K = x.shape
    N = w.shape[1]
    return pl.pallas_call(
        _norm_matmul_kernel,
        grid=(M // tm, N // tn),
        in_specs=[
            pl.BlockSpec((tm, K), lambda i, j: (i, 0)),
            pl.BlockSpec((1, K), lambda i, j: (0, 0)),
            pl.BlockSpec((K, tn), lambda i, j: (0, j)),
        ],
        out_specs=pl.BlockSpec((tm, tn), lambda i, j: (i, j)),
        out_shape=jax.ShapeDtypeStruct((M, N), BF16),
        scratch_shapes=[pltpu.VMEM((tm, K), BF16)],
        compiler_params=_params(("parallel", "arbitrary")),
        name="norm_matmul",
    )(x, g.reshape(1, K), w)


def _matmul_res_kernel(a_ref, w_ref, r_ref, o_ref):
    o_ref[...] = r_ref[...] + jnp.dot(a_ref[...], w_ref[...], preferred_element_type=F32)


def _matmul_res_norm_kernel(a_ref, w_ref, r_ref, g_ref, o_ref):
    y = r_ref[...] + jnp.dot(a_ref[...], w_ref[...], preferred_element_type=F32)
    o_ref[...] = _rms_rows(y, g_ref[...])


def _matmul_res(a, w, res, tm, norm_gain=None):
    M, K = a.shape
    N = w.shape[1]
    in_specs = [
        pl.BlockSpec((tm, K), lambda i: (i, 0)),
        pl.BlockSpec((K, N), lambda i: (0, 0), pipeline_mode=pl.Buffered(1)),
        pl.BlockSpec((tm, N), lambda i: (i, 0)),
    ]
    args = [a, w, res]
    if norm_gain is not None:
        in_specs.append(pl.BlockSpec((1, N), lambda i: (0, 0)))
        args.append(norm_gain.reshape(1, N))
    return pl.pallas_call(
        _matmul_res_kernel if norm_gain is None else _matmul_res_norm_kernel,
        grid=(M // tm,),
        in_specs=in_specs,
        out_specs=pl.BlockSpec((tm, N), lambda i: (i, 0)),
        out_shape=jax.ShapeDtypeStruct((M, N), F32),
        compiler_params=_params(("parallel",)),
        name="matmul_res" if norm_gain is None else "matmul_res_norm",
    )(*args)


def _sb_kernel(q_ref, k_ref, v_ref, o_ref, acc_ref, spent_ref, *, chains):
    scale = HEAD_DIM ** -0.5

    def causal_mask(n_rows):
        row = lax.broadcasted_iota(jnp.int32, (n_rows, BLOCK), 0)
        col = lax.broadcasted_iota(jnp.int32, (n_rows, BLOCK), 1)
        return jnp.logical_or(row >= BLOCK, col < row)

    causal = {n: causal_mask(n * BLOCK) for n in range(1, SB_STATIC_DEPTH + 1)}
    suffix = (lax.broadcasted_iota(jnp.int32, (BLOCK, BLOCK), 0)
              > lax.broadcasted_iota(jnp.int32, (BLOCK, BLOCK), 1)).astype(BF16)
    base = pl.program_id(2) * chains
    contract_last = (((1,), (1,)), ((), ()))

    def key_block(rows, kb, mask):
        start = pl.multiple_of(kb * BLOCK, BLOCK)
        k = k_ref[pl.ds(start, BLOCK), :]
        v = v_ref[pl.ds(start, BLOCK), :]
        z = lax.dot_general(q_ref[rows, :], k, contract_last, preferred_element_type=F32) * scale
        drop = jnp.maximum(z, 0.0) + jnp.log(1.0 + jnp.exp2(jnp.abs(z) * -LOG2E))
        log_beta = z - drop
        if mask is not None:
            drop = jnp.where(mask, drop, 0.0)
        within = jnp.dot(drop.astype(BF16), suffix, preferred_element_type=F32)
        a = jnp.exp(log_beta - (within + spent_ref[rows, :]))
        if mask is not None:
            a = jnp.where(mask, a, 0.0)
        acc_ref[rows, :] += jnp.dot(a.astype(BF16), v, preferred_element_type=F32)
        spent_ref[rows, :] += jnp.sum(drop, axis=-1, keepdims=True)

    def retire_if_no_keys(rows, kb):
        spent_ref[rows, :] = jnp.where(kb >= 0, spent_ref[rows, :], -NEG_INF)

    acc_ref[...] = jnp.zeros_like(acc_ref)
    spent_ref[...] = jnp.zeros_like(spent_ref)
    for first in range(chains - 1, -SB_STATIC_DEPTH, -1):
        lo, hi = max(first, 0), min(first + SB_STATIC_DEPTH - 1, chains - 1)
        rows = slice(lo * BLOCK, (hi + 1) * BLOCK)
        kb = base + first
        if first >= 0:
            key_block(rows, kb, causal[hi - lo + 1])
        else:
            retire_if_no_keys(rows, kb)
            key_block(rows, jnp.maximum(kb, 0), None)

    def cond(carry):
        _, least_spent = carry
        return least_spent < -SB_EXIT_LOG

    def body(carry):
        delta, _ = carry
        for c in range(chains):
            rows = slice(c * BLOCK, (c + 1) * BLOCK)
            kb = base + c - delta
            retire_if_no_keys(rows, kb)
            key_block(rows, jnp.maximum(kb, 0), None)
        return delta + 1, jnp.min(spent_ref[...])

    lax.while_loop(cond, body, (jnp.int32(SB_STATIC_DEPTH), jnp.min(spent_ref[...])))
    o_ref[...] = acc_ref[...].astype(o_ref.dtype)


def _stick_breaking(qkv, B, S, tq):
    H = SB_HEADS
    nq = S // tq
    return pl.pallas_call(
        functools.partial(_sb_kernel, chains=tq // BLOCK),
        grid=(B, H, nq),
        in_specs=[
            pl.BlockSpec((tq, HEAD_DIM), lambda b, h, i: (b * nq + i, h)),
            pl.BlockSpec((S, HEAD_DIM), lambda b, h, i: (b, H + h)),
            pl.BlockSpec((S, HEAD_DIM), lambda b, h, i: (b, 2 * H + h)),
        ],
        out_specs=pl.BlockSpec((tq, HEAD_DIM), lambda b, h, i: (b * nq + i, h)),
        out_shape=jax.ShapeDtypeStruct((B * S, H * HEAD_DIM), BF16),
        scratch_shapes=[pltpu.VMEM((tq, HEAD_DIM), F32), pltpu.VMEM((tq, 1), F32)],
        compiler_params=_params(("parallel", "parallel", "arbitrary")),
        name="stick_breaking",
    )(qkv, qkv, qkv)


def _residue_major_permutation():
    i = np.arange(DIL_TILE)
    p = np.zeros((DIL_TILE, DIL_TILE), np.float32)
    p[i, (i % DIL_CHUNK) * DIL_RES + i // DIL_CHUNK] = 1.0
    return p


def _norm_matmul_perm_kernel(x_ref, g_ref, p_ref, w_ref, o_ref, hn_ref, hnp_ref, *, natural_tiles, chunk):
    j = pl.program_id(1)

    @pl.when(j == 0)
    def _():
        _norm_into(x_ref, g_ref, hn_ref, 0, 128)
        for c in range(hn_ref.shape[1] // chunk):
            cols = slice(c * chunk, (c + 1) * chunk)
            hnp_ref[:, cols] = jnp.dot(p_ref[...], hn_ref[:, cols],
                                       preferred_element_type=F32).astype(hnp_ref.dtype)

    @pl.when(j < natural_tiles)
    def _():
        o_ref[...] = jnp.dot(hn_ref[...], w_ref[...], preferred_element_type=F32).astype(o_ref.dtype)

    @pl.when(j >= natural_tiles)
    def _():
        o_ref[...] = jnp.dot(hnp_ref[...], w_ref[...], preferred_element_type=F32).astype(o_ref.dtype)


def _norm_matmul_perm(x, g, w, perm, natural_cols, tn):
    M, K = x.shape
    N = w.shape[1]
    tm = DIL_TILE
    return pl.pallas_call(
        functools.partial(_norm_matmul_perm_kernel, natural_tiles=natural_cols // tn, chunk=512),
        grid=(M // tm, N // tn),
        in_specs=[
            pl.BlockSpec((tm, K), lambda i, j: (i, 0)),
            pl.BlockSpec((1, K), lambda i, j: (0, 0)),
            pl.BlockSpec((tm, tm), lambda i, j: (0, 0)),
            pl.BlockSpec((K, tn), lambda i, j: (0, j)),
        ],
        out_specs=pl.BlockSpec((tm, tn), lambda i, j: (i, j)),
        out_shape=jax.ShapeDtypeStruct((M, N), BF16),
        scratch_shapes=[pltpu.VMEM((tm, K), BF16), pltpu.VMEM((tm, K), BF16)],
        compiler_params=_params(("parallel", "arbitrary")),
        name="norm_matmul_perm",
    )(x, g.reshape(1, K), perm, w)


def _rel_bucket_table(dil, order):
    order = np.asarray(order)
    qi = order[:, None]
    kc = np.concatenate([order, BLOCK + order])[None, :]
    rel = BLOCK + qi - kc
    n = jnp.asarray(np.maximum(rel, 0) * dil)
    max_exact = REL_BUCKETS // 2
    nf = jnp.maximum(n, 1).astype(F32)
    large = max_exact + (jnp.log(nf / max_exact) / math.log(REL_MAX_DISTANCE / max_exact)
                         * (REL_BUCKETS - max_exact)).astype(jnp.int32)
    large = jnp.minimum(large, REL_BUCKETS - 1)
    bucket = jnp.where(n < max_exact, n, large).astype(jnp.int32)
    return jnp.where(jnp.asarray((rel >= 0) & (rel <= BLOCK)), bucket, -1)


class _NaturalRows:
    @staticmethod
    def load(ref, j, cols):
        return ref[j * BLOCK:(j + 1) * BLOCK, cols]

    @staticmethod
    def load_prev(ref, cols):
        return ref[:, cols]

    @staticmethod
    def store(ref, j, cols, val):
        ref[j * BLOCK:(j + 1) * BLOCK, cols] = val


class _Residue16Rows:
    per = BLOCK // DIL_CHUNK

    @classmethod
    def load(cls, ref, j, cols):
        return jnp.concatenate([ref[cls.per * j + t, :, cols] for t in range(cls.per)], axis=0)

    @classmethod
    def load_prev(cls, ref, cols):
        return jnp.concatenate([ref[t, :, cols] for t in range(cls.per)], axis=0)

    @classmethod
    def store(cls, ref, j, cols, val):
        for t in range(cls.per):
            ref[cls.per * j + t, :, cols] = val[t * DIL_CHUNK:(t + 1) * DIL_CHUNK]


class _Residue4Rows:
    sub = BLOCK // 4
    per_tile = DIL_CHUNK // sub
    order = [4 * m + a for a in range(4) for m in range(BLOCK // 4)]

    @classmethod
    def _rows(cls, j):
        return j // cls.per_tile, slice((j % cls.per_tile) * cls.sub, (j % cls.per_tile + 1) * cls.sub)

    @classmethod
    def load(cls, ref, j, cols):
        t, ms = cls._rows(j)
        return jnp.concatenate([ref[t, a, ms, cols] for a in range(4)], axis=0)

    @classmethod
    def load_prev(cls, ref, cols):
        return cls.load(ref, cls.per_tile - 1, cols)

    @classmethod
    def store(cls, ref, j, cols, val):
        t, ms = cls._rows(j)
        for a in range(4):
            ref[t, a, ms, cols] = val[a * cls.sub:(a + 1) * cls.sub]


def _dil_kernel(bucket_ref, table_ref, q_ref, kc_ref, kp_ref, vc_ref, vp_ref, o_ref, lse_ref, bias_ref,
                *, sub_blocks, rows):
    scale = HEAD_DIM ** -0.5
    first_step = jnp.logical_and(jnp.logical_and(pl.program_id(0) == 0, pl.program_id(1) == 0),
                                 pl.program_id(2) == 0)

    @pl.when(first_step)
    def _():
        bucket = bucket_ref[...]
        for h in range(DIL_GROUP_HEADS):
            b = jnp.full(bucket.shape, NEG_INF, F32)
            for t in range(REL_BUCKETS):
                b = jnp.where(bucket == t, table_ref[t, h], b)
            bias_ref[h] = b

    lane = lax.broadcasted_iota(jnp.int32, (BLOCK, 128), 1)
    keep_keys = jnp.logical_or(lax.broadcasted_iota(jnp.int32, (BLOCK, 2 * BLOCK), 1) >= BLOCK,
                               pl.program_id(2) > 0)
    ones = jnp.ones((2 * BLOCK, HEAD_DIM), BF16)
    contract_last = (((1,), (1,)), ((), ()))
    all_lanes = slice(None)

    def prev_and_cur(cur_ref, prev_ref, j, cols):
        prev = rows.load_prev(prev_ref, cols) if j == 0 else rows.load(cur_ref, j - 1, cols)
        return jnp.concatenate([prev, rows.load(cur_ref, j, cols)], axis=0)

    for j in range(sub_blocks):
        stats = jnp.zeros((BLOCK, 128), F32)
        for h in range(DIL_GROUP_HEADS):
            cols = slice(h * HEAD_DIM, (h + 1) * HEAD_DIM)
            q = rows.load(q_ref, j, cols)
            k = prev_and_cur(kc_ref, kp_ref, j, cols)
            v = prev_and_cur(vc_ref, vp_ref, j, cols)
            s = lax.dot_general(q, k, contract_last, preferred_element_type=F32) * scale + bias_ref[h]
            if j == 0:
                s = jnp.where(keep_keys, s, NEG_INF)
            m = jnp.max(s, axis=-1, keepdims=True)
            p = jnp.exp(s - m).astype(BF16)
            ol = jnp.dot(p, jnp.concatenate([v, ones], axis=1), preferred_element_type=F32)
            o, l = ol[:, :HEAD_DIM], ol[:, HEAD_DIM:]
            rows.store(o_ref, j, cols, (o / l).astype(o_ref.dtype))
            stats = jnp.where(lane == h, m + jnp.log(l), stats)
        rows.store(lse_ref, j, all_lanes, stats)


def _dilated_group(qkv, rel_table, B, S, g, dil):
    W, C = DIL_WIDTH, qkv.shape[1]
    tiles = S // DIL_TILE
    col0 = g * 3
    if dil == 1:
        rows, n_res = _NaturalRows, 1
        blocks = min(4, S // BLOCK)
        steps = S // (blocks * BLOCK)
        views = [(B * S, C), (B * S, W), (B * S, 128)]
        cur_shape, prev_shape = (blocks * BLOCK,), (BLOCK,)
        cur_idx = lambda b, r, i: (b * steps + i,)
        prev_idx = lambda b, r, i: (b * (S // BLOCK) + jnp.maximum(i * blocks - 1, 0),)
    elif dil == DIL_RES:
        rows, n_res = _Residue16Rows, DIL_RES
        per = _Residue16Rows.per
        t_step = min(4 * per, tiles)
        blocks, steps = t_step // per, tiles // t_step
        views = [(B * tiles, DIL_RES, DIL_CHUNK, c) for c in (C, W, 128)]
        cur_shape, prev_shape = (t_step, None, DIL_CHUNK), (per, None, DIL_CHUNK)
        cur_idx = lambda b, r, i: (b * steps + i, r, 0)
        prev_idx = lambda b, r, i: (b * (tiles // per) + jnp.maximum(i * blocks - 1, 0), r, 0)
    else:
        assert dil == 4 and DIL_RES == 16
        rows, n_res = _Residue4Rows, 4
        per_tile = _Residue4Rows.per_tile
        t_step = min(2, tiles)
        blocks, steps = t_step * per_tile, tiles // t_step
        views = [(B * tiles, 4, 4, DIL_CHUNK, c) for c in (C, W, 128)]
        cur_shape, prev_shape = (t_step, 4, None, DIL_CHUNK), (1, 4, None, DIL_CHUNK)
        cur_idx = lambda b, r, i: (b * steps + i, 0, r, 0)
        prev_idx = lambda b, r, i: (b * tiles + jnp.maximum(i * t_step - 1, 0), 0, r, 0)

    def spec(shape, idx, width, col):
        return pl.BlockSpec(shape + (width,), lambda b, r, i: idx(b, r, i) + (col,))

    qkv_v = qkv.reshape(views[0])
    order = getattr(rows, "order", list(range(BLOCK)))
    o, lse = pl.pallas_call(
        functools.partial(_dil_kernel, sub_blocks=blocks, rows=rows),
        grid=(B, n_res, steps),
        in_specs=[
            pl.BlockSpec((BLOCK, 2 * BLOCK), lambda b, r, i: (0, 0)),
            pl.BlockSpec(memory_space=pltpu.SMEM),
            spec(cur_shape, cur_idx, W, col0),
            spec(cur_shape, cur_idx, W, col0 + 1), spec(prev_shape, prev_idx, W, col0 + 1),
            spec(cur_shape, cur_idx, W, col0 + 2), spec(prev_shape, prev_idx, W, col0 + 2),
        ],
        out_specs=[spec(cur_shape, cur_idx, W, 0), spec(cur_shape, cur_idx, 128, 0)],
        out_shape=[jax.ShapeDtypeStruct(views[1], BF16), jax.ShapeDtypeStruct(views[2], F32)],
        scratch_shapes=[pltpu.VMEM((DIL_GROUP_HEADS, BLOCK, 2 * BLOCK), F32)],
        compiler_params=_params(("arbitrary", "arbitrary", "arbitrary")),
        name=f"dilated_group{g}",
    )(_rel_bucket_table(dil, order), rel_table, qkv_v, qkv_v, qkv_v, qkv_v, qkv_v)
    return o.reshape(B * S, W), lse.reshape(B * S, 128)


def _bf16_pieces(x):
    hi = x.astype(BF16)
    rest = x - hi.astype(F32)
    mid = rest.astype(BF16)
    return [hi, mid, (rest - mid.astype(F32)).astype(BF16)]


def _combine_matmul_res_kernel(o0_ref, o1_ref, o2_ref, l0_ref, l1_ref, l2_ref, pt_ref, w_ref, r_ref, out_ref,
                               comb_ref):
    rows = out_ref.shape[0]
    pt = pt_ref[pl.ds(pl.multiple_of(pl.program_id(1) * rows, rows), rows), :]
    pieces = jnp.concatenate(_bf16_pieces(l1_ref[...]) + _bf16_pieces(l2_ref[...]), axis=1)
    lse = jnp.dot(pt, pieces, preferred_element_type=F32)
    l0 = l0_ref[...]
    l1 = lse[:, 0:128] + lse[:, 128:256] + lse[:, 256:384]
    l2 = lse[:, 384:512] + lse[:, 512:640] + lse[:, 640:768]
    mx = jnp.maximum(jnp.maximum(l0, l1), l2)
    e0, e1, e2 = jnp.exp(l0 - mx), jnp.exp(l1 - mx), jnp.exp(l2 - mx)
    den = e0 + e1 + e2
    w0, w1, w2 = e0 / den, e1 / den, e2 / den
    pair = 2 * HEAD_DIM
    for hp in range(DIL_GROUP_HEADS // 2):
        pcols = slice(hp * pair, (hp + 1) * pair)
        o1 = jnp.dot(pt, o1_ref[:, pcols], preferred_element_type=F32)
        o2 = jnp.dot(pt, o2_ref[:, pcols], preferred_element_type=F32)
        for k in range(2):
            h = 2 * hp + k
            cols = slice(h * HEAD_DIM, (h + 1) * HEAD_DIM)
            sub = slice(k * HEAD_DIM, (k + 1) * HEAD_DIM)
            c = (w0[:, h:h + 1] * o0_ref[:, cols].astype(F32)
                 + w1[:, h:h + 1] * o1[:, sub] + w2[:, h:h + 1] * o2[:, sub])
            comb_ref[:, cols] = c.astype(comb_ref.dtype)
    out_ref[...] = r_ref[...] + jnp.dot(comb_ref[...], w_ref[...], preferred_element_type=F32)


def _combine_matmul_res(os, lses, unperm, w, res, rows):
    M, K = os[0].shape
    N = w.shape[1]
    parts = DIL_TILE // rows
    own_o = pl.BlockSpec((rows, K), lambda i, p: (i * parts + p, 0))
    own_l = pl.BlockSpec((rows, 128), lambda i, p: (i * parts + p, 0))
    tile_o = pl.BlockSpec((DIL_TILE, K), lambda i, p: (i, 0))
    tile_l = pl.BlockSpec((DIL_TILE, 128), lambda i, p: (i, 0))
    row_block = pl.BlockSpec((rows, N), lambda i, p: (i * parts + p, 0))
    return pl.pallas_call(
        _combine_matmul_res_kernel,
        grid=(M // DIL_TILE, parts),
        in_specs=[own_o, tile_o, tile_o, own_l, tile_l, tile_l,
                  pl.BlockSpec((DIL_TILE, DIL_TILE), lambda i, p: (0, 0), pipeline_mode=pl.Buffered(1)),
                  pl.BlockSpec((K, N), lambda i, p: (0, 0), pipeline_mode=pl.Buffered(1)),
                  row_block],
        out_specs=row_block,
        out_shape=jax.ShapeDtypeStruct((M, N), F32),
        scratch_shapes=[pltpu.VMEM((rows, K), BF16)],
        compiler_params=_params(("parallel", "parallel")),
        name="combine_matmul_res",
    )(*os, *lses, unperm, w, res)


def _ffn_up_kernel(x_ref, xp_ref, g_ref, wg_ref, wv_ref, cw_ref, cb_ref, o_ref, hn_ref,
                   *, tiles_per_seq, col_chunks):
    @pl.when(pl.program_id(1) == 0)
    def _():
        halo = _rms_rows(xp_ref[...], g_ref[...])
        halo = jnp.where(pl.program_id(0) % tiles_per_seq == 0, 0.0, halo)
        hn_ref[0:CONV_HALO, :] = halo.astype(hn_ref.dtype)
        _norm_into(x_ref, g_ref, hn_ref, CONV_HALO, 128)

    width = o_ref.shape[1] // col_chunks
    for c in range(col_chunks):
        cols = slice(c * width, (c + 1) * width)
        gate = jnp.dot(hn_ref[...], wg_ref[:, cols], preferred_element_type=F32)
        val = jnp.dot(hn_ref[CONV_HALO:, :], wv_ref[:, cols], preferred_element_type=F32)
        conv = cb_ref[:, cols] + gate[CONV_HALO:] * cw_ref[CONV_WIDTH - 1:CONV_WIDTH, cols]
        for back in range(1, CONV_WIDTH):
            tap = CONV_WIDTH - 1 - back
            conv = conv + pltpu.roll(gate, back, axis=0)[CONV_HALO:] * cw_ref[tap:tap + 1, cols]
        act = conv / (1.0 + jnp.exp(-conv)) * val
        o_ref[:, cols] = act.astype(o_ref.dtype)


def _ffn_up(x, g, w_up, conv_w, conv_b, S, tm, tf):
    M, K = x.shape
    d_ff = w_up.shape[1] // 2
    nf = d_ff // tf
    halo_blocks_per_tile = tm // CONV_HALO
    return pl.pallas_call(
        functools.partial(_ffn_up_kernel, tiles_per_seq=S // tm, col_chunks=1),
        grid=(M // tm, nf),
        in_specs=[
            pl.BlockSpec((tm, K), lambda i, j: (i, 0)),
            pl.BlockSpec((CONV_HALO, K), lambda i, j: (jnp.maximum(i * halo_blocks_per_tile - 1, 0), 0)),
            pl.BlockSpec((1, K), lambda i, j: (0, 0)),
            pl.BlockSpec((K, tf), lambda i, j: (0, j)),
            pl.BlockSpec((K, tf), lambda i, j: (0, nf + j)),
            pl.BlockSpec((CONV_WIDTH, tf), lambda i, j: (0, j)),
            pl.BlockSpec((1, tf), lambda i, j: (0, j)),
        ],
        out_specs=pl.BlockSpec((tm, tf), lambda i, j: (i, j)),
        out_shape=jax.ShapeDtypeStruct((M, d_ff), BF16),
        scratch_shapes=[pltpu.VMEM((tm + CONV_HALO, K), BF16)],
        compiler_params=_params(("parallel", "arbitrary")),
        name="ffn_up",
    )(x, x, g.reshape(1, K), w_up, w_up, conv_w, conv_b.reshape(1, d_ff))


def kernel(x, ln_mix, ln_ffn, ln_f, w_qkv_sb, w_o_sb, w_qkv_dil, w_o_dil, rel_bias, w_up, conv_w, conv_b, w_down):
    B, S, D = x.shape
    depth = ln_mix.shape[0]
    h = x.reshape(B * S, D)
    for i in range(depth):
        j = i // N_MIXERS
        if i % N_MIXERS == 0:
            qkv = _norm_matmul(h, ln_mix[i], w_qkv_sb[j].astype(BF16), tm=1024, tn=1024)
            o = _stick_breaking(qkv, B, S, tq=min(2048, S))
            h = _matmul_res(o, w_o_sb[j].astype(BF16), h, tm=512)
        else:
            perm = _residue_major_permutation()
            qkv = _norm_matmul_perm(h, ln_mix[i], w_qkv_dil[j].astype(BF16), jnp.asarray(perm, BF16),
                                    natural_cols=3 * DIL_WIDTH, tn=1024)
            os, lses = [], []
            for g, (window, dil) in enumerate(DIL_PATTERNS):
                assert window // dil == BLOCK and S % (dil * BLOCK) == 0
                table = rel_bias[:, g * DIL_GROUP_HEADS:(g + 1) * DIL_GROUP_HEADS]
                o, lse = _dilated_group(qkv, table, B, S, g, dil)
                os.append(o)
                lses.append(lse)
            h = _combine_matmul_res(os, lses, jnp.asarray(perm.T, BF16), w_o_dil[j].astype(BF16), h, rows=512)
        act = _ffn_up(h, ln_ffn[i], w_up[i].astype(BF16), conv_w[i], conv_b[i], S, tm=1024, tf=512)
        h = _matmul_res(act, w_down[i].astype(BF16), h, tm=512, norm_gain=ln_f if i == depth - 1 else None)
    return h.reshape(B, S, D)
```

```python
import functools
import math

import jax
import jax.numpy as jnp
import numpy as np
from jax import lax
from jax.experimental import pallas as pl
from jax.experimental.pallas import tpu as pltpu

BLOCK = 128
SB_HEADS = 16
HEAD_DIM = 128
DIL_PATTERNS = ((128, 1), (512, 4), (2048, 16))
DIL_GROUPS = len(DIL_PATTERNS)
DIL_GROUP_HEADS = 8
DIL_WIDTH = DIL_GROUP_HEADS * HEAD_DIM
REL_BUCKETS = 32
REL_MAX_DISTANCE = 2048
CONV_WIDTH = 3
RMS_EPS = 1e-6
NEG_INF = -1e30
LOG2E = math.log2(math.e)
N_MIXERS = 2

SB_EXIT_LOG = -104.0
SB_STATIC_DEPTH = 3

DIL_TILE = 1024
DIL_RES = 16
DIL_CHUNK = DIL_TILE // DIL_RES
CONV_HALO = 16
V7X_VMEM_BYTES = 64 * 1024 * 1024
VMEM_LIMIT = V7X_VMEM_BYTES * 7 // 8

BF16 = jnp.bfloat16
F32 = jnp.float32


def _params(semantics):
    return pltpu.CompilerParams(dimension_semantics=semantics, vmem_limit_bytes=VMEM_LIMIT)


def _rms_rows(x, g):
    ms = jnp.mean(x * x, axis=-1, keepdims=True)
    return x * lax.rsqrt(ms + RMS_EPS) * g


def _norm_into(x_ref, g_ref, dst_ref, dst_row0, chunk):
    g = g_ref[...]

    def body(c, carry):
        r = pl.multiple_of(c * chunk, chunk)
        y = _rms_rows(x_ref[pl.ds(r, chunk), :], g)
        dst_ref[pl.ds(dst_row0 + r, chunk), :] = y.astype(dst_ref.dtype)
        return carry

    lax.fori_loop(0, x_ref.shape[0] // chunk, body, 0)


def _norm_matmul_kernel(x_ref, g_ref, w_ref, o_ref, hn_ref):
    @pl.when(pl.program_id(1) == 0)
    def _():
        _norm_into(x_ref, g_ref, hn_ref, 0, 128)

    y = jnp.dot(hn_ref[...], w_ref[...], preferred_element_type=F32).astype(o_ref.dtype)
    for c in range(o_ref.shape[0]):
        o_ref[c] = y[:, c * HEAD_DIM:(c + 1) * HEAD_DIM]


def _norm_matmul_heads(x, g, w, tm, tn):
    M, K = x.shape
    N = w.shape[1]
    slabs = tn // HEAD_DIM
    return pl.pallas_call(
        _norm_matmul_kernel,
        grid=(M // tm, N // tn),
        in_specs=[
            pl.BlockSpec((tm, K), lambda i, j: (i, 0)),
            pl.BlockSpec((1, K), lambda i, j: (0, 0)),
            pl.BlockSpec((K, tn), lambda i, j: (0, j)),
        ],
        out_specs=pl.BlockSpec((slabs, tm, HEAD_DIM), lambda i, j: (j, i, 0)),
        out_shape=jax.ShapeDtypeStruct((N // HEAD_DIM, M, HEAD_DIM), BF16),
        scratch_shapes=[pltpu.VMEM((tm, K), BF16)],
        compiler_params=_params(("parallel", "arbitrary")),
        name="norm_matmul",
    )(x, g.reshape(1, K), w)


def _matmul_res_kernel(a_ref, w_ref, r_ref, o_ref):
    o_ref[...] = r_ref[...] + jnp.dot(a_ref[...], w_ref[...], preferred_element_type=F32)


def _matmul_res_norm_kernel(a_ref, w_ref, r_ref, g_ref, o_ref):
    y = r_ref[...] + jnp.dot(a_ref[...], w_ref[...], preferred_element_type=F32)
    o_ref[...] = _rms_rows(y, g_ref[...])


def _matmul_res(a, w, res, tm, norm_gain=None):
    M, K = a.shape
    N = w.shape[1]
    in_specs = [
        pl.BlockSpec((tm, K), lambda i: (i, 0)),
        pl.BlockSpec((K, N), lambda i: (0, 0), pipeline_mode=pl.Buffered(1)),
        pl.BlockSpec((tm, N), lambda i: (i, 0)),
    ]
    args = [a, w, res]
    if norm_gain is not None:
        in_specs.append(pl.BlockSpec((1, N), lambda i: (0, 0)))
        args.append(norm_gain.reshape(1, N))
    return pl.pallas_call(
        _matmul_res_kernel if norm_gain is None else _matmul_res_norm_kernel,
        grid=(M // tm,),
        in_specs=in_specs,
        out_specs=pl.BlockSpec((tm, N), lambda i: (i, 0)),
        out_shape=jax.ShapeDtypeStruct((M, N), F32),
        compiler_params=_params(("parallel",)),
        name="matmul_res" if norm_gain is None else "matmul_res_norm",
    )(*args)


def _sb_kernel(q_ref, k_ref, v_ref, o_ref, acc_ref, spent_ref, *, chains):
    scale = HEAD_DIM ** -0.5

    row = lax.broadcasted_iota(jnp.int32, (BLOCK, BLOCK), 0)
    col = lax.broadcasted_iota(jnp.int32, (BLOCK, BLOCK), 1)
    causal = col < row
    suffix = (row > col).astype(BF16)
    base = pl.program_id(2) * chains
    contract_last = (((1,), (1,)), ((), ()))

    def mask_first_block(x):
        head = jnp.where(causal, x[:BLOCK], 0.0)
        return head if x.shape[0] == BLOCK else jnp.concatenate([head, x[BLOCK:]], axis=0)

    def key_block(rows, kb, on_diagonal):
        start = pl.multiple_of(kb * BLOCK, BLOCK)
        k = k_ref[pl.ds(start, BLOCK), :]
        v = v_ref[pl.ds(start, BLOCK), :]
        z = lax.dot_general(q_ref[rows, :], k, contract_last, preferred_element_type=F32) * scale
        drop = jnp.maximum(z, 0.0) + jnp.log(1.0 + jnp.exp2(jnp.abs(z) * -LOG2E))
        log_beta = z - drop
        if on_diagonal:
            drop = mask_first_block(drop)
        within = jnp.dot(drop.astype(BF16), suffix, preferred_element_type=F32)
        a = jnp.exp(log_beta - (within + spent_ref[rows, :]))
        if on_diagonal:
            a = mask_first_block(a)
        acc_ref[rows, :] += jnp.dot(a.astype(BF16), v, preferred_element_type=F32)
        spent_ref[rows, :] += jnp.sum(drop, axis=-1, keepdims=True)

    def retire_if_no_keys(rows, kb):
        spent_ref[rows, :] = jnp.where(kb >= 0, spent_ref[rows, :], -NEG_INF)

    acc_ref[...] = jnp.zeros_like(acc_ref)
    spent_ref[...] = jnp.zeros_like(spent_ref)
    for first in range(chains - 1, -SB_STATIC_DEPTH, -1):
        lo, hi = max(first, 0), min(first + SB_STATIC_DEPTH - 1, chains - 1)
        rows = slice(lo * BLOCK, (hi + 1) * BLOCK)
        kb = base + first
        if first >= 0:
            key_block(rows, kb, True)
        else:
            retire_if_no_keys(rows, kb)
            key_block(rows, jnp.maximum(kb, 0), False)

    def cond(carry):
        _, least_spent = carry
        return least_spent < -SB_EXIT_LOG

    def body(carry):
        delta, _ = carry
        for c in range(chains):
            rows = slice(c * BLOCK, (c + 1) * BLOCK)
            kb = base + c - delta
            retire_if_no_keys(rows, kb)
            key_block(rows, jnp.maximum(kb, 0), False)
        return delta + 1, jnp.min(spent_ref[...])

    lax.while_loop(cond, body, (jnp.int32(SB_STATIC_DEPTH), jnp.min(spent_ref[...])))
    o_ref[...] = acc_ref[...].astype(o_ref.dtype)


def _stick_breaking(qkv, B, S, tq):
    H = SB_HEADS
    nq = S // tq
    return pl.pallas_call(
        functools.partial(_sb_kernel, chains=tq // BLOCK),
        grid=(B, H, nq),
        in_specs=[
            pl.BlockSpec((None, tq, HEAD_DIM), lambda b, h, i: (h, b * nq + i, 0)),
            pl.BlockSpec((None, S, HEAD_DIM), lambda b, h, i: (H + h, b, 0)),
            pl.BlockSpec((None, S, HEAD_DIM), lambda b, h, i: (2 * H + h, b, 0)),
        ],
        out_specs=pl.BlockSpec((tq, HEAD_DIM), lambda b, h, i: (b * nq + i, h)),
        out_shape=jax.ShapeDtypeStruct((B * S, H * HEAD_DIM), BF16),
        scratch_shapes=[pltpu.VMEM((tq, HEAD_DIM), F32), pltpu.VMEM((tq, 1), F32)],
        compiler_params=_params(("parallel", "parallel", "arbitrary")),
        name="stick_breaking",
    )(qkv, qkv, qkv)


def _residue_major_permutation():
    i = np.arange(DIL_TILE)
    p = np.zeros((DIL_TILE, DIL_TILE), np.float32)
    p[i, (i % DIL_CHUNK) * DIL_RES + i // DIL_CHUNK] = 1.0
    return p


def _norm_matmul_perm_kernel(x_ref, g_ref, p_ref, w_ref, o_ref, hn_ref, hnp_ref, *, natural_tiles, chunk):
    j = pl.program_id(1)

    @pl.when(j == 0)
    def _():
        _norm_into(x_ref, g_ref, hn_ref, 0, 128)
        for c in range(hn_ref.shape[1] // chunk):
            cols = slice(c * chunk, (c + 1) * chunk)
            hnp_ref[:, cols] = jnp.dot(p_ref[...], hn_ref[:, cols],
                                       preferred_element_type=F32).astype(hnp_ref.dtype)

    @pl.when(j < natural_tiles)
    def _():
        o_ref[...] = jnp.dot(hn_ref[...], w_ref[...], preferred_element_type=F32).astype(o_ref.dtype)

    @pl.when(j >= natural_tiles)
    def _():
        o_ref[...] = jnp.dot(hnp_ref[...], w_ref[...], preferred_element_type=F32).astype(o_ref.dtype)


def _norm_matmul_perm(x, g, w, perm, natural_cols, tn):
    M, K = x.shape
    N = w.shape[1]
    tm = DIL_TILE
    return pl.pallas_call(
        functools.partial(_norm_matmul_perm_kernel, natural_tiles=natural_cols // tn, chunk=512),
        grid=(M // tm, N // tn),
        in_specs=[
            pl.BlockSpec((tm, K), lambda i, j: (i, 0)),
            pl.BlockSpec((1, K), lambda i, j: (0, 0)),
            pl.BlockSpec((tm, tm), lambda i, j: (0, 0)),
            pl.BlockSpec((K, tn), lambda i, j: (0, j)),
        ],
        out_specs=pl.BlockSpec((tm, tn), lambda i, j: (i, j)),
        out_shape=jax.ShapeDtypeStruct((M, N), BF16),
        scratch_shapes=[pltpu.VMEM((tm, K), BF16), pltpu.VMEM((tm, K), BF16)],
        compiler_params=_params(("parallel", "arbitrary")),
        name="norm_matmul_perm",
    )(x, g.reshape(1, K), perm, w)


def _rel_bucket_table(dil, order):
    order = np.asarray(order)
    qi = order[:, None]
    kc = np.concatenate([order, BLOCK + order])[None, :]
    rel = BLOCK + qi - kc
    n = jnp.asarray(np.maximum(rel, 0) * dil)
    max_exact = REL_BUCKETS // 2
    nf = jnp.maximum(n, 1).astype(F32)
    large = max_exact + (jnp.log(nf / max_exact) / math.log(REL_MAX_DISTANCE / max_exact)
                         * (REL_BUCKETS - max_exact)).astype(jnp.int32)
    large = jnp.minimum(large, REL_BUCKETS - 1)
    bucket = jnp.where(n < max_exact, n, large).astype(jnp.int32)
    return jnp.where(jnp.asarray((rel >= 0) & (rel <= BLOCK)), bucket, -1)


class _NaturalRows:
    @staticmethod
    def load(ref, j, cols):
        return ref[j * BLOCK:(j + 1) * BLOCK, cols]

    @staticmethod
    def load_prev(ref, cols):
        return ref[:, cols]

    @staticmethod
    def store(ref, j, cols, val):
        ref[j * BLOCK:(j + 1) * BLOCK, cols] = val


class _Residue16Rows:
    per = BLOCK // DIL_CHUNK

    @classmethod
    def load(cls, ref, j, cols):
        return jnp.concatenate([ref[cls.per * j + t, :, cols] for t in range(cls.per)], axis=0)

    @classmethod
    def load_prev(cls, ref, cols):
        return jnp.concatenate([ref[t, :, cols] for t in range(cls.per)], axis=0)

    @classmethod
    def store(cls, ref, j, cols, val):
        for t in range(cls.per):
            ref[cls.per * j + t, :, cols] = val[t * DIL_CHUNK:(t + 1) * DIL_CHUNK]


class _Residue4Rows:
    sub = BLOCK // 4
    per_tile = DIL_CHUNK // sub
    order = [4 * m + a for a in range(4) for m in range(BLOCK // 4)]

    @classmethod
    def _rows(cls, j):
        return j // cls.per_tile, slice((j % cls.per_tile) * cls.sub, (j % cls.per_tile + 1) * cls.sub)

    @classmethod
    def load(cls, ref, j, cols):
        t, ms = cls._rows(j)
        return jnp.concatenate([ref[t, a, ms, cols] for a in range(4)], axis=0)

    @classmethod
    def load_prev(cls, ref, cols):
        return cls.load(ref, cls.per_tile - 1, cols)

    @classmethod
    def store(cls, ref, j, cols, val):
        t, ms = cls._rows(j)
        for a in range(4):
            ref[t, a, ms, cols] = val[a * cls.sub:(a + 1) * cls.sub]


def _dil_kernel(bucket_ref, table_ref, q_ref, kc_ref, kp_ref, vc_ref, vp_ref, o_ref, lse_ref, bias_ref,
                *, sub_blocks, rows):
    scale = HEAD_DIM ** -0.5
    first_step = jnp.logical_and(jnp.logical_and(pl.program_id(0) == 0, pl.program_id(1) == 0),
                                 pl.program_id(2) == 0)

    @pl.when(first_step)
    def _():
        bucket = bucket_ref[...]
        for h in range(DIL_GROUP_HEADS):
            b = jnp.full(bucket.shape, NEG_INF, F32)
            for t in range(REL_BUCKETS):
                b = jnp.where(bucket == t, table_ref[t, h], b)
            bias_ref[h] = b

    lane = lax.broadcasted_iota(jnp.int32, (BLOCK, 128), 1)
    keep_keys = jnp.logical_or(lax.broadcasted_iota(jnp.int32, (BLOCK, 2 * BLOCK), 1) >= BLOCK,
                               pl.program_id(2) > 0)
    ones = jnp.ones((2 * BLOCK, HEAD_DIM), BF16)
    contract_last = (((1,), (1,)), ((), ()))
    all_lanes = slice(None)

    def prev_and_cur(cur_ref, prev_ref, j, cols):
        prev = rows.load_prev(prev_ref, cols) if j == 0 else rows.load(cur_ref, j - 1, cols)
        return jnp.concatenate([prev, rows.load(cur_ref, j, cols)], axis=0)

    for j in range(sub_blocks):
        stats = jnp.zeros((BLOCK, 128), F32)
        for h in range(DIL_GROUP_HEADS):
            cols = slice(h * HEAD_DIM, (h + 1) * HEAD_DIM)
            q = rows.load(q_ref, j, cols)
            k = prev_and_cur(kc_ref, kp_ref, j, cols)
            v = prev_and_cur(vc_ref, vp_ref, j, cols)
            s = lax.dot_general(q, k, contract_last, preferred_element_type=F32) * scale + bias_ref[h]
            if j == 0:
                s = jnp.where(keep_keys, s, NEG_INF)
            m = jnp.max(s, axis=-1, keepdims=True)
            p = jnp.exp(s - m).astype(BF16)
            ol = jnp.dot(p, jnp.concatenate([v, ones], axis=1), preferred_element_type=F32)
            o, l = ol[:, :HEAD_DIM], ol[:, HEAD_DIM:]
            rows.store(o_ref, j, cols, (o / l).astype(o_ref.dtype))
            stats = jnp.where(lane == h, m + jnp.log(l), stats)
        rows.store(lse_ref, j, all_lanes, stats)


def _dilated_group(qkv, rel_table, B, S, g, dil):
    W, C = DIL_WIDTH, qkv.shape[1]
    tiles = S // DIL_TILE
    col0 = g * 3
    if dil == 1:
        rows, n_res = _NaturalRows, 1
        blocks = min(4, S // BLOCK)
        steps = S // (blocks * BLOCK)
        views = [(B * S, C), (B * S, W), (B * S, 128)]
        cur_shape, prev_shape = (blocks * BLOCK,), (BLOCK,)
        cur_idx = lambda b, r, i: (b * steps + i,)
        prev_idx = lambda b, r, i: (b * (S // BLOCK) + jnp.maximum(i * blocks - 1, 0),)
    elif dil == DIL_RES:
        rows, n_res = _Residue16Rows, DIL_RES
        per = _Residue16Rows.per
        t_step = min(4 * per, tiles)
        blocks, steps = t_step // per, tiles // t_step
        views = [(B * tiles, DIL_RES, DIL_CHUNK, c) for c in (C, W, 128)]
        cur_shape, prev_shape = (t_step, None, DIL_CHUNK), (per, None, DIL_CHUNK)
        cur_idx = lambda b, r, i: (b * steps + i, r, 0)
        prev_idx = lambda b, r, i: (b * (tiles // per) + jnp.maximum(i * blocks - 1, 0), r, 0)
    else:
        assert dil == 4 and DIL_RES == 16
        rows, n_res = _Residue4Rows, 4
        per_tile = _Residue4Rows.per_tile
        t_step = min(2, tiles)
        blocks, steps = t_step * per_tile, tiles // t_step
        views = [(B * tiles, 4, 4, DIL_CHUNK, c) for c in (C, W, 128)]
        cur_shape, prev_shape = (t_step, 4, None, DIL_CHUNK), (1, 4, None, DIL_CHUNK)
        cur_idx = lambda b, r, i: (b * steps + i, 0, r, 0)
        prev_idx = lambda b, r, i: (b * tiles + jnp.maximum(i * t_step - 1, 0), 0, r, 0)

    def spec(shape, idx, width, col):
        return pl.BlockSpec(shape + (width,), lambda b, r, i: idx(b, r, i) + (col,))

    qkv_v = qkv.reshape(views[0])
    order = getattr(rows, "order", list(range(BLOCK)))
    o, lse = pl.pallas_call(
        functools.partial(_dil_kernel, sub_blocks=blocks, rows=rows),
        grid=(B, n_res, steps),
        in_specs=[
            pl.BlockSpec((BLOCK, 2 * BLOCK), lambda b, r, i: (0, 0)),
            pl.BlockSpec(memory_space=pltpu.SMEM),
            spec(cur_shape, cur_idx, W, col0),
            spec(cur_shape, cur_idx, W, col0 + 1), spec(prev_shape, prev_idx, W, col0 + 1),
            spec(cur_shape, cur_idx, W, col0 + 2), spec(prev_shape, prev_idx, W, col0 + 2),
        ],
        out_specs=[spec(cur_shape, cur_idx, W, 0), spec(cur_shape, cur_idx, 128, 0)],
        out_shape=[jax.ShapeDtypeStruct(views[1], BF16), jax.ShapeDtypeStruct(views[2], F32)],
        scratch_shapes=[pltpu.VMEM((DIL_GROUP_HEADS, BLOCK, 2 * BLOCK), F32)],
        compiler_params=_params(("arbitrary", "arbitrary", "arbitrary")),
        name=f"dilated_group{g}",
    )(_rel_bucket_table(dil, order), rel_table, qkv_v, qkv_v, qkv_v, qkv_v, qkv_v)
    return o.reshape(B * S, W), lse.reshape(B * S, 128)


def _bf16_pieces(x):
    hi = x.astype(BF16)
    rest = x - hi.astype(F32)
    mid = rest.astype(BF16)
    return [hi, mid, (rest - mid.astype(F32)).astype(BF16)]


def _combine_matmul_res_kernel(o0_ref, o1_ref, o2_ref, l0_ref, l1_ref, l2_ref, pt_ref, w_ref, r_ref, out_ref,
                               comb_ref):
    rows = out_ref.shape[0]
    pt = pt_ref[pl.ds(pl.multiple_of(pl.program_id(1) * rows, rows), rows), :]
    pieces = jnp.concatenate(_bf16_pieces(l1_ref[...]) + _bf16_pieces(l2_ref[...]), axis=1)
    lse = jnp.dot(pt, pieces, preferred_element_type=F32)
    l0 = l0_ref[...]
    l1 = lse[:, 0:128] + lse[:, 128:256] + lse[:, 256:384]
    l2 = lse[:, 384:512] + lse[:, 512:640] + lse[:, 640:768]
    mx = jnp.maximum(jnp.maximum(l0, l1), l2)
    e0, e1, e2 = jnp.exp(l0 - mx), jnp.exp(l1 - mx), jnp.exp(l2 - mx)
    den = e0 + e1 + e2
    w0, w1, w2 = e0 / den, e1 / den, e2 / den
    pair = 2 * HEAD_DIM
    for hp in range(DIL_GROUP_HEADS // 2):
        pcols = slice(hp * pair, (hp + 1) * pair)
        o1 = jnp.dot(pt, o1_ref[:, pcols], preferred_element_type=F32)
        o2 = jnp.dot(pt, o2_ref[:, pcols], preferred_element_type=F32)
        for k in range(2):
            h = 2 * hp + k
            cols = slice(h * HEAD_DIM, (h + 1) * HEAD_DIM)
            sub = slice(k * HEAD_DIM, (k + 1) * HEAD_DIM)
            c = (w0[:, h:h + 1] * o0_ref[:, cols].astype(F32)
                 + w1[:, h:h + 1] * o1[:, sub] + w2[:, h:h + 1] * o2[:, sub])
            comb_ref[:, cols] = c.astype(comb_ref.dtype)
    out_ref[...] = r_ref[...] + jnp.dot(comb_ref[...], w_ref[...], preferred_element_type=F32)


def _combine_matmul_res(os, lses, unperm, w, res, rows):
    M, K = os[0].shape
    N = w.shape[1]
    parts = DIL_TILE // rows
    own_o = pl.BlockSpec((rows, K), lambda i, p: (i * parts + p, 0))
    own_l = pl.BlockSpec((rows, 128), lambda i, p: (i * parts + p, 0))
    tile_o = pl.BlockSpec((DIL_TILE, K), lambda i, p: (i, 0))
    tile_l = pl.BlockSpec((DIL_TILE, 128), lambda i, p: (i, 0))
    row_block = pl.BlockSpec((rows, N), lambda i, p: (i * parts + p, 0))
    return pl.pallas_call(
        _combine_matmul_res_kernel,
        grid=(M // DIL_TILE, parts),
        in_specs=[own_o, tile_o, tile_o, own_l, tile_l, tile_l,
                  pl.BlockSpec((DIL_TILE, DIL_TILE), lambda i, p: (0, 0), pipeline_mode=pl.Buffered(1)),
                  pl.BlockSpec((K, N), lambda i, p: (0, 0), pipeline_mode=pl.Buffered(1)),
                  row_block],
        out_specs=row_block,
        out_shape=jax.ShapeDtypeStruct((M, N), F32),
        scratch_shapes=[pltpu.VMEM((rows, K), BF16)],
        compiler_params=_params(("parallel", "parallel")),
        name="combine_matmul_res",
    )(*os, *lses, unperm, w, res)


def _ffn_up_kernel(x_ref, xp_ref, g_ref, wg_ref, wv_ref, cw_ref, cb_ref, o_ref, hn_ref,
                   *, tiles_per_seq, col_chunks):
    @pl.when(pl.program_id(1) == 0)
    def _():
        halo = _rms_rows(xp_ref[...], g_ref[...])
        halo = jnp.where(pl.program_id(0) % tiles_per_seq == 0, 0.0, halo)
        hn_ref[0:CONV_HALO, :] = halo.astype(hn_ref.dtype)
        _norm_into(x_ref, g_ref, hn_ref, CONV_HALO, 128)

    width = o_ref.shape[1] // col_chunks
    for c in range(col_chunks):
        cols = slice(c * width, (c + 1) * width)
        gate = jnp.dot(hn_ref[...], wg_ref[:, cols], preferred_element_type=F32)
        val = jnp.dot(hn_ref[CONV_HALO:, :], wv_ref[:, cols], preferred_element_type=F32)
        conv = cb_ref[:, cols] + gate[CONV_HALO:] * cw_ref[CONV_WIDTH - 1:CONV_WIDTH, cols]
        for back in range(1, CONV_WIDTH):
            tap = CONV_WIDTH - 1 - back
            conv = conv + pltpu.roll(gate, back, axis=0)[CONV_HALO:] * cw_ref[tap:tap + 1, cols]
        act = conv / (1.0 + jnp.exp(-conv)) * val
        o_ref[:, cols] = act.astype(o_ref.dtype)


def _ffn_up(x, g, w_up, conv_w, conv_b, S, tm, tf):
    M, K = x.shape
    d_ff = w_up.shape[1] // 2
    nf = d_ff // tf
    halo_blocks_per_tile = tm // CONV_HALO
    return pl.pallas_call(
        functools.partial(_ffn_up_kernel, tiles_per_seq=S // tm, col_chunks=1),
        grid=(M // tm, nf),
        in_specs=[
            pl.BlockSpec((tm, K), lambda i, j: (i, 0)),
            pl.BlockSpec((CONV_HALO, K), lambda i, j: (jnp.maximum(i * halo_blocks_per_tile - 1, 0), 0)),
            pl.BlockSpec((1, K), lambda i, j: (0, 0)),
            pl.BlockSpec((K, tf), lambda i, j: (0, j)),
            pl.BlockSpec((K, tf), lambda i, j: (0, nf + j)),
            pl.BlockSpec((CONV_WIDTH, tf), lambda i, j: (0, j)),
            pl.BlockSpec((1, tf), lambda i, j: (0, j)),
        ],
        out_specs=pl.BlockSpec((tm, tf), lambda i, j: (i, j)),
        out_shape=jax.ShapeDtypeStruct((M, d_ff), BF16),
        scratch_shapes=[pltpu.VMEM((tm + CONV_HALO, K), BF16)],
        compiler_params=_params(("parallel", "arbitrary")),
        name="ffn_up",
    )(x, x, g.reshape(1, K), w_up, w_up, conv_w, conv_b.reshape(1, d_ff))


def kernel(x, ln_mix, ln_ffn, ln_f, w_qkv_sb, w_o_sb, w_qkv_dil, w_o_dil, rel_bias, w_up, conv_w, conv_b, w_down):
    B, S, D = x.shape
    depth = ln_mix.shape[0]
    h = x.reshape(B * S, D)
    for i in range(depth):
        j = i // N_MIXERS
        if i % N_MIXERS == 0:
            qkv = _norm_matmul_heads(h, ln_mix[i], w_qkv_sb[j].astype(BF16), tm=1024, tn=1024)
            o = _stick_breaking(qkv, B, S, tq=min(2048, S))
            h = _matmul_res(o, w_o_sb[j].astype(BF16), h, tm=512)
        else:
            perm = _residue_major_permutation()
            qkv = _norm_matmul_perm(h, ln_mix[i], w_qkv_dil[j].astype(BF16), jnp.asarray(perm, BF16),
                                    natural_cols=3 * DIL_WIDTH, tn=1024)
            os, lses = [], []
            for g, (window, dil) in enumerate(DIL_PATTERNS):
                assert window // dil == BLOCK and S % (dil * BLOCK) == 0
                table = rel_bias[:, g * DIL_GROUP_HEADS:(g + 1) * DIL_GROUP_HEADS]
                o, lse = _dilated_group(qkv, table, B, S, g, dil)
                os.append(o)
                lses.append(lse)
            h = _combine_matmul_res(os, lses, jnp.asarray(perm.T, BF16), w_o_dil[j].astype(BF16), h, rows=512)
        act = _ffn_up(h, ln_ffn[i], w_up[i].astype(BF16), conv_w[i], conv_b[i], S, tm=1024, tf=512)
        h = _matmul_res(act, w_down[i].astype(BF16), h, tm=512, norm_gain=ln_f if i == depth - 1 else None)
    return h.reshape(B, S, D)
```

```python
import functools
import math

import jax
import jax.numpy as jnp
import numpy as np
from jax import lax
from jax.experimental import pallas as pl
from jax.experimental.pallas import tpu as pltpu

BLOCK = 128
SB_HEADS = 16
HEAD_DIM = 128
DIL_PATTERNS = ((128, 1), (512, 4), (2048, 16))
DIL_GROUPS = len(DIL_PATTERNS)
DIL_GROUP_HEADS = 8
DIL_WIDTH = DIL_GROUP_HEADS * HEAD_DIM
REL_BUCKETS = 32
REL_MAX_DISTANCE = 2048
CONV_WIDTH = 3
RMS_EPS = 1e-6
NEG_INF = -1e30
LOG2E = math.log2(math.e)
N_MIXERS = 2

SB_EXIT_LOG = -104.0
SB_STATIC_DEPTH = 3

DIL_TILE = 1024
DIL_RES = 16
DIL_CHUNK = DIL_TILE // DIL_RES
CONV_HALO = 16
V7X_VMEM_BYTES = 64 * 1024 * 1024
VMEM_LIMIT = V7X_VMEM_BYTES * 7 // 8

BF16 = jnp.bfloat16
F32 = jnp.float32


def _params(semantics):
    return pltpu.CompilerParams(dimension_semantics=semantics, vmem_limit_bytes=VMEM_LIMIT)


def _rms_rows(x, g):
    ms = jnp.mean(x * x, axis=-1, keepdims=True)
    return x * lax.rsqrt(ms + RMS_EPS) * g


def _norm_into(x_ref, g_ref, dst_ref, dst_row0, chunk):
    g = g_ref[...]

    def body(c, carry):
        r = pl.multiple_of(c * chunk, chunk)
        y = _rms_rows(x_ref[pl.ds(r, chunk), :], g)
        dst_ref[pl.ds(dst_row0 + r, chunk), :] = y.astype(dst_ref.dtype)
        return carry

    lax.fori_loop(0, x_ref.shape[0] // chunk, body, 0)


def _norm_matmul_kernel(x_ref, g_ref, w_ref, o_ref, hn_ref):
    @pl.when(pl.program_id(1) == 0)
    def _():
        _norm_into(x_ref, g_ref, hn_ref, 0, 128)

    y = jnp.dot(hn_ref[...], w_ref[...], preferred_element_type=F32).astype(o_ref.dtype)
    for c in range(o_ref.shape[0]):
        o_ref[c] = y[:, c * HEAD_DIM:(c + 1) * HEAD_DIM]


def _norm_matmul_heads(x, g, w, tm, tn):
    M, K = x.shape
    N = w.shape[1]
    slabs = tn // HEAD_DIM
    return pl.pallas_call(
        _norm_matmul_kernel,
        grid=(M // tm, N // tn),
        in_specs=[
            pl.BlockSpec((tm, K), lambda i, j: (i, 0)),
            pl.BlockSpec((1, K), lambda i, j: (0, 0)),
            pl.BlockSpec((K, tn), lambda i, j: (0, j)),
        ],
        out_specs=pl.BlockSpec((slabs, tm, HEAD_DIM), lambda i, j: (j, i, 0)),
        out_shape=jax.ShapeDtypeStruct((N // HEAD_DIM, M, HEAD_DIM), BF16),
        scratch_shapes=[pltpu.VMEM((tm, K), BF16)],
        compiler_params=_params(("parallel", "arbitrary")),
        name="norm_matmul",
    )(x, g.reshape(1, K), w)


def _matmul_res_kernel(a_ref, w_ref, r_ref, o_ref):
    o_ref[...] = r_ref[...] + jnp.dot(a_ref[...], w_ref[...], preferred_element_type=F32)


def _matmul_res_norm_kernel(a_ref, w_ref, r_ref, g_ref, o_ref):
    y = r_ref[...] + jnp.dot(a_ref[...], w_ref[...], preferred_element_type=F32)
    o_ref[...] = _rms_rows(y, g_ref[...])


def _matmul_res(a, w, res, tm, norm_gain=None):
    M, K = a.shape
    N = w.shape[1]
    in_specs = [
        pl.BlockSpec((tm, K), lambda i: (i, 0)),
        pl.BlockSpec((K, N), lambda i: (0, 0), pipeline_mode=pl.Buffered(1)),
        pl.BlockSpec((tm, N), lambda i: (i, 0)),
    ]
    args = [a, w, res]
    if norm_gain is not None:
        in_specs.append(pl.BlockSpec((1, N), lambda i: (0, 0)))
        args.append(norm_gain.reshape(1, N))
    return pl.pallas_call(
        _matmul_res_kernel if norm_gain is None else _matmul_res_norm_kernel,
        grid=(M // tm,),
        in_specs=in_specs,
        out_specs=pl.BlockSpec((tm, N), lambda i: (i, 0)),
        out_shape=jax.ShapeDtypeStruct((M, N), F32),
        compiler_params=_params(("parallel",)),
        name="matmul_res" if norm_gain is None else "matmul_res_norm",
    )(*args)


def _sb_kernel(q_ref, k_ref, v_ref, o_ref, acc_ref, spent_ref, *, chains, heads):
    scale = HEAD_DIM ** -0.5

    row = lax.broadcasted_iota(jnp.int32, (BLOCK, BLOCK), 0)
    col = lax.broadcasted_iota(jnp.int32, (BLOCK, BLOCK), 1)
    causal = col < row
    suffix = (row > col).astype(BF16)
    base = pl.program_id(2) * chains
    contract_last = (((1,), (1,)), ((), ()))

    def mask_first_block(x):
        head = jnp.where(causal, x[:BLOCK], 0.0)
        return head if x.shape[0] == BLOCK else jnp.concatenate([head, x[BLOCK:]], axis=0)

    def key_block(h, rows, kb, on_diagonal):
        start = pl.multiple_of(kb * BLOCK, BLOCK)
        k = k_ref[h, pl.ds(start, BLOCK), :]
        v = v_ref[h, pl.ds(start, BLOCK), :]
        z = lax.dot_general(q_ref[h, rows, :], k, contract_last, preferred_element_type=F32) * scale
        drop = jnp.maximum(z, 0.0) + jnp.log(1.0 + jnp.exp2(jnp.abs(z) * -LOG2E))
        log_beta = z - drop
        if on_diagonal:
            drop = mask_first_block(drop)
        drop = drop.astype(BF16)
        within = jnp.dot(drop, suffix, preferred_element_type=F32)
        a = jnp.exp(log_beta - (within + spent_ref[h, rows, :]))
        if on_diagonal:
            a = mask_first_block(a)
        acc_ref[h, rows, :] += jnp.dot(a.astype(BF16), v, preferred_element_type=F32)
        spent_ref[h, rows, :] += within[:, 0:1] + drop[:, 0:1].astype(F32)

    def retire_if_no_keys(h, rows, kb):
        spent_ref[h, rows, :] = jnp.where(kb >= 0, spent_ref[h, rows, :], -NEG_INF)

    acc_ref[...] = jnp.zeros_like(acc_ref)
    spent_ref[...] = jnp.zeros_like(spent_ref)
    for first in range(chains - 1, -SB_STATIC_DEPTH, -1):
        lo, hi = max(first, 0), min(first + SB_STATIC_DEPTH - 1, chains - 1)
        rows = slice(lo * BLOCK, (hi + 1) * BLOCK)
        kb = base + first
        for h in range(heads):
            if first >= 0:
                key_block(h, rows, kb, True)
            else:
                retire_if_no_keys(h, rows, kb)
                key_block(h, rows, jnp.maximum(kb, 0), False)

    def cond(carry):
        _, least_spent = carry
        return least_spent < -SB_EXIT_LOG

    def body(carry):
        delta, _ = carry
        for c in range(chains):
            rows = slice(c * BLOCK, (c + 1) * BLOCK)
            kb = base + c - delta
            for h in range(heads):
                retire_if_no_keys(h, rows, kb)
                key_block(h, rows, jnp.maximum(kb, 0), False)
        return delta + 1, jnp.min(spent_ref[...])

    lax.while_loop(cond, body, (jnp.int32(SB_STATIC_DEPTH), jnp.min(spent_ref[...])))
    for h in range(heads):
        o_ref[:, h * HEAD_DIM:(h + 1) * HEAD_DIM] = acc_ref[h].astype(o_ref.dtype)


def _stick_breaking(qkv, B, S, tq, heads):
    H = SB_HEADS
    nq = S // tq
    hb = H // heads
    return pl.pallas_call(
        functools.partial(_sb_kernel, chains=tq // BLOCK, heads=heads),
        grid=(B, hb, nq),
        in_specs=[
            pl.BlockSpec((heads, tq, HEAD_DIM), lambda b, h, i: (h, b * nq + i, 0)),
            pl.BlockSpec((heads, S, HEAD_DIM), lambda b, h, i: (hb + h, b, 0)),
            pl.BlockSpec((heads, S, HEAD_DIM), lambda b, h, i: (2 * hb + h, b, 0)),
        ],
        out_specs=pl.BlockSpec((tq, heads * HEAD_DIM), lambda b, h, i: (b * nq + i, h)),
        out_shape=jax.ShapeDtypeStruct((B * S, H * HEAD_DIM), BF16),
        scratch_shapes=[pltpu.VMEM((heads, tq, HEAD_DIM), F32), pltpu.VMEM((heads, tq, BLOCK), F32)],
        compiler_params=_params(("parallel", "parallel", "arbitrary")),
        name="stick_breaking",
    )(qkv, qkv, qkv)


def _residue_major_permutation():
    i = np.arange(DIL_TILE)
    p = np.zeros((DIL_TILE, DIL_TILE), np.float32)
    p[i, (i % DIL_CHUNK) * DIL_RES + i // DIL_CHUNK] = 1.0
    return p


def _norm_matmul_perm_kernel(x_ref, g_ref, p_ref, w_ref, o_ref, hn_ref, hnp_ref, *, natural_tiles, chunk):
    j = pl.program_id(1)

    @pl.when(j == 0)
    def _():
        _norm_into(x_ref, g_ref, hn_ref, 0, 128)
        for c in range(hn_ref.shape[1] // chunk):
            cols = slice(c * chunk, (c + 1) * chunk)
            hnp_ref[:, cols] = jnp.dot(p_ref[...], hn_ref[:, cols],
                                       preferred_element_type=F32).astype(hnp_ref.dtype)

    @pl.when(j < natural_tiles)
    def _():
        o_ref[...] = jnp.dot(hn_ref[...], w_ref[...], preferred_element_type=F32).astype(o_ref.dtype)

    @pl.when(j >= natural_tiles)
    def _():
        o_ref[...] = jnp.dot(hnp_ref[...], w_ref[...], preferred_element_type=F32).astype(o_ref.dtype)


def _norm_matmul_perm(x, g, w, perm, natural_cols, tn):
    M, K = x.shape
    N = w.shape[1]
    tm = DIL_TILE
    return pl.pallas_call(
        functools.partial(_norm_matmul_perm_kernel, natural_tiles=natural_cols // tn, chunk=512),
        grid=(M // tm, N // tn),
        in_specs=[
            pl.BlockSpec((tm, K), lambda i, j: (i, 0)),
            pl.BlockSpec((1, K), lambda i, j: (0, 0)),
            pl.BlockSpec((tm, tm), lambda i, j: (0, 0)),
            pl.BlockSpec((K, tn), lambda i, j: (0, j)),
        ],
        out_specs=pl.BlockSpec((tm, tn), lambda i, j: (i, j)),
        out_shape=jax.ShapeDtypeStruct((M, N), BF16),
        scratch_shapes=[pltpu.VMEM((tm, K), BF16), pltpu.VMEM((tm, K), BF16)],
        compiler_params=_params(("parallel", "arbitrary")),
        name="norm_matmul_perm",
    )(x, g.reshape(1, K), perm, w)


def _rel_bucket_table(dil, order):
    order = np.asarray(order)
    qi = order[:, None]
    kc = np.concatenate([order, BLOCK + order])[None, :]
    rel = BLOCK + qi - kc
    n = jnp.asarray(np.maximum(rel, 0) * dil)
    max_exact = REL_BUCKETS // 2
    nf = jnp.maximum(n, 1).astype(F32)
    large = max_exact + (jnp.log(nf / max_exact) / math.log(REL_MAX_DISTANCE / max_exact)
                         * (REL_BUCKETS - max_exact)).astype(jnp.int32)
    large = jnp.minimum(large, REL_BUCKETS - 1)
    bucket = jnp.where(n < max_exact, n, large).astype(jnp.int32)
    return jnp.where(jnp.asarray((rel >= 0) & (rel <= BLOCK)), bucket, -1)


class _NaturalRows:
    @staticmethod
    def load(ref, j, cols):
        return ref[j * BLOCK:(j + 1) * BLOCK, cols]

    @staticmethod
    def load_prev(ref, cols):
        return ref[:, cols]

    @staticmethod
    def store(ref, j, cols, val):
        ref[j * BLOCK:(j + 1) * BLOCK, cols] = val


class _Residue16Rows:
    per = BLOCK // DIL_CHUNK

    @classmethod
    def load(cls, ref, j, cols):
        return jnp.concatenate([ref[cls.per * j + t, :, cols] for t in range(cls.per)], axis=0)

    @classmethod
    def load_prev(cls, ref, cols):
        return jnp.concatenate([ref[t, :, cols] for t in range(cls.per)], axis=0)

    @classmethod
    def store(cls, ref, j, cols, val):
        for t in range(cls.per):
            ref[cls.per * j + t, :, cols] = val[t * DIL_CHUNK:(t + 1) * DIL_CHUNK]


class _Residue4Rows:
    sub = BLOCK // 4
    per_tile = DIL_CHUNK // sub
    order = [4 * m + a for a in range(4) for m in range(BLOCK // 4)]

    @classmethod
    def _rows(cls, j):
        return j // cls.per_tile, slice((j % cls.per_tile) * cls.sub, (j % cls.per_tile + 1) * cls.sub)

    @classmethod
    def load(cls, ref, j, cols):
        t, ms = cls._rows(j)
        return jnp.concatenate([ref[t, a, ms, cols] for a in range(4)], axis=0)

    @classmethod
    def load_prev(cls, ref, cols):
        return cls.load(ref, cls.per_tile - 1, cols)

    @classmethod
    def store(cls, ref, j, cols, val):
        t, ms = cls._rows(j)
        for a in range(4):
            ref[t, a, ms, cols] = val[a * cls.sub:(a + 1) * cls.sub]


def _dil_kernel(bucket_ref, table_ref, q_ref, kc_ref, kp_ref, vc_ref, vp_ref, o_ref, lse_ref, bias_ref,
                *, sub_blocks, rows):
    scale = HEAD_DIM ** -0.5
    first_step = jnp.logical_and(jnp.logical_and(pl.program_id(0) == 0, pl.program_id(1) == 0),
                                 pl.program_id(2) == 0)

    @pl.when(first_step)
    def _():
        bucket = bucket_ref[...]
        for h in range(DIL_GROUP_HEADS):
            b = jnp.full(bucket.shape, NEG_INF, F32)
            for t in range(REL_BUCKETS):
                b = jnp.where(bucket == t, table_ref[t, h], b)
            bias_ref[h] = b

    lane = lax.broadcasted_iota(jnp.int32, (BLOCK, 128), 1)
    keep_keys = jnp.logical_or(lax.broadcasted_iota(jnp.int32, (BLOCK, 2 * BLOCK), 1) >= BLOCK,
                               pl.program_id(2) > 0)
    ones = jnp.ones((2 * BLOCK, HEAD_DIM), BF16)
    contract_last = (((1,), (1,)), ((), ()))
    all_lanes = slice(None)

    def prev_and_cur(cur_ref, prev_ref, j, cols):
        prev = rows.load_prev(prev_ref, cols) if j == 0 else rows.load(cur_ref, j - 1, cols)
        return jnp.concatenate([prev, rows.load(cur_ref, j, cols)], axis=0)

    for j in range(sub_blocks):
        stats = jnp.zeros((BLOCK, 128), F32)
        for h in range(DIL_GROUP_HEADS):
            cols = slice(h * HEAD_DIM, (h + 1) * HEAD_DIM)
            q = rows.load(q_ref, j, cols)
            k = prev_and_cur(kc_ref, kp_ref, j, cols)
            v = prev_and_cur(vc_ref, vp_ref, j, cols)
            s = lax.dot_general(q, k, contract_last, preferred_element_type=F32) * scale + bias_ref[h]
            if j == 0:
                s = jnp.where(keep_keys, s, NEG_INF)
            m = jnp.max(s, axis=-1, keepdims=True)
            p = jnp.exp(s - m).astype(BF16)
            ol = jnp.dot(p, jnp.concatenate([v, ones], axis=1), preferred_element_type=F32)
            o, l = ol[:, :HEAD_DIM], ol[:, HEAD_DIM:]
            rows.store(o_ref, j, cols, (o / l).astype(o_ref.dtype))
            stats = jnp.where(lane == h, m + jnp.log(l), stats)
        rows.store(lse_ref, j, all_lanes, stats)


def _dilated_group(qkv, rel_table, B, S, g, dil):
    W, C = DIL_WIDTH, qkv.shape[1]
    tiles = S // DIL_TILE
    col0 = g * 3
    if dil == 1:
        rows, n_res = _NaturalRows, 1
        blocks = min(4, S // BLOCK)
        steps = S // (blocks * BLOCK)
        views = [(B * S, C), (B * S, W), (B * S, 128)]
        cur_shape, prev_shape = (blocks * BLOCK,), (BLOCK,)
        cur_idx = lambda b, r, i: (b * steps + i,)
        prev_idx = lambda b, r, i: (b * (S // BLOCK) + jnp.maximum(i * blocks - 1, 0),)
    elif dil == DIL_RES:
        rows, n_res = _Residue16Rows, DIL_RES
        per = _Residue16Rows.per
        t_step = min(4 * per, tiles)
        blocks, steps = t_step // per, tiles // t_step
        views = [(B * tiles, DIL_RES, DIL_CHUNK, c) for c in (C, W, 128)]
        cur_shape, prev_shape = (t_step, None, DIL_CHUNK), (per, None, DIL_CHUNK)
        cur_idx = lambda b, r, i: (b * steps + i, r, 0)
        prev_idx = lambda b, r, i: (b * (tiles // per) + jnp.maximum(i * blocks - 1, 0), r, 0)
    else:
        assert dil == 4 and DIL_RES == 16
        rows, n_res = _Residue4Rows, 4
        per_tile = _Residue4Rows.per_tile
        t_step = min(2, tiles)
        blocks, steps = t_step * per_tile, tiles // t_step
        views = [(B * tiles, 4, 4, DIL_CHUNK, c) for c in (C, W, 128)]
        cur_shape, prev_shape = (t_step, 4, None, DIL_CHUNK), (1, 4, None, DIL_CHUNK)
        cur_idx = lambda b, r, i: (b * steps + i, 0, r, 0)
        prev_idx = lambda b, r, i: (b * tiles + jnp.maximum(i * t_step - 1, 0), 0, r, 0)

    def spec(shape, idx, width, col):
        return pl.BlockSpec(shape + (width,), lambda b, r, i: idx(b, r, i) + (col,))

    qkv_v = qkv.reshape(views[0])
    order = getattr(rows, "order", list(range(BLOCK)))
    o, lse = pl.pallas_call(
        functools.partial(_dil_kernel, sub_blocks=blocks, rows=rows),
        grid=(B, n_res, steps),
        in_specs=[
            pl.BlockSpec((BLOCK, 2 * BLOCK), lambda b, r, i: (0, 0)),
            pl.BlockSpec(memory_space=pltpu.SMEM),
            spec(cur_shape, cur_idx, W, col0),
            spec(cur_shape, cur_idx, W, col0 + 1), spec(prev_shape, prev_idx, W, col0 + 1),
            spec(cur_shape, cur_idx, W, col0 + 2), spec(prev_shape, prev_idx, W, col0 + 2),
        ],
        out_specs=[spec(cur_shape, cur_idx, W, 0), spec(cur_shape, cur_idx, 128, 0)],
        out_shape=[jax.ShapeDtypeStruct(views[1], BF16), jax.ShapeDtypeStruct(views[2], F32)],
        scratch_shapes=[pltpu.VMEM((DIL_GROUP_HEADS, BLOCK, 2 * BLOCK), F32)],
        compiler_params=_params(("arbitrary", "arbitrary", "arbitrary")),
        name=f"dilated_group{g}",
    )(_rel_bucket_table(dil, order), rel_table, qkv_v, qkv_v, qkv_v, qkv_v, qkv_v)
    return o.reshape(B * S, W), lse.reshape(B * S, 128)


def _bf16_pieces(x):
    hi = x.astype(BF16)
    rest = x - hi.astype(F32)
    mid = rest.astype(BF16)
    return [hi, mid, (rest - mid.astype(F32)).astype(BF16)]


def _combine_matmul_res_kernel(o0_ref, o1_ref, o2_ref, l0_ref, l1_ref, l2_ref, pt_ref, w_ref, r_ref, out_ref,
                               comb_ref):
    rows = out_ref.shape[0]
    pt = pt_ref[pl.ds(pl.multiple_of(pl.program_id(1) * rows, rows), rows), :]
    pieces = jnp.concatenate(_bf16_pieces(l1_ref[...]) + _bf16_pieces(l2_ref[...]), axis=1)
    lse = jnp.dot(pt, pieces, preferred_element_type=F32)
    l0 = l0_ref[...]
    l1 = lse[:, 0:128] + lse[:, 128:256] + lse[:, 256:384]
    l2 = lse[:, 384:512] + lse[:, 512:640] + lse[:, 640:768]
    mx = jnp.maximum(jnp.maximum(l0, l1), l2)
    e0, e1, e2 = jnp.exp(l0 - mx), jnp.exp(l1 - mx), jnp.exp(l2 - mx)
    den = e0 + e1 + e2
    w0, w1, w2 = e0 / den, e1 / den, e2 / den
    pair = 2 * HEAD_DIM
    for hp in range(DIL_GROUP_HEADS // 2):
        pcols = slice(hp * pair, (hp + 1) * pair)
        o1 = jnp.dot(pt, o1_ref[:, pcols], preferred_element_type=F32)
        o2 = jnp.dot(pt, o2_ref[:, pcols], preferred_element_type=F32)
        for k in range(2):
            h = 2 * hp + k
            cols = slice(h * HEAD_DIM, (h + 1) * HEAD_DIM)
            sub = slice(k * HEAD_DIM, (k + 1) * HEAD_DIM)
            c = (w0[:, h:h + 1] * o0_ref[:, cols].astype(F32)
                 + w1[:, h:h + 1] * o1[:, sub] + w2[:, h:h + 1] * o2[:, sub])
            comb_ref[:, cols] = c.astype(comb_ref.dtype)
    out_ref[...] = r_ref[...] + jnp.dot(comb_ref[...], w_ref[...], preferred_element_type=F32)


def _combine_matmul_res(os, lses, unperm, w, res, rows):
    M, K = os[0].shape
    N = w.shape[1]
    parts = DIL_TILE // rows
    own_o = pl.BlockSpec((rows, K), lambda i, p: (i * parts + p, 0))
    own_l = pl.BlockSpec((rows, 128), lambda i, p: (i * parts + p, 0))
    tile_o = pl.BlockSpec((DIL_TILE, K), lambda i, p: (i, 0))
    tile_l = pl.BlockSpec((DIL_TILE, 128), lambda i, p: (i, 0))
    row_block = pl.BlockSpec((rows, N), lambda i, p: (i * parts + p, 0))
    return pl.pallas_call(
        _combine_matmul_res_kernel,
        grid=(M // DIL_TILE, parts),
        in_specs=[own_o, tile_o, tile_o, own_l, tile_l, tile_l,
                  pl.BlockSpec((DIL_TILE, DIL_TILE), lambda i, p: (0, 0), pipeline_mode=pl.Buffered(1)),
                  pl.BlockSpec((K, N), lambda i, p: (0, 0), pipeline_mode=pl.Buffered(1)),
                  row_block],
        out_specs=row_block,
        out_shape=jax.ShapeDtypeStruct((M, N), F32),
        scratch_shapes=[pltpu.VMEM((rows, K), BF16)],
        compiler_params=_params(("parallel", "parallel")),
        name="combine_matmul_res",
    )(*os, *lses, unperm, w, res)


def _ffn_up_kernel(x_ref, xp_ref, g_ref, wg_ref, wv_ref, cw_ref, cb_ref, o_ref, hn_ref,
                   *, tiles_per_seq, col_chunks):
    @pl.when(pl.program_id(1) == 0)
    def _():
        halo = _rms_rows(xp_ref[...], g_ref[...])
        halo = jnp.where(pl.program_id(0) % tiles_per_seq == 0, 0.0, halo)
        hn_ref[0:CONV_HALO, :] = halo.astype(hn_ref.dtype)
        _norm_into(x_ref, g_ref, hn_ref, CONV_HALO, 128)

    width = o_ref.shape[1] // col_chunks
    for c in range(col_chunks):
        cols = slice(c * width, (c + 1) * width)
        gate = jnp.dot(hn_ref[...], wg_ref[:, cols], preferred_element_type=F32)
        val = jnp.dot(hn_ref[CONV_HALO:, :], wv_ref[:, cols], preferred_element_type=F32)
        conv = cb_ref[:, cols] + gate[CONV_HALO:] * cw_ref[CONV_WIDTH - 1:CONV_WIDTH, cols]
        for back in range(1, CONV_WIDTH):
            tap = CONV_WIDTH - 1 - back
            conv = conv + pltpu.roll(gate, back, axis=0)[CONV_HALO:] * cw_ref[tap:tap + 1, cols]
        act = conv / (1.0 + jnp.exp(-conv)) * val
        o_ref[:, cols] = act.astype(o_ref.dtype)


def _ffn_up(x, g, w_up, conv_w, conv_b, S, tm, tf):
    M, K = x.shape
    d_ff = w_up.shape[1] // 2
    nf = d_ff // tf
    halo_blocks_per_tile = tm // CONV_HALO
    return pl.pallas_call(
        functools.partial(_ffn_up_kernel, tiles_per_seq=S // tm, col_chunks=1),
        grid=(M // tm, nf),
        in_specs=[
            pl.BlockSpec((tm, K), lambda i, j: (i, 0)),
            pl.BlockSpec((CONV_HALO, K), lambda i, j: (jnp.maximum(i * halo_blocks_per_tile - 1, 0), 0)),
            pl.BlockSpec((1, K), lambda i, j: (0, 0)),
            pl.BlockSpec((K, tf), lambda i, j: (0, j)),
            pl.BlockSpec((K, tf), lambda i, j: (0, nf + j)),
            pl.BlockSpec((CONV_WIDTH, tf), lambda i, j: (0, j)),
            pl.BlockSpec((1, tf), lambda i, j: (0, j)),
        ],
        out_specs=pl.BlockSpec((tm, tf), lambda i, j: (i, j)),
        out_shape=jax.ShapeDtypeStruct((M, d_ff), BF16),
        scratch_shapes=[pltpu.VMEM((tm + CONV_HALO, K), BF16)],
        compiler_params=_params(("parallel", "arbitrary")),
        name="ffn_up",
    )(x, x, g.reshape(1, K), w_up, w_up, conv_w, conv_b.reshape(1, d_ff))


def kernel(x, ln_mix, ln_ffn, ln_f, w_qkv_sb, w_o_sb, w_qkv_dil, w_o_dil, rel_bias, w_up, conv_w, conv_b, w_down):
    B, S, D = x.shape
    depth = ln_mix.shape[0]
    h = x.reshape(B * S, D)
    for i in range(depth):
        j = i // N_MIXERS
        if i % N_MIXERS == 0:
            qkv = _norm_matmul_heads(h, ln_mix[i], w_qkv_sb[j].astype(BF16), tm=1024, tn=1024)
            o = _stick_breaking(qkv, B, S, tq=min(2048, S), heads=2)
            h = _matmul_res(o, w_o_sb[j].astype(BF16), h, tm=512)
        else:
            perm = _residue_major_permutation()
            qkv = _norm_matmul_perm(h, ln_mix[i], w_qkv_dil[j].astype(BF16), jnp.asarray(perm, BF16),
                                    natural_cols=3 * DIL_WIDTH, tn=1024)
            os, lses = [], []
            for g, (window, dil) in enumerate(DIL_PATTERNS):
                assert window // dil == BLOCK and S % (dil * BLOCK) == 0
                table = rel_bias[:, g * DIL_GROUP_HEADS:(g + 1) * DIL_GROUP_HEADS]
                o, lse = _dilated_group(qkv, table, B, S, g, dil)
                os.append(o)
                lses.append(lse)
            h = _combine_matmul_res(os, lses, jnp.asarray(perm.T, BF16), w_o_dil[j].astype(BF16), h, rows=512)
        act = _ffn_up(h, ln_ffn[i], w_up[i].astype(BF16), conv_w[i], conv_b[i], S, tm=1024, tf=512)
        h = _matmul_res(act, w_down[i].astype(BF16), h, tm=512, norm_gain=ln_f if i == depth - 1 else None)
    return h.reshape(B, S, D)
```

```python
import functools
import math

import jax
import jax.numpy as jnp
import numpy as np
from jax import lax
from jax.experimental import pallas as pl
from jax.experimental.pallas import tpu as pltpu

BLOCK = 128
SB_HEADS = 16
HEAD_DIM = 128
DIL_PATTERNS = ((128, 1), (512, 4), (2048, 16))
DIL_GROUPS = len(DIL_PATTERNS)
DIL_GROUP_HEADS = 8
DIL_WIDTH = DIL_GROUP_HEADS * HEAD_DIM
REL_BUCKETS = 32
REL_MAX_DISTANCE = 2048
CONV_WIDTH = 3
RMS_EPS = 1e-6
NEG_INF = -1e30
LOG2E = math.log2(math.e)
N_MIXERS = 2

SB_EXIT_LOG = -104.0
SB_STATIC_DEPTH = 3

DIL_TILE = 1024
DIL_RES = 16
DIL_CHUNK = DIL_TILE // DIL_RES
CONV_HALO = 16
NORM_CHUNK = 64
V7X_VMEM_BYTES = 64 * 1024 * 1024
VMEM_LIMIT = V7X_VMEM_BYTES * 7 // 8

BF16 = jnp.bfloat16
F32 = jnp.float32


def _params(semantics):
    return pltpu.CompilerParams(dimension_semantics=semantics, vmem_limit_bytes=VMEM_LIMIT)


def _rms_rows(x, g):
    ms = jnp.mean(x * x, axis=-1, keepdims=True)
    return x * lax.rsqrt(ms + RMS_EPS) * g


def _norm_into(x_ref, g_ref, dst_ref, dst_row0, chunk):
    g = g_ref[...]

    def body(c, carry):
        r = pl.multiple_of(c * chunk, chunk)
        y = _rms_rows(x_ref[pl.ds(r, chunk), :], g)
        dst_ref[pl.ds(dst_row0 + r, chunk), :] = y.astype(dst_ref.dtype)
        return carry

    lax.fori_loop(0, x_ref.shape[0] // chunk, body, 0)


def _norm_share(x_ref, g_ref, dst_ref, dst_row0, step, steps):
    n_chunks = x_ref.shape[0] // NORM_CHUNK
    per_step = -(-n_chunks // steps)
    g = g_ref[...]
    for t in range(per_step):
        c = jnp.minimum(step * per_step + t, n_chunks - 1)
        r = pl.multiple_of(c * NORM_CHUNK, NORM_CHUNK)
        y = _rms_rows(x_ref[pl.ds(r, NORM_CHUNK), :], g)
        dst_ref[pl.ds(dst_row0 + r, NORM_CHUNK), :] = y.astype(dst_ref.dtype)


def _norm_matmul_kernel(x_ref, g_ref, w_ref, o_ref, hn_even_ref, hn_odd_ref, *, steps):
    i, j = pl.program_id(0), pl.program_id(1)

    def project(hn_ref):
        y = jnp.dot(hn_ref[...], w_ref[...], preferred_element_type=F32).astype(o_ref.dtype)
        for c in range(o_ref.shape[0]):
            o_ref[c] = y[:, c * HEAD_DIM:(c + 1) * HEAD_DIM]

    @pl.when(i == 0)
    def _():
        _norm_share(x_ref, g_ref, hn_even_ref, 0, j, steps)

    @pl.when(i % 2 == 1)
    def _():
        _norm_share(x_ref, g_ref, hn_odd_ref, 0, j, steps)
        project(hn_even_ref)

    @pl.when(jnp.logical_and(i % 2 == 0, i > 0))
    def _():
        _norm_share(x_ref, g_ref, hn_even_ref, 0, j, steps)
        project(hn_odd_ref)


def _norm_matmul_heads(x, g, w, tm, tn):
    M, K = x.shape
    N = w.shape[1]
    slabs = tn // HEAD_DIM
    tiles, steps = M // tm, N // tn

    def col(i, j):
        return jnp.where(i > 0, j, 0)

    return pl.pallas_call(
        functools.partial(_norm_matmul_kernel, steps=steps),
        grid=(tiles + 1, steps),
        in_specs=[
            pl.BlockSpec((tm, K), lambda i, j: (jnp.minimum(i, tiles - 1), 0)),
            pl.BlockSpec((1, K), lambda i, j: (0, 0)),
            pl.BlockSpec((K, tn), lambda i, j: (0, col(i, j))),
        ],
        out_specs=pl.BlockSpec((slabs, tm, HEAD_DIM), lambda i, j: (col(i, j), jnp.maximum(i - 1, 0), 0)),
        out_shape=jax.ShapeDtypeStruct((N // HEAD_DIM, M, HEAD_DIM), BF16),
        scratch_shapes=[pltpu.VMEM((tm, K), BF16), pltpu.VMEM((tm, K), BF16)],
        compiler_params=_params(("arbitrary", "arbitrary")),
        name="norm_matmul",
    )(x, g.reshape(1, K), w)


def _matmul_res_kernel(a_ref, w_ref, r_ref, o_ref):
    o_ref[...] = r_ref[...] + jnp.dot(a_ref[...], w_ref[...], preferred_element_type=F32)


def _matmul_res_norm_kernel(a_ref, w_ref, r_ref, g_ref, o_ref):
    y = r_ref[...] + jnp.dot(a_ref[...], w_ref[...], preferred_element_type=F32)
    o_ref[...] = _rms_rows(y, g_ref[...])


def _matmul_res(a, w, res, tm, norm_gain=None):
    M, K = a.shape
    N = w.shape[1]
    in_specs = [
        pl.BlockSpec((tm, K), lambda i: (i, 0)),
        pl.BlockSpec((K, N), lambda i: (0, 0), pipeline_mode=pl.Buffered(1)),
        pl.BlockSpec((tm, N), lambda i: (i, 0)),
    ]
    args = [a, w, res]
    if norm_gain is not None:
        in_specs.append(pl.BlockSpec((1, N), lambda i: (0, 0)))
        args.append(norm_gain.reshape(1, N))
    return pl.pallas_call(
        _matmul_res_kernel if norm_gain is None else _matmul_res_norm_kernel,
        grid=(M // tm,),
        in_specs=in_specs,
        out_specs=pl.BlockSpec((tm, N), lambda i: (i, 0)),
        out_shape=jax.ShapeDtypeStruct((M, N), F32),
        compiler_params=_params(("parallel",)),
        name="matmul_res" if norm_gain is None else "matmul_res_norm",
    )(*args)


def _sb_kernel(q_ref, k_ref, v_ref, o_ref, acc_ref, spent_ref, *, chains, heads):
    scale = HEAD_DIM ** -0.5

    row = lax.broadcasted_iota(jnp.int32, (BLOCK, BLOCK), 0)
    col = lax.broadcasted_iota(jnp.int32, (BLOCK, BLOCK), 1)
    causal = col < row
    suffix = (row > col).astype(BF16)
    base = pl.program_id(2) * chains
    contract_last = (((1,), (1,)), ((), ()))

    def mask_first_block(x):
        head = jnp.where(causal, x[:BLOCK], 0.0)
        return head if x.shape[0] == BLOCK else jnp.concatenate([head, x[BLOCK:]], axis=0)

    def key_block(h, rows, kb, on_diagonal):
        start = pl.multiple_of(kb * BLOCK, BLOCK)
        k = k_ref[h, pl.ds(start, BLOCK), :]
        v = v_ref[h, pl.ds(start, BLOCK), :]
        z = lax.dot_general(q_ref[h, rows, :], k, contract_last, preferred_element_type=F32) * scale
        drop = jnp.maximum(z, 0.0) + jnp.log(1.0 + jnp.exp2(jnp.abs(z) * -LOG2E))
        log_beta = z - drop
        if on_diagonal:
            drop = mask_first_block(drop)
        drop = drop.astype(BF16)
        within = jnp.dot(drop, suffix, preferred_element_type=F32)
        a = jnp.exp(log_beta - (within + spent_ref[h, rows, :]))
        if on_diagonal:
            a = mask_first_block(a)
        acc_ref[h, rows, :] += jnp.dot(a.astype(BF16), v, preferred_element_type=F32)
        spent_ref[h, rows, :] += within[:, 0:1] + drop[:, 0:1].astype(F32)

    def retire_if_no_keys(h, rows, kb):
        spent_ref[h, rows, :] = jnp.where(kb >= 0, spent_ref[h, rows, :], -NEG_INF)

    acc_ref[...] = jnp.zeros_like(acc_ref)
    spent_ref[...] = jnp.zeros_like(spent_ref)
    for first in range(chains - 1, -SB_STATIC_DEPTH, -1):
        lo, hi = max(first, 0), min(first + SB_STATIC_DEPTH - 1, chains - 1)
        rows = slice(lo * BLOCK, (hi + 1) * BLOCK)
        kb = base + first
        for h in range(heads):
            if first >= 0:
                key_block(h, rows, kb, True)
            else:
                retire_if_no_keys(h, rows, kb)
                key_block(h, rows, jnp.maximum(kb, 0), False)

    def cond(carry):
        _, least_spent = carry
        return least_spent < -SB_EXIT_LOG

    def body(carry):
        delta, _ = carry
        for c in range(chains):
            rows = slice(c * BLOCK, (c + 1) * BLOCK)
            kb = base + c - delta
            for h in range(heads):
                retire_if_no_keys(h, rows, kb)
                key_block(h, rows, jnp.maximum(kb, 0), False)
        return delta + 1, jnp.min(spent_ref[...])

    lax.while_loop(cond, body, (jnp.int32(SB_STATIC_DEPTH), jnp.min(spent_ref[...])))
    for h in range(heads):
        o_ref[:, h * HEAD_DIM:(h + 1) * HEAD_DIM] = acc_ref[h].astype(o_ref.dtype)


def _stick_breaking(qkv, B, S, tq, heads):
    H = SB_HEADS
    nq = S // tq
    hb = H // heads
    return pl.pallas_call(
        functools.partial(_sb_kernel, chains=tq // BLOCK, heads=heads),
        grid=(B, hb, nq),
        in_specs=[
            pl.BlockSpec((heads, tq, HEAD_DIM), lambda b, h, i: (h, b * nq + i, 0)),
            pl.BlockSpec((heads, S, HEAD_DIM), lambda b, h, i: (hb + h, b, 0)),
            pl.BlockSpec((heads, S, HEAD_DIM), lambda b, h, i: (2 * hb + h, b, 0)),
        ],
        out_specs=pl.BlockSpec((tq, heads * HEAD_DIM), lambda b, h, i: (b * nq + i, h)),
        out_shape=jax.ShapeDtypeStruct((B * S, H * HEAD_DIM), BF16),
        scratch_shapes=[pltpu.VMEM((heads, tq, HEAD_DIM), F32), pltpu.VMEM((heads, tq, BLOCK), F32)],
        compiler_params=_params(("parallel", "parallel", "arbitrary")),
        name="stick_breaking",
    )(qkv, qkv, qkv)


def _residue_major_permutation():
    i = np.arange(DIL_TILE)
    p = np.zeros((DIL_TILE, DIL_TILE), np.float32)
    p[i, (i % DIL_CHUNK) * DIL_RES + i // DIL_CHUNK] = 1.0
    return p


def _norm_matmul_perm_kernel(x_ref, g_ref, p_ref, w_ref, o_ref, hn_even_ref, hn_odd_ref, hnp_ref,
                             *, steps, natural_tiles, chunk):
    i, j = pl.program_id(0), pl.program_id(1)

    def permute(hn_ref):
        for c in range(hn_ref.shape[1] // chunk):
            cols = slice(c * chunk, (c + 1) * chunk)
            hnp_ref[:, cols] = jnp.dot(p_ref[...], hn_ref[:, cols],
                                       preferred_element_type=F32).astype(hnp_ref.dtype)

    def project(lhs_ref):
        o_ref[...] = jnp.dot(lhs_ref[...], w_ref[...], preferred_element_type=F32).astype(o_ref.dtype)

    def row_step(filling_ref, finished_ref):
        @pl.when(j == 0)
        def _():
            permute(finished_ref)

        @pl.when(j < natural_tiles)
        def _():
            _norm_share(x_ref, g_ref, filling_ref, 0, j, steps)
            project(finished_ref)

        @pl.when(j >= natural_tiles)
        def _():
            _norm_share(x_ref, g_ref, filling_ref, 0, j, steps)
            project(hnp_ref)

    @pl.when(i == 0)
    def _():
        _norm_share(x_ref, g_ref, hn_even_ref, 0, j, steps)

    @pl.when(i % 2 == 1)
    def _():
        row_step(hn_odd_ref, hn_even_ref)

    @pl.when(jnp.logical_and(i % 2 == 0, i > 0))
    def _():
        row_step(hn_even_ref, hn_odd_ref)


def _norm_matmul_perm(x, g, w, perm, natural_cols, tn):
    M, K = x.shape
    N = w.shape[1]
    tm = DIL_TILE
    tiles, steps = M // tm, N // tn

    def col(i, j):
        return jnp.where(i > 0, j, 0)

    return pl.pallas_call(
        functools.partial(_norm_matmul_perm_kernel, steps=steps, natural_tiles=natural_cols // tn, chunk=512),
        grid=(tiles + 1, steps),
        in_specs=[
            pl.BlockSpec((tm, K), lambda i, j: (jnp.minimum(i, tiles - 1), 0)),
            pl.BlockSpec((1, K), lambda i, j: (0, 0)),
            pl.BlockSpec((tm, tm), lambda i, j: (0, 0), pipeline_mode=pl.Buffered(1)),
            pl.BlockSpec((K, tn), lambda i, j: (0, col(i, j))),
        ],
        out_specs=pl.BlockSpec((tm, tn), lambda i, j: (jnp.maximum(i - 1, 0), col(i, j))),
        out_shape=jax.ShapeDtypeStruct((M, N), BF16),
        scratch_shapes=[pltpu.VMEM((tm, K), BF16), pltpu.VMEM((tm, K), BF16), pltpu.VMEM((tm, K), BF16)],
        compiler_params=_params(("arbitrary", "arbitrary")),
        name="norm_matmul_perm",
    )(x, g.reshape(1, K), perm, w)


def _rel_bucket_table(dil, order):
    order = np.asarray(order)
    qi = order[:, None]
    kc = np.concatenate([order, BLOCK + order])[None, :]
    rel = BLOCK + qi - kc
    n = jnp.asarray(np.maximum(rel, 0) * dil)
    max_exact = REL_BUCKETS // 2
    nf = jnp.maximum(n, 1).astype(F32)
    large = max_exact + (jnp.log(nf / max_exact) / math.log(REL_MAX_DISTANCE / max_exact)
                         * (REL_BUCKETS - max_exact)).astype(jnp.int32)
    large = jnp.minimum(large, REL_BUCKETS - 1)
    bucket = jnp.where(n < max_exact, n, large).astype(jnp.int32)
    return jnp.where(jnp.asarray((rel >= 0) & (rel <= BLOCK)), bucket, -1)


class _NaturalRows:
    @staticmethod
    def load(ref, j, cols):
        return ref[j * BLOCK:(j + 1) * BLOCK, cols]

    @staticmethod
    def load_prev(ref, cols):
        return ref[:, cols]

    @staticmethod
    def store(ref, j, cols, val):
        ref[j * BLOCK:(j + 1) * BLOCK, cols] = val


class _Residue16Rows:
    per = BLOCK // DIL_CHUNK

    @classmethod
    def load(cls, ref, j, cols):
        return jnp.concatenate([ref[cls.per * j + t, :, cols] for t in range(cls.per)], axis=0)

    @classmethod
    def load_prev(cls, ref, cols):
        return jnp.concatenate([ref[t, :, cols] for t in range(cls.per)], axis=0)

    @classmethod
    def store(cls, ref, j, cols, val):
        for t in range(cls.per):
            ref[cls.per * j + t, :, cols] = val[t * DIL_CHUNK:(t + 1) * DIL_CHUNK]


class _Residue4Rows:
    sub = BLOCK // 4
    per_tile = DIL_CHUNK // sub
    order = [4 * m + a for a in range(4) for m in range(BLOCK // 4)]

    @classmethod
    def _rows(cls, j):
        return j // cls.per_tile, slice((j % cls.per_tile) * cls.sub, (j % cls.per_tile + 1) * cls.sub)

    @classmethod
    def load(cls, ref, j, cols):
        t, ms = cls._rows(j)
        return jnp.concatenate([ref[t, a, ms, cols] for a in range(4)], axis=0)

    @classmethod
    def load_prev(cls, ref, cols):
        return cls.load(ref, cls.per_tile - 1, cols)

    @classmethod
    def store(cls, ref, j, cols, val):
        t, ms = cls._rows(j)
        for a in range(4):
            ref[t, a, ms, cols] = val[a * cls.sub:(a + 1) * cls.sub]


def _dil_kernel(bucket_ref, table_ref, q_ref, kc_ref, kp_ref, vc_ref, vp_ref, o_ref, lse_ref, bias_ref,
                *, sub_blocks, rows):
    scale = HEAD_DIM ** -0.5
    first_step = jnp.logical_and(jnp.logical_and(pl.program_id(0) == 0, pl.program_id(1) == 0),
                                 pl.program_id(2) == 0)

    @pl.when(first_step)
    def _():
        bucket = bucket_ref[...]
        for h in range(DIL_GROUP_HEADS):
            b = jnp.full(bucket.shape, NEG_INF, F32)
            for t in range(REL_BUCKETS):
                b = jnp.where(bucket == t, table_ref[t, h], b)
            bias_ref[h] = b

    lane = lax.broadcasted_iota(jnp.int32, (BLOCK, 128), 1)
    keep_keys = jnp.logical_or(lax.broadcasted_iota(jnp.int32, (BLOCK, 2 * BLOCK), 1) >= BLOCK,
                               pl.program_id(2) > 0)
    ones = jnp.ones((2 * BLOCK, HEAD_DIM), BF16)
    contract_last = (((1,), (1,)), ((), ()))
    all_lanes = slice(None)

    def prev_and_cur(cur_ref, prev_ref, j, cols):
        prev = rows.load_prev(prev_ref, cols) if j == 0 else rows.load(cur_ref, j - 1, cols)
        return jnp.concatenate([prev, rows.load(cur_ref, j, cols)], axis=0)

    for j in range(sub_blocks):
        stats = jnp.zeros((BLOCK, 128), F32)
        for h in range(DIL_GROUP_HEADS):
            cols = slice(h * HEAD_DIM, (h + 1) * HEAD_DIM)
            q = rows.load(q_ref, j, cols)
            k = prev_and_cur(kc_ref, kp_ref, j, cols)
            v = prev_and_cur(vc_ref, vp_ref, j, cols)
            s = lax.dot_general(q, k, contract_last, preferred_element_type=F32) * scale + bias_ref[h]
            if j == 0:
                s = jnp.where(keep_keys, s, NEG_INF)
            m = jnp.max(s, axis=-1, keepdims=True)
            p = jnp.exp(s - m).astype(BF16)
            ol = jnp.dot(p, jnp.concatenate([v, ones], axis=1), preferred_element_type=F32)
            o, l = ol[:, :HEAD_DIM], ol[:, HEAD_DIM:]
            rows.store(o_ref, j, cols, (o / l).astype(o_ref.dtype))
            stats = jnp.where(lane == h, m + jnp.log(l), stats)
        rows.store(lse_ref, j, all_lanes, stats)


def _dilated_group(qkv, rel_table, B, S, g, dil):
    W, C = DIL_WIDTH, qkv.shape[1]
    tiles = S // DIL_TILE
    col0 = g * 3
    if dil == 1:
        rows, n_res = _NaturalRows, 1
        blocks = min(4, S // BLOCK)
        steps = S // (blocks * BLOCK)
        views = [(B * S, C), (B * S, W), (B * S, 128)]
        cur_shape, prev_shape = (blocks * BLOCK,), (BLOCK,)
        cur_idx = lambda b, r, i: (b * steps + i,)
        prev_idx = lambda b, r, i: (b * (S // BLOCK) + jnp.maximum(i * blocks - 1, 0),)
    elif dil == DIL_RES:
        rows, n_res = _Residue16Rows, DIL_RES
        per = _Residue16Rows.per
        t_step = min(4 * per, tiles)
        blocks, steps = t_step // per, tiles // t_step
        views = [(B * tiles, DIL_RES, DIL_CHUNK, c) for c in (C, W, 128)]
        cur_shape, prev_shape = (t_step, None, DIL_CHUNK), (per, None, DIL_CHUNK)
        cur_idx = lambda b, r, i: (b * steps + i, r, 0)
        prev_idx = lambda b, r, i: (b * (tiles // per) + jnp.maximum(i * blocks - 1, 0), r, 0)
    else:
        assert dil == 4 and DIL_RES == 16
        rows, n_res = _Residue4Rows, 4
        per_tile = _Residue4Rows.per_tile
        t_step = min(2, tiles)
        blocks, steps = t_step * per_tile, tiles // t_step
        views = [(B * tiles, 4, 4, DIL_CHUNK, c) for c in (C, W, 128)]
        cur_shape, prev_shape = (t_step, 4, None, DIL_CHUNK), (1, 4, None, DIL_CHUNK)
        cur_idx = lambda b, r, i: (b * steps + i, 0, r, 0)
        prev_idx = lambda b, r, i: (b * tiles + jnp.maximum(i * t_step - 1, 0), 0, r, 0)

    def spec(shape, idx, width, col):
        return pl.BlockSpec(shape + (width,), lambda b, r, i: idx(b, r, i) + (col,))

    qkv_v = qkv.reshape(views[0])
    order = getattr(rows, "order", list(range(BLOCK)))
    o, lse = pl.pallas_call(
        functools.partial(_dil_kernel, sub_blocks=blocks, rows=rows),
        grid=(B, n_res, steps),
        in_specs=[
            pl.BlockSpec((BLOCK, 2 * BLOCK), lambda b, r, i: (0, 0)),
            pl.BlockSpec(memory_space=pltpu.SMEM),
            spec(cur_shape, cur_idx, W, col0),
            spec(cur_shape, cur_idx, W, col0 + 1), spec(prev_shape, prev_idx, W, col0 + 1),
            spec(cur_shape, cur_idx, W, col0 + 2), spec(prev_shape, prev_idx, W, col0 + 2),
        ],
        out_specs=[spec(cur_shape, cur_idx, W, 0), spec(cur_shape, cur_idx, 128, 0)],
        out_shape=[jax.ShapeDtypeStruct(views[1], BF16), jax.ShapeDtypeStruct(views[2], F32)],
        scratch_shapes=[pltpu.VMEM((DIL_GROUP_HEADS, BLOCK, 2 * BLOCK), F32)],
        compiler_params=_params(("arbitrary", "arbitrary", "arbitrary")),
        name=f"dilated_group{g}",
    )(_rel_bucket_table(dil, order), rel_table, qkv_v, qkv_v, qkv_v, qkv_v, qkv_v)
    return o.reshape(B * S, W), lse.reshape(B * S, 128)


def _bf16_pieces(x):
    hi = x.astype(BF16)
    rest = x - hi.astype(F32)
    mid = rest.astype(BF16)
    return [hi, mid, (rest - mid.astype(F32)).astype(BF16)]


def _combine_matmul_res_kernel(o0_ref, o1_ref, o2_ref, l0_ref, l1_ref, l2_ref, pt_ref, w_ref, r_ref, out_ref,
                               comb_ref):
    rows = out_ref.shape[0]
    pt = pt_ref[pl.ds(pl.multiple_of(pl.program_id(1) * rows, rows), rows), :]
    pieces = jnp.concatenate(_bf16_pieces(l1_ref[...]) + _bf16_pieces(l2_ref[...]), axis=1)
    lse = jnp.dot(pt, pieces, preferred_element_type=F32)
    l0 = l0_ref[...]
    l1 = lse[:, 0:128] + lse[:, 128:256] + lse[:, 256:384]
    l2 = lse[:, 384:512] + lse[:, 512:640] + lse[:, 640:768]
    mx = jnp.maximum(jnp.maximum(l0, l1), l2)
    e0, e1, e2 = jnp.exp(l0 - mx), jnp.exp(l1 - mx), jnp.exp(l2 - mx)
    den = e0 + e1 + e2
    w0, w1, w2 = e0 / den, e1 / den, e2 / den
    pair = 2 * HEAD_DIM
    for hp in range(DIL_GROUP_HEADS // 2):
        pcols = slice(hp * pair, (hp + 1) * pair)
        o1 = jnp.dot(pt, o1_ref[:, pcols], preferred_element_type=F32)
        o2 = jnp.dot(pt, o2_ref[:, pcols], preferred_element_type=F32)
        for k in range(2):
            h = 2 * hp + k
            cols = slice(h * HEAD_DIM, (h + 1) * HEAD_DIM)
            sub = slice(k * HEAD_DIM, (k + 1) * HEAD_DIM)
            c = (w0[:, h:h + 1] * o0_ref[:, cols].astype(F32)
                 + w1[:, h:h + 1] * o1[:, sub] + w2[:, h:h + 1] * o2[:, sub])
            comb_ref[:, cols] = c.astype(comb_ref.dtype)
    out_ref[...] = r_ref[...] + jnp.dot(comb_ref[...], w_ref[...], preferred_element_type=F32)


def _combine_matmul_res(os, lses, unperm, w, res, rows):
    M, K = os[0].shape
    N = w.shape[1]
    parts = DIL_TILE // rows
    own_o = pl.BlockSpec((rows, K), lambda i, p: (i * parts + p, 0))
    own_l = pl.BlockSpec((rows, 128), lambda i, p: (i * parts + p, 0))
    tile_o = pl.BlockSpec((DIL_TILE, K), lambda i, p: (i, 0))
    tile_l = pl.BlockSpec((DIL_TILE, 128), lambda i, p: (i, 0))
    row_block = pl.BlockSpec((rows, N), lambda i, p: (i * parts + p, 0))
    return pl.pallas_call(
        _combine_matmul_res_kernel,
        grid=(M // DIL_TILE, parts),
        in_specs=[own_o, tile_o, tile_o, own_l, tile_l, tile_l,
                  pl.BlockSpec((DIL_TILE, DIL_TILE), lambda i, p: (0, 0), pipeline_mode=pl.Buffered(1)),
                  pl.BlockSpec((K, N), lambda i, p: (0, 0), pipeline_mode=pl.Buffered(1)),
                  row_block],
        out_specs=row_block,
        out_shape=jax.ShapeDtypeStruct((M, N), F32),
        scratch_shapes=[pltpu.VMEM((rows, K), BF16)],
        compiler_params=_params(("parallel", "parallel")),
        name="combine_matmul_res",
    )(*os, *lses, unperm, w, res)


def _ffn_up_kernel(x_ref, g_ref, wg_ref, wv_ref, cw_ref, cb_ref, o_ref, hn_even_ref, hn_odd_ref,
                   *, steps, tiles_per_seq):
    i, j = pl.program_id(0), pl.program_id(1)
    tm = x_ref.shape[0]

    def gated(hn_ref):
        gate = jnp.dot(hn_ref[...], wg_ref[...], preferred_element_type=F32)
        val = jnp.dot(hn_ref[CONV_HALO:, :], wv_ref[...], preferred_element_type=F32)
        conv = cb_ref[...] + gate[CONV_HALO:] * cw_ref[CONV_WIDTH - 1:CONV_WIDTH, :]
        for back in range(1, CONV_WIDTH):
            tap = CONV_WIDTH - 1 - back
            conv = conv + pltpu.roll(gate, back, axis=0)[CONV_HALO:] * cw_ref[tap:tap + 1, :]
        act = conv / (1.0 + jnp.exp(-conv)) * val
        o_ref[...] = act.astype(o_ref.dtype)

    def row_step(filling_ref, finished_ref):
        tail = finished_ref[tm:tm + CONV_HALO, :]
        filling_ref[0:CONV_HALO, :] = jnp.where(i % tiles_per_seq == 0, jnp.zeros_like(tail), tail)
        _norm_share(x_ref, g_ref, filling_ref, CONV_HALO, j, steps)
        gated(finished_ref)

    @pl.when(i == 0)
    def _():
        hn_even_ref[0:CONV_HALO, :] = jnp.zeros((CONV_HALO, hn_even_ref.shape[1]), hn_even_ref.dtype)
        _norm_share(x_ref, g_ref, hn_even_ref, CONV_HALO, j, steps)

    @pl.when(i % 2 == 1)
    def _():
        row_step(hn_odd_ref, hn_even_ref)

    @pl.when(jnp.logical_and(i % 2 == 0, i > 0))
    def _():
        row_step(hn_even_ref, hn_odd_ref)


def _ffn_up(x, g, w_up, conv_w, conv_b, S, tm, tf):
    M, K = x.shape
    d_ff = w_up.shape[1] // 2
    tiles, steps = M // tm, d_ff // tf

    def col(i, j):
        return jnp.where(i > 0, j, 0)

    return pl.pallas_call(
        functools.partial(_ffn_up_kernel, steps=steps, tiles_per_seq=S // tm),
        grid=(tiles + 1, steps),
        in_specs=[
            pl.BlockSpec((tm, K), lambda i, j: (jnp.minimum(i, tiles - 1), 0)),
            pl.BlockSpec((1, K), lambda i, j: (0, 0)),
            pl.BlockSpec((K, tf), lambda i, j: (0, col(i, j))),
            pl.BlockSpec((K, tf), lambda i, j: (0, steps + col(i, j))),
            pl.BlockSpec((CONV_WIDTH, tf), lambda i, j: (0, col(i, j))),
            pl.BlockSpec((1, tf), lambda i, j: (0, col(i, j))),
        ],
        out_specs=pl.BlockSpec((tm, tf), lambda i, j: (jnp.maximum(i - 1, 0), col(i, j))),
        out_shape=jax.ShapeDtypeStruct((M, d_ff), BF16),
        scratch_shapes=[pltpu.VMEM((tm + CONV_HALO, K), BF16), pltpu.VMEM((tm + CONV_HALO, K), BF16)],
        compiler_params=_params(("arbitrary", "arbitrary")),
        name="ffn_up",
    )(x, g.reshape(1, K), w_up, w_up, conv_w, conv_b.reshape(1, d_ff))


def kernel(x, ln_mix, ln_ffn, ln_f, w_qkv_sb, w_o_sb, w_qkv_dil, w_o_dil, rel_bias, w_up, conv_w, conv_b, w_down):
    B, S, D = x.shape
    depth = ln_mix.shape[0]
    h = x.reshape(B * S, D)
    for i in range(depth):
        j = i // N_MIXERS
        if i % N_MIXERS == 0:
            qkv = _norm_matmul_heads(h, ln_mix[i], w_qkv_sb[j].astype(BF16), tm=1024, tn=1024)
            o = _stick_breaking(qkv, B, S, tq=min(2048, S), heads=2)
            h = _matmul_res(o, w_o_sb[j].astype(BF16), h, tm=512)
        else:
            perm = _residue_major_permutation()
            qkv = _norm_matmul_perm(h, ln_mix[i], w_qkv_dil[j].astype(BF16), jnp.asarray(perm, BF16),
                                    natural_cols=3 * DIL_WIDTH, tn=1024)
            os, lses = [], []
            for g, (window, dil) in enumerate(DIL_PATTERNS):
                assert window // dil == BLOCK and S % (dil * BLOCK) == 0
                table = rel_bias[:, g * DIL_GROUP_HEADS:(g + 1) * DIL_GROUP_HEADS]
                o, lse = _dilated_group(qkv, table, B, S, g, dil)
                os.append(o)
                lses.append(lse)
            h = _combine_matmul_res(os, lses, jnp.asarray(perm.T, BF16), w_o_dil[j].astype(BF16), h, rows=512)
        act = _ffn_up(h, ln_ffn[i], w_up[i].astype(BF16), conv_w[i], conv_b[i], S, tm=1024, tf=512)
        h = _matmul_res(act, w_down[i].astype(BF16), h, tm=512, norm_gain=ln_f if i == depth - 1 else None)
    return h.reshape(B, S, D)
```

```python
import functools
import math

import jax
import jax.numpy as jnp
import numpy as np
from jax import lax
from jax.experimental import pallas as pl
from jax.experimental.pallas import tpu as pltpu

BLOCK = 128
SB_HEADS = 16
HEAD_DIM = 128
DIL_PATTERNS = ((128, 1), (512, 4), (2048, 16))
DIL_GROUPS = len(DIL_PATTERNS)
DIL_GROUP_HEADS = 8
DIL_WIDTH = DIL_GROUP_HEADS * HEAD_DIM
REL_BUCKETS = 32
REL_MAX_DISTANCE = 2048
CONV_WIDTH = 3
RMS_EPS = 1e-6
NEG_INF = -1e30
LOG2E = math.log2(math.e)
N_MIXERS = 2

SB_EXIT_LOG = -104.0
SB_STATIC_DEPTH = 3

DIL_TILE = 1024
DIL_RES = 16
DIL_CHUNK = DIL_TILE // DIL_RES
CONV_HALO = 16
NORM_CHUNK = 64
V7X_VMEM_BYTES = 64 * 1024 * 1024
VMEM_LIMIT = V7X_VMEM_BYTES * 7 // 8

BF16 = jnp.bfloat16
F32 = jnp.float32


def _params(semantics):
    return pltpu.CompilerParams(dimension_semantics=semantics, vmem_limit_bytes=VMEM_LIMIT)


def _rms_rows(x, g):
    ms = jnp.mean(x * x, axis=-1, keepdims=True)
    return x * lax.rsqrt(ms + RMS_EPS) * g


def _norm_into(x_ref, g_ref, dst_ref, dst_row0, chunk):
    g = g_ref[...]

    def body(c, carry):
        r = pl.multiple_of(c * chunk, chunk)
        y = _rms_rows(x_ref[pl.ds(r, chunk), :], g)
        dst_ref[pl.ds(dst_row0 + r, chunk), :] = y.astype(dst_ref.dtype)
        return carry

    lax.fori_loop(0, x_ref.shape[0] // chunk, body, 0)


def _norm_share(x_ref, g_ref, dst_ref, dst_row0, step):
    rows = x_ref.shape[0]
    shares = (dst_ref.shape[0] - dst_row0) // rows
    base = dst_row0 + jnp.minimum(step, shares - 1) * rows
    g = g_ref[...]
    for c in range(rows // NORM_CHUNK):
        y = _rms_rows(x_ref[c * NORM_CHUNK:(c + 1) * NORM_CHUNK, :], g)
        dst_ref[pl.ds(pl.multiple_of(base + c * NORM_CHUNK, NORM_CHUNK // 4), NORM_CHUNK), :] = y.astype(dst_ref.dtype)


def _row_shares(steps):
    return 1 << (steps.bit_length() - 1)


def _share_spec(tm, K, tiles, steps):
    shares = _row_shares(steps)
    return pl.BlockSpec((tm // shares, K),
                        lambda i, j: (jnp.minimum(i, tiles - 1) * shares + jnp.minimum(j, shares - 1), 0))


def _norm_matmul_kernel(x_ref, g_ref, w_ref, o_ref, hn_even_ref, hn_odd_ref):
    i, j = pl.program_id(0), pl.program_id(1)

    def project(hn_ref):
        y = jnp.dot(hn_ref[...], w_ref[...], preferred_element_type=F32).astype(o_ref.dtype)
        for c in range(o_ref.shape[0]):
            o_ref[c] = y[:, c * HEAD_DIM:(c + 1) * HEAD_DIM]

    @pl.when(i == 0)
    def _():
        _norm_share(x_ref, g_ref, hn_even_ref, 0, j)

    @pl.when(i % 2 == 1)
    def _():
        _norm_share(x_ref, g_ref, hn_odd_ref, 0, j)
        project(hn_even_ref)

    @pl.when(jnp.logical_and(i % 2 == 0, i > 0))
    def _():
        _norm_share(x_ref, g_ref, hn_even_ref, 0, j)
        project(hn_odd_ref)


def _norm_matmul_heads(x, g, w, tm, tn):
    M, K = x.shape
    N = w.shape[1]
    slabs = tn // HEAD_DIM
    tiles, steps = M // tm, N // tn

    def col(i, j):
        return jnp.where(i > 0, j, 0)

    return pl.pallas_call(
        _norm_matmul_kernel,
        grid=(tiles + 1, steps),
        in_specs=[
            _share_spec(tm, K, tiles, steps),
            pl.BlockSpec((1, K), lambda i, j: (0, 0)),
            pl.BlockSpec((K, tn), lambda i, j: (0, col(i, j))),
        ],
        out_specs=pl.BlockSpec((slabs, tm, HEAD_DIM), lambda i, j: (col(i, j), jnp.maximum(i - 1, 0), 0)),
        out_shape=jax.ShapeDtypeStruct((N // HEAD_DIM, M, HEAD_DIM), BF16),
        scratch_shapes=[pltpu.VMEM((tm, K), BF16), pltpu.VMEM((tm, K), BF16)],
        compiler_params=_params(("arbitrary", "arbitrary")),
        name="norm_matmul",
    )(x, g.reshape(1, K), w)


def _matmul_res_kernel(a_ref, w_ref, r_ref, o_ref):
    o_ref[...] = r_ref[...] + jnp.dot(a_ref[...], w_ref[...], preferred_element_type=F32)


def _matmul_res_norm_kernel(a_ref, w_ref, r_ref, g_ref, o_ref):
    y = r_ref[...] + jnp.dot(a_ref[...], w_ref[...], preferred_element_type=F32)
    o_ref[...] = _rms_rows(y, g_ref[...])


def _matmul_res(a, w, res, tm, norm_gain=None):
    M, K = a.shape
    N = w.shape[1]
    in_specs = [
        pl.BlockSpec((tm, K), lambda i: (i, 0)),
        pl.BlockSpec((K, N), lambda i: (0, 0), pipeline_mode=pl.Buffered(1)),
        pl.BlockSpec((tm, N), lambda i: (i, 0)),
    ]
    args = [a, w, res]
    if norm_gain is not None:
        in_specs.append(pl.BlockSpec((1, N), lambda i: (0, 0)))
        args.append(norm_gain.reshape(1, N))
    return pl.pallas_call(
        _matmul_res_kernel if norm_gain is None else _matmul_res_norm_kernel,
        grid=(M // tm,),
        in_specs=in_specs,
        out_specs=pl.BlockSpec((tm, N), lambda i: (i, 0)),
        out_shape=jax.ShapeDtypeStruct((M, N), F32),
        compiler_params=_params(("parallel",)),
        name="matmul_res" if norm_gain is None else "matmul_res_norm",
    )(*args)


def _sb_kernel(q_ref, k_ref, v_ref, o_ref, acc_ref, spent_ref, *, chains, heads):
    scale = HEAD_DIM ** -0.5

    row = lax.broadcasted_iota(jnp.int32, (BLOCK, BLOCK), 0)
    col = lax.broadcasted_iota(jnp.int32, (BLOCK, BLOCK), 1)
    causal = col < row
    suffix = (row > col).astype(BF16)
    base = pl.program_id(2) * chains
    contract_last = (((1,), (1,)), ((), ()))

    def mask_first_block(x):
        head = jnp.where(causal, x[:BLOCK], 0.0)
        return head if x.shape[0] == BLOCK else jnp.concatenate([head, x[BLOCK:]], axis=0)

    def key_block(h, rows, kb, on_diagonal):
        start = pl.multiple_of(kb * BLOCK, BLOCK)
        k = k_ref[h, pl.ds(start, BLOCK), :]
        v = v_ref[h, pl.ds(start, BLOCK), :]
        z = lax.dot_general(q_ref[h, rows, :], k, contract_last, preferred_element_type=F32) * scale
        drop = jnp.maximum(z, 0.0) + jnp.log(1.0 + jnp.exp2(jnp.abs(z) * -LOG2E))
        log_beta = z - drop
        if on_diagonal:
            drop = mask_first_block(drop)
        drop = drop.astype(BF16)
        within = jnp.dot(drop, suffix, preferred_element_type=F32)
        a = jnp.exp(log_beta - (within + spent_ref[h, rows, :]))
        if on_diagonal:
            a = mask_first_block(a)
        acc_ref[h, rows, :] += jnp.dot(a.astype(BF16), v, preferred_element_type=F32)
        spent_ref[h, rows, :] += within[:, 0:1] + drop[:, 0:1].astype(F32)

    def retire_if_no_keys(h, rows, kb):
        spent_ref[h, rows, :] = jnp.where(kb >= 0, spent_ref[h, rows, :], -NEG_INF)

    acc_ref[...] = jnp.zeros_like(acc_ref)
    spent_ref[...] = jnp.zeros_like(spent_ref)
    for first in range(chains - 1, -SB_STATIC_DEPTH, -1):
        lo, hi = max(first, 0), min(first + SB_STATIC_DEPTH - 1, chains - 1)
        rows = slice(lo * BLOCK, (hi + 1) * BLOCK)
        kb = base + first
        for h in range(heads):
            if first >= 0:
                key_block(h, rows, kb, True)
            else:
                retire_if_no_keys(h, rows, kb)
                key_block(h, rows, jnp.maximum(kb, 0), False)

    def cond(carry):
        _, least_spent = carry
        return least_spent < -SB_EXIT_LOG

    def body(carry):
        delta, _ = carry
        for c in range(chains):
            rows = slice(c * BLOCK, (c + 1) * BLOCK)
            kb = base + c - delta
            for h in range(heads):
                retire_if_no_keys(h, rows, kb)
                key_block(h, rows, jnp.maximum(kb, 0), False)
        return delta + 1, jnp.min(spent_ref[...])

    lax.while_loop(cond, body, (jnp.int32(SB_STATIC_DEPTH), jnp.min(spent_ref[...])))
    for h in range(heads):
        o_ref[:, h * HEAD_DIM:(h + 1) * HEAD_DIM] = acc_ref[h].astype(o_ref.dtype)


def _stick_breaking(qkv, B, S, tq, heads):
    H = SB_HEADS
    nq = S // tq
    hb = H // heads
    return pl.pallas_call(
        functools.partial(_sb_kernel, chains=tq // BLOCK, heads=heads),
        grid=(B, hb, nq),
        in_specs=[
            pl.BlockSpec((heads, tq, HEAD_DIM), lambda b, h, i: (h, b * nq + i, 0)),
            pl.BlockSpec((heads, S, HEAD_DIM), lambda b, h, i: (hb + h, b, 0)),
            pl.BlockSpec((heads, S, HEAD_DIM), lambda b, h, i: (2 * hb + h, b, 0)),
        ],
        out_specs=pl.BlockSpec((tq, heads * HEAD_DIM), lambda b, h, i: (b * nq + i, h)),
        out_shape=jax.ShapeDtypeStruct((B * S, H * HEAD_DIM), BF16),
        scratch_shapes=[pltpu.VMEM((heads, tq, HEAD_DIM), F32), pltpu.VMEM((heads, tq, BLOCK), F32)],
        compiler_params=_params(("parallel", "parallel", "arbitrary")),
        name="stick_breaking",
    )(qkv, qkv, qkv)


def _residue_major_permutation():
    i = np.arange(DIL_TILE)
    p = np.zeros((DIL_TILE, DIL_TILE), np.float32)
    p[i, (i % DIL_CHUNK) * DIL_RES + i // DIL_CHUNK] = 1.0
    return p


def _norm_matmul_perm_kernel(x_ref, g_ref, p_ref, w_ref, o_ref, hn_even_ref, hn_odd_ref, hnp_ref,
                             *, natural_tiles, chunk):
    i, j = pl.program_id(0), pl.program_id(1)

    def permute(hn_ref):
        for c in range(hn_ref.shape[1] // chunk):
            cols = slice(c * chunk, (c + 1) * chunk)
            hnp_ref[:, cols] = jnp.dot(p_ref[...], hn_ref[:, cols],
                                       preferred_element_type=F32).astype(hnp_ref.dtype)

    def project(lhs_ref):
        o_ref[...] = jnp.dot(lhs_ref[...], w_ref[...], preferred_element_type=F32).astype(o_ref.dtype)

    def row_step(filling_ref, finished_ref):
        @pl.when(j == 0)
        def _():
            permute(finished_ref)

        @pl.when(j < natural_tiles)
        def _():
            _norm_share(x_ref, g_ref, filling_ref, 0, j)
            project(finished_ref)

        @pl.when(j >= natural_tiles)
        def _():
            _norm_share(x_ref, g_ref, filling_ref, 0, j)
            project(hnp_ref)

    @pl.when(i == 0)
    def _():
        _norm_share(x_ref, g_ref, hn_even_ref, 0, j)

    @pl.when(i % 2 == 1)
    def _():
        row_step(hn_odd_ref, hn_even_ref)

    @pl.when(jnp.logical_and(i % 2 == 0, i > 0))
    def _():
        row_step(hn_even_ref, hn_odd_ref)


def _norm_matmul_perm(x, g, w, perm, natural_cols, tn):
    M, K = x.shape
    N = w.shape[1]
    tm = DIL_TILE
    tiles, steps = M // tm, N // tn

    def col(i, j):
        return jnp.where(i > 0, j, 0)

    return pl.pallas_call(
        functools.partial(_norm_matmul_perm_kernel, natural_tiles=natural_cols // tn, chunk=512),
        grid=(tiles + 1, steps),
        in_specs=[
            _share_spec(tm, K, tiles, steps),
            pl.BlockSpec((1, K), lambda i, j: (0, 0)),
            pl.BlockSpec((tm, tm), lambda i, j: (0, 0), pipeline_mode=pl.Buffered(1)),
            pl.BlockSpec((K, tn), lambda i, j: (0, col(i, j))),
        ],
        out_specs=pl.BlockSpec((tm, tn), lambda i, j: (jnp.maximum(i - 1, 0), col(i, j))),
        out_shape=jax.ShapeDtypeStruct((M, N), BF16),
        scratch_shapes=[pltpu.VMEM((tm, K), BF16), pltpu.VMEM((tm, K), BF16), pltpu.VMEM((tm, K), BF16)],
        compiler_params=_params(("arbitrary", "arbitrary")),
        name="norm_matmul_perm",
    )(x, g.reshape(1, K), perm, w)


def _rel_bucket_table(dil, order):
    order = np.asarray(order)
    qi = order[:, None]
    kc = np.concatenate([order, BLOCK + order])[None, :]
    rel = BLOCK + qi - kc
    n = jnp.asarray(np.maximum(rel, 0) * dil)
    max_exact = REL_BUCKETS // 2
    nf = jnp.maximum(n, 1).astype(F32)
    large = max_exact + (jnp.log(nf / max_exact) / math.log(REL_MAX_DISTANCE / max_exact)
                         * (REL_BUCKETS - max_exact)).astype(jnp.int32)
    large = jnp.minimum(large, REL_BUCKETS - 1)
    bucket = jnp.where(n < max_exact, n, large).astype(jnp.int32)
    return jnp.where(jnp.asarray((rel >= 0) & (rel <= BLOCK)), bucket, -1)


class _NaturalRows:
    @staticmethod
    def load(ref, j, cols):
        return ref[j * BLOCK:(j + 1) * BLOCK, cols]

    @staticmethod
    def load_prev(ref, cols):
        return ref[:, cols]

    @staticmethod
    def store(ref, j, cols, val):
        ref[j * BLOCK:(j + 1) * BLOCK, cols] = val


class _Residue16Rows:
    per = BLOCK // DIL_CHUNK

    @classmethod
    def load(cls, ref, j, cols):
        return jnp.concatenate([ref[cls.per * j + t, :, cols] for t in range(cls.per)], axis=0)

    @classmethod
    def load_prev(cls, ref, cols):
        return jnp.concatenate([ref[t, :, cols] for t in range(cls.per)], axis=0)

    @classmethod
    def store(cls, ref, j, cols, val):
        for t in range(cls.per):
            ref[cls.per * j + t, :, cols] = val[t * DIL_CHUNK:(t + 1) * DIL_CHUNK]


class _Residue4Rows:
    sub = BLOCK // 4
    per_tile = DIL_CHUNK // sub
    order = [4 * m + a for a in range(4) for m in range(BLOCK // 4)]

    @classmethod
    def _rows(cls, j):
        return j // cls.per_tile, slice((j % cls.per_tile) * cls.sub, (j % cls.per_tile + 1) * cls.sub)

    @classmethod
    def load(cls, ref, j, cols):
        t, ms = cls._rows(j)
        return jnp.concatenate([ref[t, a, ms, cols] for a in range(4)], axis=0)

    @classmethod
    def load_prev(cls, ref, cols):
        return cls.load(ref, cls.per_tile - 1, cols)

    @classmethod
    def store(cls, ref, j, cols, val):
        t, ms = cls._rows(j)
        for a in range(4):
            ref[t, a, ms, cols] = val[a * cls.sub:(a + 1) * cls.sub]


def _dil_kernel(bucket_ref, table_ref, q_ref, kc_ref, kp_ref, vc_ref, vp_ref, o_ref, lse_ref, bias_ref,
                *, sub_blocks, rows):
    scale = HEAD_DIM ** -0.5
    first_step = jnp.logical_and(jnp.logical_and(pl.program_id(0) == 0, pl.program_id(1) == 0),
                                 pl.program_id(2) == 0)

    @pl.when(first_step)
    def _():
        bucket = bucket_ref[...]
        for h in range(DIL_GROUP_HEADS):
            b = jnp.full(bucket.shape, NEG_INF, F32)
            for t in range(REL_BUCKETS):
                b = jnp.where(bucket == t, table_ref[t, h], b)
            bias_ref[h] = b

    lane = lax.broadcasted_iota(jnp.int32, (BLOCK, 128), 1)
    keep_keys = jnp.logical_or(lax.broadcasted_iota(jnp.int32, (BLOCK, 2 * BLOCK), 1) >= BLOCK,
                               pl.program_id(2) > 0)
    ones = jnp.ones((2 * BLOCK, HEAD_DIM), BF16)
    contract_last = (((1,), (1,)), ((), ()))
    all_lanes = slice(None)

    def prev_and_cur(cur_ref, prev_ref, j, cols):
        prev = rows.load_prev(prev_ref, cols) if j == 0 else rows.load(cur_ref, j - 1, cols)
        return jnp.concatenate([prev, rows.load(cur_ref, j, cols)], axis=0)

    for j in range(sub_blocks):
        stats = jnp.zeros((BLOCK, 128), F32)
        for h in range(DIL_GROUP_HEADS):
            cols = slice(h * HEAD_DIM, (h + 1) * HEAD_DIM)
            q = rows.load(q_ref, j, cols)
            k = prev_and_cur(kc_ref, kp_ref, j, cols)
            v = prev_and_cur(vc_ref, vp_ref, j, cols)
            s = lax.dot_general(q, k, contract_last, preferred_element_type=F32) * scale + bias_ref[h]
            if j == 0:
                s = jnp.where(keep_keys, s, NEG_INF)
            m = jnp.max(s, axis=-1, keepdims=True)
            p = jnp.exp(s - m).astype(BF16)
            ol = jnp.dot(p, jnp.concatenate([v, ones], axis=1), preferred_element_type=F32)
            o, l = ol[:, :HEAD_DIM], ol[:, HEAD_DIM:]
            rows.store(o_ref, j, cols, (o / l).astype(o_ref.dtype))
            stats = jnp.where(lane == h, m + jnp.log(l), stats)
        rows.store(lse_ref, j, all_lanes, stats)


def _dilated_group(qkv, rel_table, B, S, g, dil):
    W, C = DIL_WIDTH, qkv.shape[1]
    tiles = S // DIL_TILE
    col0 = g * 3
    if dil == 1:
        rows, n_res = _NaturalRows, 1
        blocks = min(4, S // BLOCK)
        steps = S // (blocks * BLOCK)
        views = [(B * S, C), (B * S, W), (B * S, 128)]
        cur_shape, prev_shape = (blocks * BLOCK,), (BLOCK,)
        cur_idx = lambda b, r, i: (b * steps + i,)
        prev_idx = lambda b, r, i: (b * (S // BLOCK) + jnp.maximum(i * blocks - 1, 0),)
    elif dil == DIL_RES:
        rows, n_res = _Residue16Rows, DIL_RES
        per = _Residue16Rows.per
        t_step = min(4 * per, tiles)
        blocks, steps = t_step // per, tiles // t_step
        views = [(B * tiles, DIL_RES, DIL_CHUNK, c) for c in (C, W, 128)]
        cur_shape, prev_shape = (t_step, None, DIL_CHUNK), (per, None, DIL_CHUNK)
        cur_idx = lambda b, r, i: (b * steps + i, r, 0)
        prev_idx = lambda b, r, i: (b * (tiles // per) + jnp.maximum(i * blocks - 1, 0), r, 0)
    else:
        assert dil == 4 and DIL_RES == 16
        rows, n_res = _Residue4Rows, 4
        per_tile = _Residue4Rows.per_tile
        t_step = min(2, tiles)
        blocks, steps = t_step * per_tile, tiles // t_step
        views = [(B * tiles, 4, 4, DIL_CHUNK, c) for c in (C, W, 128)]
        cur_shape, prev_shape = (t_step, 4, None, DIL_CHUNK), (1, 4, None, DIL_CHUNK)
        cur_idx = lambda b, r, i: (b * steps + i, 0, r, 0)
        prev_idx = lambda b, r, i: (b * tiles + jnp.maximum(i * t_step - 1, 0), 0, r, 0)

    def spec(shape, idx, width, col):
        return pl.BlockSpec(shape + (width,), lambda b, r, i: idx(b, r, i) + (col,))

    qkv_v = qkv.reshape(views[0])
    order = getattr(rows, "order", list(range(BLOCK)))
    o, lse = pl.pallas_call(
        functools.partial(_dil_kernel, sub_blocks=blocks, rows=rows),
        grid=(B, n_res, steps),
        in_specs=[
            pl.BlockSpec((BLOCK, 2 * BLOCK), lambda b, r, i: (0, 0)),
            pl.BlockSpec(memory_space=pltpu.SMEM),
            spec(cur_shape, cur_idx, W, col0),
            spec(cur_shape, cur_idx, W, col0 + 1), spec(prev_shape, prev_idx, W, col0 + 1),
            spec(cur_shape, cur_idx, W, col0 + 2), spec(prev_shape, prev_idx, W, col0 + 2),
        ],
        out_specs=[spec(cur_shape, cur_idx, W, 0), spec(cur_shape, cur_idx, 128, 0)],
        out_shape=[jax.ShapeDtypeStruct(views[1], BF16), jax.ShapeDtypeStruct(views[2], F32)],
        scratch_shapes=[pltpu.VMEM((DIL_GROUP_HEADS, BLOCK, 2 * BLOCK), F32)],
        compiler_params=_params(("arbitrary", "arbitrary", "arbitrary")),
        name=f"dilated_group{g}",
    )(_rel_bucket_table(dil, order), rel_table, qkv_v, qkv_v, qkv_v, qkv_v, qkv_v)
    return o.reshape(B * S, W), lse.reshape(B * S, 128)


def _bf16_pieces(x):
    hi = x.astype(BF16)
    rest = x - hi.astype(F32)
    mid = rest.astype(BF16)
    return [hi, mid, (rest - mid.astype(F32)).astype(BF16)]


def _combine_matmul_res_kernel(o0_ref, o1_ref, o2_ref, l0_ref, l1_ref, l2_ref, pt_ref, w_ref, r_ref, out_ref,
                               comb_ref):
    rows = out_ref.shape[0]
    pt = pt_ref[pl.ds(pl.multiple_of(pl.program_id(1) * rows, rows), rows), :]
    pieces = jnp.concatenate(_bf16_pieces(l1_ref[...]) + _bf16_pieces(l2_ref[...]), axis=1)
    lse = jnp.dot(pt, pieces, preferred_element_type=F32)
    l0 = l0_ref[...]
    l1 = lse[:, 0:128] + lse[:, 128:256] + lse[:, 256:384]
    l2 = lse[:, 384:512] + lse[:, 512:640] + lse[:, 640:768]
    mx = jnp.maximum(jnp.maximum(l0, l1), l2)
    e0, e1, e2 = jnp.exp(l0 - mx), jnp.exp(l1 - mx), jnp.exp(l2 - mx)
    den = e0 + e1 + e2
    w0, w1, w2 = e0 / den, e1 / den, e2 / den
    pair = 2 * HEAD_DIM
    for hp in range(DIL_GROUP_HEADS // 2):
        pcols = slice(hp * pair, (hp + 1) * pair)
        o1 = jnp.dot(pt, o1_ref[:, pcols], preferred_element_type=F32)
        o2 = jnp.dot(pt, o2_ref[:, pcols], preferred_element_type=F32)
        for k in range(2):
            h = 2 * hp + k
            cols = slice(h * HEAD_DIM, (h + 1) * HEAD_DIM)
            sub = slice(k * HEAD_DIM, (k + 1) * HEAD_DIM)
            c = (w0[:, h:h + 1] * o0_ref[:, cols].astype(F32)
                 + w1[:, h:h + 1] * o1[:, sub] + w2[:, h:h + 1] * o2[:, sub])
            comb_ref[:, cols] = c.astype(comb_ref.dtype)
    out_ref[...] = r_ref[...] + jnp.dot(comb_ref[...], w_ref[...], preferred_element_type=F32)


def _combine_matmul_res(os, lses, unperm, w, res, rows):
    M, K = os[0].shape
    N = w.shape[1]
    parts = DIL_TILE // rows
    own_o = pl.BlockSpec((rows, K), lambda i, p: (i * parts + p, 0))
    own_l = pl.BlockSpec((rows, 128), lambda i, p: (i * parts + p, 0))
    tile_o = pl.BlockSpec((DIL_TILE, K), lambda i, p: (i, 0))
    tile_l = pl.BlockSpec((DIL_TILE, 128), lambda i, p: (i, 0))
    row_block = pl.BlockSpec((rows, N), lambda i, p: (i * parts + p, 0))
    return pl.pallas_call(
        _combine_matmul_res_kernel,
        grid=(M // DIL_TILE, parts),
        in_specs=[own_o, tile_o, tile_o, own_l, tile_l, tile_l,
                  pl.BlockSpec((DIL_TILE, DIL_TILE), lambda i, p: (0, 0), pipeline_mode=pl.Buffered(1)),
                  pl.BlockSpec((K, N), lambda i, p: (0, 0), pipeline_mode=pl.Buffered(1)),
                  row_block],
        out_specs=row_block,
        out_shape=jax.ShapeDtypeStruct((M, N), F32),
        scratch_shapes=[pltpu.VMEM((rows, K), BF16)],
        compiler_params=_params(("parallel", "parallel")),
        name="combine_matmul_res",
    )(*os, *lses, unperm, w, res)


def _ffn_up_kernel(x_ref, g_ref, wg_ref, wv_ref, cw_ref, cb_ref, o_ref, hn_even_ref, hn_odd_ref,
                   *, tiles_per_seq):
    i, j = pl.program_id(0), pl.program_id(1)
    tm = hn_even_ref.shape[0] - CONV_HALO

    def gated(hn_ref):
        gate = jnp.dot(hn_ref[...], wg_ref[...], preferred_element_type=F32)
        val = jnp.dot(hn_ref[CONV_HALO:, :], wv_ref[...], preferred_element_type=F32)
        conv = cb_ref[...] + gate[CONV_HALO:] * cw_ref[CONV_WIDTH - 1:CONV_WIDTH, :]
        for back in range(1, CONV_WIDTH):
            tap = CONV_WIDTH - 1 - back
            conv = conv + pltpu.roll(gate, back, axis=0)[CONV_HALO:] * cw_ref[tap:tap + 1, :]
        act = conv / (1.0 + jnp.exp(-conv)) * val
        o_ref[...] = act.astype(o_ref.dtype)

    def row_step(filling_ref, finished_ref):
        tail = finished_ref[tm:tm + CONV_HALO, :]
        filling_ref[0:CONV_HALO, :] = jnp.where(i % tiles_per_seq == 0, jnp.zeros_like(tail), tail)
        _norm_share(x_ref, g_ref, filling_ref, CONV_HALO, j)
        gated(finished_ref)

    @pl.when(i == 0)
    def _():
        hn_even_ref[0:CONV_HALO, :] = jnp.zeros((CONV_HALO, hn_even_ref.shape[1]), hn_even_ref.dtype)
        _norm_share(x_ref, g_ref, hn_even_ref, CONV_HALO, j)

    @pl.when(i % 2 == 1)
    def _():
        row_step(hn_odd_ref, hn_even_ref)

    @pl.when(jnp.logical_and(i % 2 == 0, i > 0))
    def _():
        row_step(hn_even_ref, hn_odd_ref)


def _ffn_up(x, g, w_up, conv_w, conv_b, S, tm, tf):
    M, K = x.shape
    d_ff = w_up.shape[1] // 2
    tiles, steps = M // tm, d_ff // tf

    def col(i, j):
        return jnp.where(i > 0, j, 0)

    return pl.pallas_call(
        functools.partial(_ffn_up_kernel, tiles_per_seq=S // tm),
        grid=(tiles + 1, steps),
        in_specs=[
            _share_spec(tm, K, tiles, steps),
            pl.BlockSpec((1, K), lambda i, j: (0, 0)),
            pl.BlockSpec((K, tf), lambda i, j: (0, col(i, j))),
            pl.BlockSpec((K, tf), lambda i, j: (0, steps + col(i, j))),
            pl.BlockSpec((CONV_WIDTH, tf), lambda i, j: (0, col(i, j))),
            pl.BlockSpec((1, tf), lambda i, j: (0, col(i, j))),
        ],
        out_specs=pl.BlockSpec((tm, tf), lambda i, j: (jnp.maximum(i - 1, 0), col(i, j))),
        out_shape=jax.ShapeDtypeStruct((M, d_ff), BF16),
        scratch_shapes=[pltpu.VMEM((tm + CONV_HALO, K), BF16), pltpu.VMEM((tm + CONV_HALO, K), BF16)],
        compiler_params=_params(("arbitrary", "arbitrary")),
        name="ffn_up",
    )(x, g.reshape(1, K), w_up, w_up, conv_w, conv_b.reshape(1, d_ff))


def kernel(x, ln_mix, ln_ffn, ln_f, w_qkv_sb, w_o_sb, w_qkv_dil, w_o_dil, rel_bias, w_up, conv_w, conv_b, w_down):
    B, S, D = x.shape
    depth = ln_mix.shape[0]
    h = x.reshape(B * S, D)
    for i in range(depth):
        j = i // N_MIXERS
        if i % N_MIXERS == 0:
            qkv = _norm_matmul_heads(h, ln_mix[i], w_qkv_sb[j].astype(BF16), tm=1024, tn=1536)
            o = _stick_breaking(qkv, B, S, tq=min(2048, S), heads=2)
            h = _matmul_res(o, w_o_sb[j].astype(BF16), h, tm=512)
        else:
            perm = _residue_major_permutation()
            qkv = _norm_matmul_perm(h, ln_mix[i], w_qkv_dil[j].astype(BF16), jnp.asarray(perm, BF16),
                                    natural_cols=3 * DIL_WIDTH, tn=1536)
            os, lses = [], []
            for g, (window, dil) in enumerate(DIL_PATTERNS):
                assert window // dil == BLOCK and S % (dil * BLOCK) == 0
                table = rel_bias[:, g * DIL_GROUP_HEADS:(g + 1) * DIL_GROUP_HEADS]
                o, lse = _dilated_group(qkv, table, B, S, g, dil)
                os.append(o)
                lses.append(lse)
            h = _combine_matmul_res(os, lses, jnp.asarray(perm.T, BF16), w_o_dil[j].astype(BF16), h, rows=512)
        act = _ffn_up(h, ln_ffn[i], w_up[i].astype(BF16), conv_w[i], conv_b[i], S, tm=1024, tf=1024)
        h = _matmul_res(act, w_down[i].astype(BF16), h, tm=512, norm_gain=ln_f if i == depth - 1 else None)
    return h.reshape(B, S, D)
```

```python
import functools
import math

import jax
import jax.numpy as jnp
import numpy as np
from jax import lax
from jax.experimental import pallas as pl
from jax.experimental.pallas import tpu as pltpu

BLOCK = 128
SB_HEADS = 16
HEAD_DIM = 128
DIL_PATTERNS = ((128, 1), (512, 4), (2048, 16))
DIL_GROUPS = len(DIL_PATTERNS)
DIL_GROUP_HEADS = 8
DIL_WIDTH = DIL_GROUP_HEADS * HEAD_DIM
REL_BUCKETS = 32
REL_MAX_DISTANCE = 2048
CONV_WIDTH = 3
RMS_EPS = 1e-6
NEG_INF = -1e30
LOG2E = math.log2(math.e)
N_MIXERS = 2

SB_EXIT_LOG = -104.0
SB_STATIC_DEPTH = 3

DIL_TILE = 1024
DIL_RES = 16
DIL_CHUNK = DIL_TILE // DIL_RES
CONV_HALO = 16
NORM_CHUNK = 64
V7X_VMEM_BYTES = 64 * 1024 * 1024
VMEM_LIMIT = V7X_VMEM_BYTES * 7 // 8

BF16 = jnp.bfloat16
F32 = jnp.float32


def _params(semantics):
    return pltpu.CompilerParams(dimension_semantics=semantics, vmem_limit_bytes=VMEM_LIMIT)


def _rms_rows(x, g):
    ms = jnp.mean(x * x, axis=-1, keepdims=True)
    return x * lax.rsqrt(ms + RMS_EPS) * g


def _norm_into(x_ref, g_ref, dst_ref, dst_row0, chunk):
    g = g_ref[...]

    def body(c, carry):
        r = pl.multiple_of(c * chunk, chunk)
        y = _rms_rows(x_ref[pl.ds(r, chunk), :], g)
        dst_ref[pl.ds(dst_row0 + r, chunk), :] = y.astype(dst_ref.dtype)
        return carry

    lax.fori_loop(0, x_ref.shape[0] // chunk, body, 0)


def _norm_share(x_ref, g_ref, dst_ref, dst_row0, step):
    rows = x_ref.shape[0]
    shares = (dst_ref.shape[0] - dst_row0) // rows
    base = dst_row0 + jnp.minimum(step, shares - 1) * rows
    g = g_ref[...]
    for c in range(rows // NORM_CHUNK):
        y = _rms_rows(x_ref[c * NORM_CHUNK:(c + 1) * NORM_CHUNK, :], g)
        dst_ref[pl.ds(pl.multiple_of(base + c * NORM_CHUNK, NORM_CHUNK // 4), NORM_CHUNK), :] = y.astype(dst_ref.dtype)


def _row_shares(steps):
    return 1 << (steps.bit_length() - 1)


def _share_spec(tm, K, tiles, steps):
    shares = _row_shares(steps)
    return pl.BlockSpec((tm // shares, K),
                        lambda i, j: (jnp.minimum(i, tiles - 1) * shares + jnp.minimum(j, shares - 1), 0))


def _norm_matmul_kernel(x_ref, g_ref, w_ref, o_ref, hn_even_ref, hn_odd_ref):
    i, j = pl.program_id(0), pl.program_id(1)

    def project(hn_ref):
        y = jnp.dot(hn_ref[...], w_ref[...], preferred_element_type=F32).astype(o_ref.dtype)
        for c in range(o_ref.shape[0]):
            o_ref[c] = y[:, c * HEAD_DIM:(c + 1) * HEAD_DIM]

    @pl.when(i == 0)
    def _():
        _norm_share(x_ref, g_ref, hn_even_ref, 0, j)

    @pl.when(i % 2 == 1)
    def _():
        _norm_share(x_ref, g_ref, hn_odd_ref, 0, j)
        project(hn_even_ref)

    @pl.when(jnp.logical_and(i % 2 == 0, i > 0))
    def _():
        _norm_share(x_ref, g_ref, hn_even_ref, 0, j)
        project(hn_odd_ref)


def _norm_matmul_heads(x, g, w, tm, tn):
    M, K = x.shape
    N = w.shape[1]
    slabs = tn // HEAD_DIM
    tiles, steps = M // tm, N // tn

    def col(i, j):
        return jnp.where(i > 0, j, 0)

    return pl.pallas_call(
        _norm_matmul_kernel,
        grid=(tiles + 1, steps),
        in_specs=[
            _share_spec(tm, K, tiles, steps),
            pl.BlockSpec((1, K), lambda i, j: (0, 0)),
            pl.BlockSpec((K, tn), lambda i, j: (0, col(i, j))),
        ],
        out_specs=pl.BlockSpec((slabs, tm, HEAD_DIM), lambda i, j: (col(i, j), jnp.maximum(i - 1, 0), 0)),
        out_shape=jax.ShapeDtypeStruct((N // HEAD_DIM, M, HEAD_DIM), BF16),
        scratch_shapes=[pltpu.VMEM((tm, K), BF16), pltpu.VMEM((tm, K), BF16)],
        compiler_params=_params(("arbitrary", "arbitrary")),
        name="norm_matmul",
    )(x, g.reshape(1, K), w)


def _matmul_res_kernel(a_ref, w_ref, r_ref, o_ref):
    o_ref[...] = r_ref[...] + jnp.dot(a_ref[...], w_ref[...], preferred_element_type=F32)


def _matmul_res_norm_kernel(a_ref, w_ref, r_ref, g_ref, o_ref):
    y = r_ref[...] + jnp.dot(a_ref[...], w_ref[...], preferred_element_type=F32)
    o_ref[...] = _rms_rows(y, g_ref[...])


def _matmul_res(a, w, res, tm, norm_gain=None):
    M, K = a.shape
    N = w.shape[1]
    in_specs = [
        pl.BlockSpec((tm, K), lambda i: (i, 0)),
        pl.BlockSpec((K, N), lambda i: (0, 0), pipeline_mode=pl.Buffered(1)),
        pl.BlockSpec((tm, N), lambda i: (i, 0)),
    ]
    args = [a, w, res]
    if norm_gain is not None:
        in_specs.append(pl.BlockSpec((1, N), lambda i: (0, 0)))
        args.append(norm_gain.reshape(1, N))
    return pl.pallas_call(
        _matmul_res_kernel if norm_gain is None else _matmul_res_norm_kernel,
        grid=(M // tm,),
        in_specs=in_specs,
        out_specs=pl.BlockSpec((tm, N), lambda i: (i, 0)),
        out_shape=jax.ShapeDtypeStruct((M, N), F32),
        compiler_params=_params(("parallel",)),
        name="matmul_res" if norm_gain is None else "matmul_res_norm",
    )(*args)


def _sb_kernel(q_ref, k_ref, v_ref, o_ref, acc_ref, spent_ref, *, chains, heads):
    scale = HEAD_DIM ** -0.5

    row = lax.broadcasted_iota(jnp.int32, (BLOCK, BLOCK), 0)
    col = lax.broadcasted_iota(jnp.int32, (BLOCK, BLOCK), 1)
    causal = col < row
    suffix = (row >= col).astype(BF16)
    base = pl.program_id(2) * chains
    contract_last = (((1,), (1,)), ((), ()))

    def mask_first_block(x):
        head = jnp.where(causal, x[:BLOCK], 0.0)
        return head if x.shape[0] == BLOCK else jnp.concatenate([head, x[BLOCK:]], axis=0)

    def key_block(h, rows, kb, on_diagonal):
        start = pl.multiple_of(kb * BLOCK, BLOCK)
        k = k_ref[h, pl.ds(start, BLOCK), :]
        v = v_ref[h, pl.ds(start, BLOCK), :]
        z = lax.dot_general(q_ref[h, rows, :], k, contract_last, preferred_element_type=F32) * scale
        drop = jnp.maximum(z, 0.0) + jnp.log(1.0 + jnp.exp2(jnp.abs(z) * -LOG2E))
        if on_diagonal:
            drop = mask_first_block(drop)
        upto = jnp.dot(drop.astype(BF16), suffix, preferred_element_type=F32)
        a = jnp.exp(z - (upto + spent_ref[h, rows, :]))
        if on_diagonal:
            a = mask_first_block(a)
        acc_ref[h, rows, :] += jnp.dot(a.astype(BF16), v, preferred_element_type=F32)
        spent_ref[h, rows, :] += upto[:, 0:1]

    def retire_if_no_keys(h, rows, kb):
        spent_ref[h, rows, :] = jnp.where(kb >= 0, spent_ref[h, rows, :], -NEG_INF)

    acc_ref[...] = jnp.zeros_like(acc_ref)
    spent_ref[...] = jnp.zeros_like(spent_ref)
    for first in range(chains - 1, -SB_STATIC_DEPTH, -1):
        lo, hi = max(first, 0), min(first + SB_STATIC_DEPTH - 1, chains - 1)
        rows = slice(lo * BLOCK, (hi + 1) * BLOCK)
        kb = base + first
        for h in range(heads):
            if first >= 0:
                key_block(h, rows, kb, True)
            else:
                retire_if_no_keys(h, rows, kb)
                key_block(h, rows, jnp.maximum(kb, 0), False)

    def cond(carry):
        _, least_spent = carry
        return least_spent < -SB_EXIT_LOG

    def body(carry):
        delta, _ = carry
        for c in range(chains):
            rows = slice(c * BLOCK, (c + 1) * BLOCK)
            kb = base + c - delta
            for h in range(heads):
                retire_if_no_keys(h, rows, kb)
                key_block(h, rows, jnp.maximum(kb, 0), False)
        return delta + 1, jnp.min(spent_ref[...])

    lax.while_loop(cond, body, (jnp.int32(SB_STATIC_DEPTH), jnp.min(spent_ref[...])))
    for h in range(heads):
        o_ref[:, h * HEAD_DIM:(h + 1) * HEAD_DIM] = acc_ref[h].astype(o_ref.dtype)


def _stick_breaking(qkv, B, S, tq, heads):
    H = SB_HEADS
    nq = S // tq
    hb = H // heads
    return pl.pallas_call(
        functools.partial(_sb_kernel, chains=tq // BLOCK, heads=heads),
        grid=(B, hb, nq),
        in_specs=[
            pl.BlockSpec((heads, tq, HEAD_DIM), lambda b, h, i: (h, b * nq + i, 0)),
            pl.BlockSpec((heads, S, HEAD_DIM), lambda b, h, i: (hb + h, b, 0)),
            pl.BlockSpec((heads, S, HEAD_DIM), lambda b, h, i: (2 * hb + h, b, 0)),
        ],
        out_specs=pl.BlockSpec((tq, heads * HEAD_DIM), lambda b, h, i: (b * nq + i, h)),
        out_shape=jax.ShapeDtypeStruct((B * S, H * HEAD_DIM), BF16),
        scratch_shapes=[pltpu.VMEM((heads, tq, HEAD_DIM), F32), pltpu.VMEM((heads, tq, BLOCK), F32)],
        compiler_params=_params(("parallel", "parallel", "arbitrary")),
        name="stick_breaking",
    )(qkv, qkv, qkv)


def _residue_major_permutation():
    i = np.arange(DIL_TILE)
    p = np.zeros((DIL_TILE, DIL_TILE), np.float32)
    p[i, (i % DIL_CHUNK) * DIL_RES + i // DIL_CHUNK] = 1.0
    return p


def _norm_matmul_perm_kernel(x_ref, g_ref, p_ref, w_ref, o_ref, hn_even_ref, hn_odd_ref, hnp_ref,
                             *, natural_tiles, chunk):
    i, j = pl.program_id(0), pl.program_id(1)

    def permute(hn_ref):
        for c in range(hn_ref.shape[1] // chunk):
            cols = slice(c * chunk, (c + 1) * chunk)
            hnp_ref[:, cols] = jnp.dot(p_ref[...], hn_ref[:, cols],
                                       preferred_element_type=F32).astype(hnp_ref.dtype)

    def project(lhs_ref):
        o_ref[...] = jnp.dot(lhs_ref[...], w_ref[...], preferred_element_type=F32).astype(o_ref.dtype)

    def row_step(filling_ref, finished_ref):
        @pl.when(j == 0)
        def _():
            permute(finished_ref)

        @pl.when(j < natural_tiles)
        def _():
            _norm_share(x_ref, g_ref, filling_ref, 0, j)
            project(finished_ref)

        @pl.when(j >= natural_tiles)
        def _():
            _norm_share(x_ref, g_ref, filling_ref, 0, j)
            project(hnp_ref)

    @pl.when(i == 0)
    def _():
        _norm_share(x_ref, g_ref, hn_even_ref, 0, j)

    @pl.when(i % 2 == 1)
    def _():
        row_step(hn_odd_ref, hn_even_ref)

    @pl.when(jnp.logical_and(i % 2 == 0, i > 0))
    def _():
        row_step(hn_even_ref, hn_odd_ref)


def _norm_matmul_perm(x, g, w, perm, natural_cols, tn):
    M, K = x.shape
    N = w.shape[1]
    tm = DIL_TILE
    tiles, steps = M // tm, N // tn

    def col(i, j):
        return jnp.where(i > 0, j, 0)

    return pl.pallas_call(
        functools.partial(_norm_matmul_perm_kernel, natural_tiles=natural_cols // tn, chunk=512),
        grid=(tiles + 1, steps),
        in_specs=[
            _share_spec(tm, K, tiles, steps),
            pl.BlockSpec((1, K), lambda i, j: (0, 0)),
            pl.BlockSpec((tm, tm), lambda i, j: (0, 0), pipeline_mode=pl.Buffered(1)),
            pl.BlockSpec((K, tn), lambda i, j: (0, col(i, j))),
        ],
        out_specs=pl.BlockSpec((tm, tn), lambda i, j: (jnp.maximum(i - 1, 0), col(i, j))),
        out_shape=jax.ShapeDtypeStruct((M, N), BF16),
        scratch_shapes=[pltpu.VMEM((tm, K), BF16), pltpu.VMEM((tm, K), BF16), pltpu.VMEM((tm, K), BF16)],
        compiler_params=_params(("arbitrary", "arbitrary")),
        name="norm_matmul_perm",
    )(x, g.reshape(1, K), perm, w)


def _rel_bucket_table(dil, order):
    order = np.asarray(order)
    qi = order[:, None]
    kc = np.concatenate([order, BLOCK + order])[None, :]
    rel = BLOCK + qi - kc
    n = jnp.asarray(np.maximum(rel, 0) * dil)
    max_exact = REL_BUCKETS // 2
    nf = jnp.maximum(n, 1).astype(F32)
    large = max_exact + (jnp.log(nf / max_exact) / math.log(REL_MAX_DISTANCE / max_exact)
                         * (REL_BUCKETS - max_exact)).astype(jnp.int32)
    large = jnp.minimum(large, REL_BUCKETS - 1)
    bucket = jnp.where(n < max_exact, n, large).astype(jnp.int32)
    return jnp.where(jnp.asarray((rel >= 0) & (rel <= BLOCK)), bucket, -1)


class _NaturalRows:
    @staticmethod
    def load(ref, j, cols):
        return ref[j * BLOCK:(j + 1) * BLOCK, cols]

    @staticmethod
    def load_prev(ref, cols):
        return ref[:, cols]

    @staticmethod
    def store(ref, j, cols, val):
        ref[j * BLOCK:(j + 1) * BLOCK, cols] = val


class _Residue16Rows:
    per = BLOCK // DIL_CHUNK

    @classmethod
    def load(cls, ref, j, cols):
        return jnp.concatenate([ref[cls.per * j + t, :, cols] for t in range(cls.per)], axis=0)

    @classmethod
    def load_prev(cls, ref, cols):
        return jnp.concatenate([ref[t, :, cols] for t in range(cls.per)], axis=0)

    @classmethod
    def store(cls, ref, j, cols, val):
        for t in range(cls.per):
            ref[cls.per * j + t, :, cols] = val[t * DIL_CHUNK:(t + 1) * DIL_CHUNK]


class _Residue4Rows:
    sub = BLOCK // 4
    per_tile = DIL_CHUNK // sub
    order = [4 * m + a for a in range(4) for m in range(BLOCK // 4)]

    @classmethod
    def _rows(cls, j):
        return j // cls.per_tile, slice((j % cls.per_tile) * cls.sub, (j % cls.per_tile + 1) * cls.sub)

    @classmethod
    def load(cls, ref, j, cols):
        t, ms = cls._rows(j)
        return jnp.concatenate([ref[t, a, ms, cols] for a in range(4)], axis=0)

    @classmethod
    def load_prev(cls, ref, cols):
        return cls.load(ref, cls.per_tile - 1, cols)

    @classmethod
    def store(cls, ref, j, cols, val):
        t, ms = cls._rows(j)
        for a in range(4):
            ref[t, a, ms, cols] = val[a * cls.sub:(a + 1) * cls.sub]


def _dil_kernel(bucket_ref, table_ref, q_ref, kc_ref, kp_ref, vc_ref, vp_ref, o_ref, lse_ref, bias_ref,
                *, sub_blocks, rows):
    scale = HEAD_DIM ** -0.5
    first_step = jnp.logical_and(jnp.logical_and(pl.program_id(0) == 0, pl.program_id(1) == 0),
                                 pl.program_id(2) == 0)

    @pl.when(first_step)
    def _():
        bucket = bucket_ref[...]
        for h in range(DIL_GROUP_HEADS):
            b = jnp.full(bucket.shape, NEG_INF, F32)
            for t in range(REL_BUCKETS):
                b = jnp.where(bucket == t, table_ref[t, h], b)
            bias_ref[h] = b

    lane = lax.broadcasted_iota(jnp.int32, (BLOCK, 128), 1)
    keep_keys = jnp.logical_or(lax.broadcasted_iota(jnp.int32, (BLOCK, 2 * BLOCK), 1) >= BLOCK,
                               pl.program_id(2) > 0)
    ones = jnp.ones((2 * BLOCK, HEAD_DIM), BF16)
    contract_last = (((1,), (1,)), ((), ()))
    all_lanes = slice(None)

    def prev_and_cur(cur_ref, prev_ref, j, cols):
        prev = rows.load_prev(prev_ref, cols) if j == 0 else rows.load(cur_ref, j - 1, cols)
        return jnp.concatenate([prev, rows.load(cur_ref, j, cols)], axis=0)

    for j in range(sub_blocks):
        stats = jnp.zeros((BLOCK, 128), F32)
        for h in range(DIL_GROUP_HEADS):
            cols = slice(h * HEAD_DIM, (h + 1) * HEAD_DIM)
            q = rows.load(q_ref, j, cols)
            k = prev_and_cur(kc_ref, kp_ref, j, cols)
            v = prev_and_cur(vc_ref, vp_ref, j, cols)
            s = lax.dot_general(q, k, contract_last, preferred_element_type=F32) * scale + bias_ref[h]
            if j == 0:
                s = jnp.where(keep_keys, s, NEG_INF)
            m = jnp.max(s, axis=-1, keepdims=True)
            p = jnp.exp(s - m).astype(BF16)
            ol = jnp.dot(p, jnp.concatenate([v, ones], axis=1), preferred_element_type=F32)
            o, l = ol[:, :HEAD_DIM], ol[:, HEAD_DIM:]
            rows.store(o_ref, j, cols, (o / l).astype(o_ref.dtype))
            stats = jnp.where(lane == h, m + jnp.log(l), stats)
        rows.store(lse_ref, j, all_lanes, stats)


def _dilated_group(qkv, rel_table, B, S, g, dil):
    W, C = DIL_WIDTH, qkv.shape[1]
    tiles = S // DIL_TILE
    col0 = g * 3
    if dil == 1:
        rows, n_res = _NaturalRows, 1
        blocks = min(4, S // BLOCK)
        steps = S // (blocks * BLOCK)
        views = [(B * S, C), (B * S, W), (B * S, 128)]
        cur_shape, prev_shape = (blocks * BLOCK,), (BLOCK,)
        cur_idx = lambda b, r, i: (b * steps + i,)
        prev_idx = lambda b, r, i: (b * (S // BLOCK) + jnp.maximum(i * blocks - 1, 0),)
    elif dil == DIL_RES:
        rows, n_res = _Residue16Rows, DIL_RES
        per = _Residue16Rows.per
        t_step = min(4 * per, tiles)
        blocks, steps = t_step // per, tiles // t_step
        views = [(B * tiles, DIL_RES, DIL_CHUNK, c) for c in (C, W, 128)]
        cur_shape, prev_shape = (t_step, None, DIL_CHUNK), (per, None, DIL_CHUNK)
        cur_idx = lambda b, r, i: (b * steps + i, r, 0)
        prev_idx = lambda b, r, i: (b * (tiles // per) + jnp.maximum(i * blocks - 1, 0), r, 0)
    else:
        assert dil == 4 and DIL_RES == 16
        rows, n_res = _Residue4Rows, 4
        per_tile = _Residue4Rows.per_tile
        t_step = min(2, tiles)
        blocks, steps = t_step * per_tile, tiles // t_step
        views = [(B * tiles, 4, 4, DIL_CHUNK, c) for c in (C, W, 128)]
        cur_shape, prev_shape = (t_step, 4, None, DIL_CHUNK), (1, 4, None, DIL_CHUNK)
        cur_idx = lambda b, r, i: (b * steps + i, 0, r, 0)
        prev_idx = lambda b, r, i: (b * tiles + jnp.maximum(i * t_step - 1, 0), 0, r, 0)

    def spec(shape, idx, width, col):
        return pl.BlockSpec(shape + (width,), lambda b, r, i: idx(b, r, i) + (col,))

    qkv_v = qkv.reshape(views[0])
    order = getattr(rows, "order", list(range(BLOCK)))
    o, lse = pl.pallas_call(
        functools.partial(_dil_kernel, sub_blocks=blocks, rows=rows),
        grid=(B, n_res, steps),
        in_specs=[
            pl.BlockSpec((BLOCK, 2 * BLOCK), lambda b, r, i: (0, 0)),
            pl.BlockSpec(memory_space=pltpu.SMEM),
            spec(cur_shape, cur_idx, W, col0),
            spec(cur_shape, cur_idx, W, col0 + 1), spec(prev_shape, prev_idx, W, col0 + 1),
            spec(cur_shape, cur_idx, W, col0 + 2), spec(prev_shape, prev_idx, W, col0 + 2),
        ],
        out_specs=[spec(cur_shape, cur_idx, W, 0), spec(cur_shape, cur_idx, 128, 0)],
        out_shape=[jax.ShapeDtypeStruct(views[1], BF16), jax.ShapeDtypeStruct(views[2], F32)],
        scratch_shapes=[pltpu.VMEM((DIL_GROUP_HEADS, BLOCK, 2 * BLOCK), F32)],
        compiler_params=_params(("arbitrary", "arbitrary", "arbitrary")),
        name=f"dilated_group{g}",
    )(_rel_bucket_table(dil, order), rel_table, qkv_v, qkv_v, qkv_v, qkv_v, qkv_v)
    return o.reshape(B * S, W), lse.reshape(B * S, 128)


def _bf16_pieces(x):
    hi = x.astype(BF16)
    rest = x - hi.astype(F32)
    mid = rest.astype(BF16)
    return [hi, mid, (rest - mid.astype(F32)).astype(BF16)]


def _combine_matmul_res_kernel(o0_ref, o1_ref, o2_ref, l0_ref, l1_ref, l2_ref, pt_ref, w_ref, r_ref, out_ref,
                               comb_ref):
    rows = out_ref.shape[0]
    pt = pt_ref[pl.ds(pl.multiple_of(pl.program_id(1) * rows, rows), rows), :]
    pieces = jnp.concatenate(_bf16_pieces(l1_ref[...]) + _bf16_pieces(l2_ref[...]), axis=1)
    lse = jnp.dot(pt, pieces, preferred_element_type=F32)
    l0 = l0_ref[...]
    l1 = lse[:, 0:128] + lse[:, 128:256] + lse[:, 256:384]
    l2 = lse[:, 384:512] + lse[:, 512:640] + lse[:, 640:768]
    mx = jnp.maximum(jnp.maximum(l0, l1), l2)
    e0, e1, e2 = jnp.exp(l0 - mx), jnp.exp(l1 - mx), jnp.exp(l2 - mx)
    den = e0 + e1 + e2
    w0, w1, w2 = e0 / den, e1 / den, e2 / den
    pair = 2 * HEAD_DIM
    for hp in range(DIL_GROUP_HEADS // 2):
        pcols = slice(hp * pair, (hp + 1) * pair)
        o1 = jnp.dot(pt, o1_ref[:, pcols], preferred_element_type=F32)
        o2 = jnp.dot(pt, o2_ref[:, pcols], preferred_element_type=F32)
        for k in range(2):
            h = 2 * hp + k
            cols = slice(h * HEAD_DIM, (h + 1) * HEAD_DIM)
            sub = slice(k * HEAD_DIM, (k + 1) * HEAD_DIM)
            c = (w0[:, h:h + 1] * o0_ref[:, cols].astype(F32)
                 + w1[:, h:h + 1] * o1[:, sub] + w2[:, h:h + 1] * o2[:, sub])
            comb_ref[:, cols] = c.astype(comb_ref.dtype)
    out_ref[...] = r_ref[...] + jnp.dot(comb_ref[...], w_ref[...], preferred_element_type=F32)


def _combine_matmul_res(os, lses, unperm, w, res, rows):
    M, K = os[0].shape
    N = w.shape[1]
    parts = DIL_TILE // rows
    own_o = pl.BlockSpec((rows, K), lambda i, p: (i * parts + p, 0))
    own_l = pl.BlockSpec((rows, 128), lambda i, p: (i * parts + p, 0))
    tile_o = pl.BlockSpec((DIL_TILE, K), lambda i, p: (i, 0))
    tile_l = pl.BlockSpec((DIL_TILE, 128), lambda i, p: (i, 0))
    row_block = pl.BlockSpec((rows, N), lambda i, p: (i * parts + p, 0))
    return pl.pallas_call(
        _combine_matmul_res_kernel,
        grid=(M // DIL_TILE, parts),
        in_specs=[own_o, tile_o, tile_o, own_l, tile_l, tile_l,
                  pl.BlockSpec((DIL_TILE, DIL_TILE), lambda i, p: (0, 0), pipeline_mode=pl.Buffered(1)),
                  pl.BlockSpec((K, N), lambda i, p: (0, 0), pipeline_mode=pl.Buffered(1)),
                  row_block],
        out_specs=row_block,
        out_shape=jax.ShapeDtypeStruct((M, N), F32),
        scratch_shapes=[pltpu.VMEM((rows, K), BF16)],
        compiler_params=_params(("parallel", "parallel")),
        name="combine_matmul_res",
    )(*os, *lses, unperm, w, res)


def _ffn_up_kernel(x_ref, g_ref, wg_ref, wv_ref, cw_ref, cb_ref, o_ref, hn_even_ref, hn_odd_ref,
                   *, tiles_per_seq):
    i, j = pl.program_id(0), pl.program_id(1)
    tm = hn_even_ref.shape[0] - CONV_HALO

    def gated(hn_ref):
        gate = jnp.dot(hn_ref[...], wg_ref[...], preferred_element_type=F32)
        val = jnp.dot(hn_ref[CONV_HALO:, :], wv_ref[...], preferred_element_type=F32)
        conv = cb_ref[...] + gate[CONV_HALO:] * cw_ref[CONV_WIDTH - 1:CONV_WIDTH, :]
        for back in range(1, CONV_WIDTH):
            tap = CONV_WIDTH - 1 - back
            conv = conv + pltpu.roll(gate, back, axis=0)[CONV_HALO:] * cw_ref[tap:tap + 1, :]
        act = conv / (1.0 + jnp.exp(-conv)) * val
        o_ref[...] = act.astype(o_ref.dtype)

    def row_step(filling_ref, finished_ref):
        tail = finished_ref[tm:tm + CONV_HALO, :]
        filling_ref[0:CONV_HALO, :] = jnp.where(i % tiles_per_seq == 0, jnp.zeros_like(tail), tail)
        _norm_share(x_ref, g_ref, filling_ref, CONV_HALO, j)
        gated(finished_ref)

    @pl.when(i == 0)
    def _():
        hn_even_ref[0:CONV_HALO, :] = jnp.zeros((CONV_HALO, hn_even_ref.shape[1]), hn_even_ref.dtype)
        _norm_share(x_ref, g_ref, hn_even_ref, CONV_HALO, j)

    @pl.when(i % 2 == 1)
    def _():
        row_step(hn_odd_ref, hn_even_ref)

    @pl.when(jnp.logical_and(i % 2 == 0, i > 0))
    def _():
        row_step(hn_even_ref, hn_odd_ref)


def _ffn_up(x, g, w_up, conv_w, conv_b, S, tm, tf):
    M, K = x.shape
    d_ff = w_up.shape[1] // 2
    tiles, steps = M // tm, d_ff // tf

    def col(i, j):
        return jnp.where(i > 0, j, 0)

    return pl.pallas_call(
        functools.partial(_ffn_up_kernel, tiles_per_seq=S // tm),
        grid=(tiles + 1, steps),
        in_specs=[
            _share_spec(tm, K, tiles, steps),
            pl.BlockSpec((1, K), lambda i, j: (0, 0)),
            pl.BlockSpec((K, tf), lambda i, j: (0, col(i, j))),
            pl.BlockSpec((K, tf), lambda i, j: (0, steps + col(i, j))),
            pl.BlockSpec((CONV_WIDTH, tf), lambda i, j: (0, col(i, j))),
            pl.BlockSpec((1, tf), lambda i, j: (0, col(i, j))),
        ],
        out_specs=pl.BlockSpec((tm, tf), lambda i, j: (jnp.maximum(i - 1, 0), col(i, j))),
        out_shape=jax.ShapeDtypeStruct((M, d_ff), BF16),
        scratch_shapes=[pltpu.VMEM((tm + CONV_HALO, K), BF16), pltpu.VMEM((tm + CONV_HALO, K), BF16)],
        compiler_params=_params(("arbitrary", "arbitrary")),
        name="ffn_up",
    )(x, g.reshape(1, K), w_up, w_up, conv_w, conv_b.reshape(1, d_ff))


def kernel(x, ln_mix, ln_ffn, ln_f, w_qkv_sb, w_o_sb, w_qkv_dil, w_o_dil, rel_bias, w_up, conv_w, conv_b, w_down):
    B, S, D = x.shape
    depth = ln_mix.shape[0]
    h = x.reshape(B * S, D)
    for i in range(depth):
        j = i // N_MIXERS
        if i % N_MIXERS == 0:
            qkv = _norm_matmul_heads(h, ln_mix[i], w_qkv_sb[j].astype(BF16), tm=1024, tn=1536)
            o = _stick_breaking(qkv, B, S, tq=min(2048, S), heads=2)
            h = _matmul_res(o, w_o_sb[j].astype(BF16), h, tm=512)
        else:
            perm = _residue_major_permutation()
            qkv = _norm_matmul_perm(h, ln_mix[i], w_qkv_dil[j].astype(BF16), jnp.asarray(perm, BF16),
                                    natural_cols=3 * DIL_WIDTH, tn=1536)
            os, lses = [], []
            for g, (window, dil) in enumerate(DIL_PATTERNS):
                assert window // dil == BLOCK and S % (dil * BLOCK) == 0
                table = rel_bias[:, g * DIL_GROUP_HEADS:(g + 1) * DIL_GROUP_HEADS]
                o, lse = _dilated_group(qkv, table, B, S, g, dil)
                os.append(o)
                lses.append(lse)
            h = _combine_matmul_res(os, lses, jnp.asarray(perm.T, BF16), w_o_dil[j].astype(BF16), h, rows=512)
        act = _ffn_up(h, ln_ffn[i], w_up[i].astype(BF16), conv_w[i], conv_b[i], S, tm=1024, tf=1024)
        h = _matmul_res(act, w_down[i].astype(BF16), h, tm=512, norm_gain=ln_f if i == depth - 1 else None)
    return h.reshape(B, S, D)
```

```python
import functools
import math

import jax
import jax.numpy as jnp
import numpy as np
from jax import lax
from jax.experimental import pallas as pl
from jax.experimental.pallas import tpu as pltpu

BLOCK = 128
SB_HEADS = 16
HEAD_DIM = 128
DIL_PATTERNS = ((128, 1), (512, 4), (2048, 16))
DIL_GROUPS = len(DIL_PATTERNS)
DIL_GROUP_HEADS = 8
DIL_WIDTH = DIL_GROUP_HEADS * HEAD_DIM
REL_BUCKETS = 32
REL_MAX_DISTANCE = 2048
CONV_WIDTH = 3
RMS_EPS = 1e-6
NEG_INF = -1e30
LOG2E = math.log2(math.e)
N_MIXERS = 2

SB_EXIT_LOG = -104.0
SB_STATIC_DEPTH = 3

DIL_TILE = 256
DIL_RES = 16
DIL_CHUNK = DIL_TILE // DIL_RES
CONV_HALO = 16
NORM_CHUNK = 64
V7X_VMEM_BYTES = 64 * 1024 * 1024
VMEM_LIMIT = V7X_VMEM_BYTES * 7 // 8

BF16 = jnp.bfloat16
F32 = jnp.float32


def _params(semantics):
    return pltpu.CompilerParams(dimension_semantics=semantics, vmem_limit_bytes=VMEM_LIMIT)


def _rms_rows(x, g):
    ms = jnp.mean(x * x, axis=-1, keepdims=True)
    return x * lax.rsqrt(ms + RMS_EPS) * g


def _norm_into(x_ref, g_ref, dst_ref, dst_row0, chunk):
    g = g_ref[...]

    def body(c, carry):
        r = pl.multiple_of(c * chunk, chunk)
        y = _rms_rows(x_ref[pl.ds(r, chunk), :], g)
        dst_ref[pl.ds(dst_row0 + r, chunk), :] = y.astype(dst_ref.dtype)
        return carry

    lax.fori_loop(0, x_ref.shape[0] // chunk, body, 0)


def _norm_share(x_ref, g_ref, dst_ref, dst_row0, step):
    rows = x_ref.shape[0]
    shares = (dst_ref.shape[0] - dst_row0) // rows
    base = dst_row0 + jnp.minimum(step, shares - 1) * rows
    g = g_ref[...]
    for c in range(rows // NORM_CHUNK):
        y = _rms_rows(x_ref[c * NORM_CHUNK:(c + 1) * NORM_CHUNK, :], g)
        dst_ref[pl.ds(pl.multiple_of(base + c * NORM_CHUNK, NORM_CHUNK // 4), NORM_CHUNK), :] = y.astype(dst_ref.dtype)


def _row_shares(steps):
    return 1 << (steps.bit_length() - 1)


def _share_spec(tm, K, tiles, steps):
    shares = _row_shares(steps)
    return pl.BlockSpec((tm // shares, K),
                        lambda i, j: (jnp.minimum(i, tiles - 1) * shares + jnp.minimum(j, shares - 1), 0))


def _norm_matmul_kernel(x_ref, g_ref, w_ref, o_ref, hn_even_ref, hn_odd_ref):
    i, j = pl.program_id(0), pl.program_id(1)

    def project(hn_ref):
        y = jnp.dot(hn_ref[...], w_ref[...], preferred_element_type=F32).astype(o_ref.dtype)
        for c in range(o_ref.shape[0]):
            o_ref[c] = y[:, c * HEAD_DIM:(c + 1) * HEAD_DIM]

    @pl.when(i == 0)
    def _():
        _norm_share(x_ref, g_ref, hn_even_ref, 0, j)

    @pl.when(i % 2 == 1)
    def _():
        _norm_share(x_ref, g_ref, hn_odd_ref, 0, j)
        project(hn_even_ref)

    @pl.when(jnp.logical_and(i % 2 == 0, i > 0))
    def _():
        _norm_share(x_ref, g_ref, hn_even_ref, 0, j)
        project(hn_odd_ref)


def _norm_matmul_heads(x, g, w, tm, tn):
    M, K = x.shape
    N = w.shape[1]
    slabs = tn // HEAD_DIM
    tiles, steps = M // tm, N // tn

    def col(i, j):
        return jnp.where(i > 0, j, 0)

    return pl.pallas_call(
        _norm_matmul_kernel,
        grid=(tiles + 1, steps),
        in_specs=[
            _share_spec(tm, K, tiles, steps),
            pl.BlockSpec((1, K), lambda i, j: (0, 0)),
            pl.BlockSpec((K, tn), lambda i, j: (0, col(i, j))),
        ],
        out_specs=pl.BlockSpec((slabs, tm, HEAD_DIM), lambda i, j: (col(i, j), jnp.maximum(i - 1, 0), 0)),
        out_shape=jax.ShapeDtypeStruct((N // HEAD_DIM, M, HEAD_DIM), BF16),
        scratch_shapes=[pltpu.VMEM((tm, K), BF16), pltpu.VMEM((tm, K), BF16)],
        compiler_params=_params(("arbitrary", "arbitrary")),
        name="norm_matmul",
    )(x, g.reshape(1, K), w)


def _matmul_res_kernel(a_ref, w_ref, r_ref, o_ref):
    o_ref[...] = r_ref[...] + jnp.dot(a_ref[...], w_ref[...], preferred_element_type=F32)


def _matmul_res_norm_kernel(a_ref, w_ref, r_ref, g_ref, o_ref):
    y = r_ref[...] + jnp.dot(a_ref[...], w_ref[...], preferred_element_type=F32)
    o_ref[...] = _rms_rows(y, g_ref[...])


def _matmul_res(a, w, res, tm, norm_gain=None):
    M, K = a.shape
    N = w.shape[1]
    in_specs = [
        pl.BlockSpec((tm, K), lambda i: (i, 0)),
        pl.BlockSpec((K, N), lambda i: (0, 0), pipeline_mode=pl.Buffered(1)),
        pl.BlockSpec((tm, N), lambda i: (i, 0)),
    ]
    args = [a, w, res]
    if norm_gain is not None:
        in_specs.append(pl.BlockSpec((1, N), lambda i: (0, 0)))
        args.append(norm_gain.reshape(1, N))
    return pl.pallas_call(
        _matmul_res_kernel if norm_gain is None else _matmul_res_norm_kernel,
        grid=(M // tm,),
        in_specs=in_specs,
        out_specs=pl.BlockSpec((tm, N), lambda i: (i, 0)),
        out_shape=jax.ShapeDtypeStruct((M, N), F32),
        compiler_params=_params(("parallel",)),
        name="matmul_res" if norm_gain is None else "matmul_res_norm",
    )(*args)


def _sb_kernel(q_ref, k_ref, v_ref, o_ref, acc_ref, spent_ref, *, chains, heads):
    scale = HEAD_DIM ** -0.5

    row = lax.broadcasted_iota(jnp.int32, (BLOCK, BLOCK), 0)
    col = lax.broadcasted_iota(jnp.int32, (BLOCK, BLOCK), 1)
    causal = col < row
    suffix = (row >= col).astype(BF16)
    base = pl.program_id(2) * chains
    contract_last = (((1,), (1,)), ((), ()))

    def mask_first_block(x):
        head = jnp.where(causal, x[:BLOCK], 0.0)
        return head if x.shape[0] == BLOCK else jnp.concatenate([head, x[BLOCK:]], axis=0)

    def key_block(h, rows, kb, on_diagonal):
        start = pl.multiple_of(kb * BLOCK, BLOCK)
        k = k_ref[h, pl.ds(start, BLOCK), :]
        v = v_ref[h, pl.ds(start, BLOCK), :]
        z = lax.dot_general(q_ref[h, rows, :], k, contract_last, preferred_element_type=F32) * scale
        drop = jnp.maximum(z, 0.0) + jnp.log(1.0 + jnp.exp2(jnp.abs(z) * -LOG2E))
        if on_diagonal:
            drop = mask_first_block(drop)
        upto = jnp.dot(drop.astype(BF16), suffix, preferred_element_type=F32)
        a = jnp.exp(z - (upto + spent_ref[h, rows, :]))
        if on_diagonal:
            a = mask_first_block(a)
        acc_ref[h, rows, :] += jnp.dot(a.astype(BF16), v, preferred_element_type=F32)
        spent_ref[h, rows, :] += upto[:, 0:1]

    def retire_if_no_keys(h, rows, kb):
        spent_ref[h, rows, :] = jnp.where(kb >= 0, spent_ref[h, rows, :], -NEG_INF)

    acc_ref[...] = jnp.zeros_like(acc_ref)
    spent_ref[...] = jnp.zeros_like(spent_ref)
    for first in range(chains - 1, -SB_STATIC_DEPTH, -1):
        lo, hi = max(first, 0), min(first + SB_STATIC_DEPTH - 1, chains - 1)
        rows = slice(lo * BLOCK, (hi + 1) * BLOCK)
        kb = base + first
        for h in range(heads):
            if first >= 0:
                key_block(h, rows, kb, True)
            else:
                retire_if_no_keys(h, rows, kb)
                key_block(h, rows, jnp.maximum(kb, 0), False)

    def cond(carry):
        _, least_spent = carry
        return least_spent < -SB_EXIT_LOG

    def body(carry):
        delta, _ = carry
        for c in range(chains):
            rows = slice(c * BLOCK, (c + 1) * BLOCK)
            kb = base + c - delta
            for h in range(heads):
                retire_if_no_keys(h, rows, kb)
                key_block(h, rows, jnp.maximum(kb, 0), False)
        return delta + 1, jnp.min(spent_ref[...])

    lax.while_loop(cond, body, (jnp.int32(SB_STATIC_DEPTH), jnp.min(spent_ref[...])))
    for h in range(heads):
        o_ref[:, h * HEAD_DIM:(h + 1) * HEAD_DIM] = acc_ref[h].astype(o_ref.dtype)


def _stick_breaking(qkv, B, S, tq, heads):
    H = SB_HEADS
    nq = S // tq
    hb = H // heads
    return pl.pallas_call(
        functools.partial(_sb_kernel, chains=tq // BLOCK, heads=heads),
        grid=(B, hb, nq),
        in_specs=[
            pl.BlockSpec((heads, tq, HEAD_DIM), lambda b, h, i: (h, b * nq + i, 0)),
            pl.BlockSpec((heads, S, HEAD_DIM), lambda b, h, i: (hb + h, b, 0)),
            pl.BlockSpec((heads, S, HEAD_DIM), lambda b, h, i: (2 * hb + h, b, 0)),
        ],
        out_specs=pl.BlockSpec((tq, heads * HEAD_DIM), lambda b, h, i: (b * nq + i, h)),
        out_shape=jax.ShapeDtypeStruct((B * S, H * HEAD_DIM), BF16),
        scratch_shapes=[pltpu.VMEM((heads, tq, HEAD_DIM), F32), pltpu.VMEM((heads, tq, BLOCK), F32)],
        compiler_params=_params(("parallel", "parallel", "arbitrary")),
        name="stick_breaking",
    )(qkv, qkv, qkv)


def _residue_major_permutation():
    i = np.arange(DIL_TILE)
    p = np.zeros((DIL_TILE, DIL_TILE), np.float32)
    p[i, (i % DIL_CHUNK) * DIL_RES + i // DIL_CHUNK] = 1.0
    return p


def _norm_matmul_perm_kernel(x_ref, g_ref, p_ref, w_ref, o_ref, hn_even_ref, hn_odd_ref, hnp_ref,
                             *, natural_tiles):
    i, j = pl.program_id(0), pl.program_id(1)

    def permute(hn_ref):
        for t in range(hn_ref.shape[0] // DIL_TILE):
            rows = slice(t * DIL_TILE, (t + 1) * DIL_TILE)
            hnp_ref[rows, :] = jnp.dot(p_ref[...], hn_ref[rows, :],
                                       preferred_element_type=F32).astype(hnp_ref.dtype)

    def project(lhs_ref):
        o_ref[...] = jnp.dot(lhs_ref[...], w_ref[...], preferred_element_type=F32).astype(o_ref.dtype)

    def row_step(filling_ref, finished_ref):
        @pl.when(j == 0)
        def _():
            permute(finished_ref)

        @pl.when(j < natural_tiles)
        def _():
            _norm_share(x_ref, g_ref, filling_ref, 0, j)
            project(finished_ref)

        @pl.when(j >= natural_tiles)
        def _():
            _norm_share(x_ref, g_ref, filling_ref, 0, j)
            project(hnp_ref)

    @pl.when(i == 0)
    def _():
        _norm_share(x_ref, g_ref, hn_even_ref, 0, j)

    @pl.when(i % 2 == 1)
    def _():
        row_step(hn_odd_ref, hn_even_ref)

    @pl.when(jnp.logical_and(i % 2 == 0, i > 0))
    def _():
        row_step(hn_even_ref, hn_odd_ref)


def _norm_matmul_perm(x, g, w, perm, natural_cols, tm, tn):
    M, K = x.shape
    N = w.shape[1]
    tiles, steps = M // tm, N // tn

    def col(i, j):
        return jnp.where(i > 0, j, 0)

    return pl.pallas_call(
        functools.partial(_norm_matmul_perm_kernel, natural_tiles=natural_cols // tn),
        grid=(tiles + 1, steps),
        in_specs=[
            _share_spec(tm, K, tiles, steps),
            pl.BlockSpec((1, K), lambda i, j: (0, 0)),
            pl.BlockSpec((DIL_TILE, DIL_TILE), lambda i, j: (0, 0)),
            pl.BlockSpec((K, tn), lambda i, j: (0, col(i, j))),
        ],
        out_specs=pl.BlockSpec((tm, tn), lambda i, j: (jnp.maximum(i - 1, 0), col(i, j))),
        out_shape=jax.ShapeDtypeStruct((M, N), BF16),
        scratch_shapes=[pltpu.VMEM((tm, K), BF16), pltpu.VMEM((tm, K), BF16), pltpu.VMEM((tm, K), BF16)],
        compiler_params=_params(("arbitrary", "arbitrary")),
        name="norm_matmul_perm",
    )(x, g.reshape(1, K), perm, w)


def _rel_bucket_table(dil, order):
    order = np.asarray(order)
    qi = order[:, None]
    kc = np.concatenate([order, BLOCK + order])[None, :]
    rel = BLOCK + qi - kc
    n = jnp.asarray(np.maximum(rel, 0) * dil)
    max_exact = REL_BUCKETS // 2
    nf = jnp.maximum(n, 1).astype(F32)
    large = max_exact + (jnp.log(nf / max_exact) / math.log(REL_MAX_DISTANCE / max_exact)
                         * (REL_BUCKETS - max_exact)).astype(jnp.int32)
    large = jnp.minimum(large, REL_BUCKETS - 1)
    bucket = jnp.where(n < max_exact, n, large).astype(jnp.int32)
    return jnp.where(jnp.asarray((rel >= 0) & (rel <= BLOCK)), bucket, -1)


class _NaturalRows:
    @staticmethod
    def load(ref, j, cols):
        return ref[j * BLOCK:(j + 1) * BLOCK, cols]

    @staticmethod
    def load_prev(ref, cols):
        return ref[:, cols]

    @staticmethod
    def store(ref, j, cols, val):
        ref[j * BLOCK:(j + 1) * BLOCK, cols] = val


class _Residue16Rows:
    per = BLOCK // DIL_CHUNK

    @classmethod
    def load(cls, ref, j, cols):
        return jnp.concatenate([ref[cls.per * j + t, :, cols] for t in range(cls.per)], axis=0)

    @classmethod
    def load_prev(cls, ref, cols):
        return cls.load(ref, 0, cols)

    @classmethod
    def store(cls, ref, j, cols, val):
        for t in range(cls.per):
            ref[cls.per * j + t, :, cols] = val[t * DIL_CHUNK:(t + 1) * DIL_CHUNK]


class _Residue4Rows:
    per = BLOCK // 4 // DIL_CHUNK
    order = [4 * m + a for a in range(4) for m in range(BLOCK // 4)]

    @classmethod
    def load(cls, ref, j, cols):
        return jnp.concatenate([ref[cls.per * j + t, a, :, cols] for a in range(4) for t in range(cls.per)],
                               axis=0)

    @classmethod
    def load_prev(cls, ref, cols):
        return cls.load(ref, 0, cols)

    @classmethod
    def store(cls, ref, j, cols, val):
        for a in range(4):
            for t in range(cls.per):
                piece = (a * cls.per + t) * DIL_CHUNK
                ref[cls.per * j + t, a, :, cols] = val[piece:piece + DIL_CHUNK]


def _dil_kernel(bucket_ref, table_ref, q_ref, kc_ref, kp_ref, vc_ref, vp_ref, o_ref, lse_ref, bias_ref,
                *, sub_blocks, rows):
    scale = HEAD_DIM ** -0.5
    first_step = jnp.logical_and(jnp.logical_and(pl.program_id(0) == 0, pl.program_id(1) == 0),
                                 pl.program_id(2) == 0)

    @pl.when(first_step)
    def _():
        bucket = bucket_ref[...]
        for h in range(DIL_GROUP_HEADS):
            b = jnp.full(bucket.shape, NEG_INF, F32)
            for t in range(REL_BUCKETS):
                b = jnp.where(bucket == t, table_ref[t, h], b)
            bias_ref[h] = b

    lane = lax.broadcasted_iota(jnp.int32, (BLOCK, 128), 1)
    keep_keys = jnp.logical_or(lax.broadcasted_iota(jnp.int32, (BLOCK, 2 * BLOCK), 1) >= BLOCK,
                               pl.program_id(2) > 0)
    ones = jnp.ones((2 * BLOCK, HEAD_DIM), BF16)
    contract_last = (((1,), (1,)), ((), ()))
    all_lanes = slice(None)

    def prev_and_cur(cur_ref, prev_ref, j, cols):
        prev = rows.load_prev(prev_ref, cols) if j == 0 else rows.load(cur_ref, j - 1, cols)
        return jnp.concatenate([prev, rows.load(cur_ref, j, cols)], axis=0)

    for j in range(sub_blocks):
        stats = jnp.zeros((BLOCK, 128), F32)
        for h in range(DIL_GROUP_HEADS):
            cols = slice(h * HEAD_DIM, (h + 1) * HEAD_DIM)
            q = rows.load(q_ref, j, cols)
            k = prev_and_cur(kc_ref, kp_ref, j, cols)
            v = prev_and_cur(vc_ref, vp_ref, j, cols)
            s = lax.dot_general(q, k, contract_last, preferred_element_type=F32) * scale + bias_ref[h]
            if j == 0:
                s = jnp.where(keep_keys, s, NEG_INF)
            m = jnp.max(s, axis=-1, keepdims=True)
            p = jnp.exp(s - m).astype(BF16)
            ol = jnp.dot(p, jnp.concatenate([v, ones], axis=1), preferred_element_type=F32)
            o, l = ol[:, :HEAD_DIM], ol[:, HEAD_DIM:]
            rows.store(o_ref, j, cols, (o / l).astype(o_ref.dtype))
            stats = jnp.where(lane == h, m + jnp.log(l), stats)
        rows.store(lse_ref, j, all_lanes, stats)


def _dilated_group(qkv, rel_table, B, S, g, dil):
    W, C = DIL_WIDTH, qkv.shape[1]
    tiles = S // DIL_TILE
    col0 = g * 3
    if dil == 1:
        rows, n_res = _NaturalRows, 1
        blocks = min(4, S // BLOCK)
        steps = S // (blocks * BLOCK)
        views = [(B * S, C), (B * S, W), (B * S, 128)]
        cur_shape, prev_shape = (blocks * BLOCK,), (BLOCK,)
        cur_idx = lambda b, r, i: (b * steps + i,)
        prev_idx = lambda b, r, i: (b * (S // BLOCK) + jnp.maximum(i * blocks - 1, 0),)
    else:
        rows = _Residue16Rows if dil == DIL_RES else _Residue4Rows
        assert dil in (4, DIL_RES) and DIL_RES == 16
        per = rows.per
        t_step = min(4 * per, tiles)
        blocks, steps = t_step // per, tiles // t_step
        if dil == DIL_RES:
            n_res = DIL_RES
            views = [(B * tiles, DIL_RES, DIL_CHUNK, c) for c in (C, W, 128)]
            cur_shape, prev_shape = (t_step, None, DIL_CHUNK), (per, None, DIL_CHUNK)
            cur_idx = lambda b, r, i: (b * steps + i, r, 0)
            prev_idx = lambda b, r, i: (b * (tiles // per) + jnp.maximum(i * blocks - 1, 0), r, 0)
        else:
            n_res = 4
            views = [(B * tiles, 4, 4, DIL_CHUNK, c) for c in (C, W, 128)]
            cur_shape, prev_shape = (t_step, 4, None, DIL_CHUNK), (per, 4, None, DIL_CHUNK)
            cur_idx = lambda b, r, i: (b * steps + i, 0, r, 0)
            prev_idx = lambda b, r, i: (b * (tiles // per) + jnp.maximum(i * blocks - 1, 0), 0, r, 0)

    def spec(shape, idx, width, col):
        return pl.BlockSpec(shape + (width,), lambda b, r, i: idx(b, r, i) + (col,))

    qkv_v = qkv.reshape(views[0])
    order = getattr(rows, "order", list(range(BLOCK)))
    o, lse = pl.pallas_call(
        functools.partial(_dil_kernel, sub_blocks=blocks, rows=rows),
        grid=(B, n_res, steps),
        in_specs=[
            pl.BlockSpec((BLOCK, 2 * BLOCK), lambda b, r, i: (0, 0)),
            pl.BlockSpec(memory_space=pltpu.SMEM),
            spec(cur_shape, cur_idx, W, col0),
            spec(cur_shape, cur_idx, W, col0 + 1), spec(prev_shape, prev_idx, W, col0 + 1),
            spec(cur_shape, cur_idx, W, col0 + 2), spec(prev_shape, prev_idx, W, col0 + 2),
        ],
        out_specs=[spec(cur_shape, cur_idx, W, 0), spec(cur_shape, cur_idx, 128, 0)],
        out_shape=[jax.ShapeDtypeStruct(views[1], BF16), jax.ShapeDtypeStruct(views[2], F32)],
        scratch_shapes=[pltpu.VMEM((DIL_GROUP_HEADS, BLOCK, 2 * BLOCK), F32)],
        compiler_params=_params(("arbitrary", "arbitrary", "arbitrary")),
        name=f"dilated_group{g}",
    )(_rel_bucket_table(dil, order), rel_table, qkv_v, qkv_v, qkv_v, qkv_v, qkv_v)
    return o.reshape(B * S, W), lse.reshape(B * S, 128)


def _bf16_pieces(x):
    hi = x.astype(BF16)
    rest = x - hi.astype(F32)
    mid = rest.astype(BF16)
    return [hi, mid, (rest - mid.astype(F32)).astype(BF16)]


def _combine_matmul_res_kernel(o0_ref, o1_ref, o2_ref, l0_ref, l1_ref, l2_ref, pt_ref, w_ref, r_ref, out_ref,
                               comb_ref):
    pt = pt_ref[...]
    for t in range(out_ref.shape[0] // DIL_TILE):
        rows = slice(t * DIL_TILE, (t + 1) * DIL_TILE)
        pieces = jnp.concatenate(_bf16_pieces(l1_ref[rows, :]) + _bf16_pieces(l2_ref[rows, :]), axis=1)
        lse = jnp.dot(pt, pieces, preferred_element_type=F32)
        l0 = l0_ref[rows, :]
        l1 = lse[:, 0:128] + lse[:, 128:256] + lse[:, 256:384]
        l2 = lse[:, 384:512] + lse[:, 512:640] + lse[:, 640:768]
        mx = jnp.maximum(jnp.maximum(l0, l1), l2)
        e0, e1, e2 = jnp.exp(l0 - mx), jnp.exp(l1 - mx), jnp.exp(l2 - mx)
        den = e0 + e1 + e2
        w0, w1, w2 = e0 / den, e1 / den, e2 / den
        o1 = jnp.dot(pt, o1_ref[rows, :], preferred_element_type=F32)
        o2 = jnp.dot(pt, o2_ref[rows, :], preferred_element_type=F32)
        for h in range(DIL_GROUP_HEADS):
            cols = slice(h * HEAD_DIM, (h + 1) * HEAD_DIM)
            c = (w0[:, h:h + 1] * o0_ref[rows, cols].astype(F32)
                 + w1[:, h:h + 1] * o1[:, cols] + w2[:, h:h + 1] * o2[:, cols])
            comb_ref[rows, cols] = c.astype(comb_ref.dtype)
    out_ref[...] = r_ref[...] + jnp.dot(comb_ref[...], w_ref[...], preferred_element_type=F32)


def _combine_matmul_res(os, lses, unperm, w, res, tm):
    M, K = os[0].shape
    N = w.shape[1]
    o_spec = pl.BlockSpec((tm, K), lambda i: (i, 0))
    l_spec = pl.BlockSpec((tm, 128), lambda i: (i, 0))
    row_block = pl.BlockSpec((tm, N), lambda i: (i, 0))
    return pl.pallas_call(
        _combine_matmul_res_kernel,
        grid=(M // tm,),
        in_specs=[o_spec, o_spec, o_spec, l_spec, l_spec, l_spec,
                  pl.BlockSpec((DIL_TILE, DIL_TILE), lambda i: (0, 0)),
                  pl.BlockSpec((K, N), lambda i: (0, 0), pipeline_mode=pl.Buffered(1)),
                  row_block],
        out_specs=row_block,
        out_shape=jax.ShapeDtypeStruct((M, N), F32),
        scratch_shapes=[pltpu.VMEM((tm, K), BF16)],
        compiler_params=_params(("parallel",)),
        name="combine_matmul_res",
    )(*os, *lses, unperm, w, res)


def _ffn_up_kernel(x_ref, g_ref, wg_ref, wv_ref, cw_ref, cb_ref, o_ref, hn_even_ref, hn_odd_ref,
                   *, tiles_per_seq):
    i, j = pl.program_id(0), pl.program_id(1)
    tm = hn_even_ref.shape[0] - CONV_HALO

    def gated(hn_ref):
        gate = jnp.dot(hn_ref[...], wg_ref[...], preferred_element_type=F32)
        val = jnp.dot(hn_ref[CONV_HALO:, :], wv_ref[...], preferred_element_type=F32)
        conv = cb_ref[...] + gate[CONV_HALO:] * cw_ref[CONV_WIDTH - 1:CONV_WIDTH, :]
        for back in range(1, CONV_WIDTH):
            tap = CONV_WIDTH - 1 - back
            conv = conv + pltpu.roll(gate, back, axis=0)[CONV_HALO:] * cw_ref[tap:tap + 1, :]
        act = conv / (1.0 + jnp.exp(-conv)) * val
        o_ref[...] = act.astype(o_ref.dtype)

    def row_step(filling_ref, finished_ref):
        tail = finished_ref[tm:tm + CONV_HALO, :]
        filling_ref[0:CONV_HALO, :] = jnp.where(i % tiles_per_seq == 0, jnp.zeros_like(tail), tail)
        _norm_share(x_ref, g_ref, filling_ref, CONV_HALO, j)
        gated(finished_ref)

    @pl.when(i == 0)
    def _():
        hn_even_ref[0:CONV_HALO, :] = jnp.zeros((CONV_HALO, hn_even_ref.shape[1]), hn_even_ref.dtype)
        _norm_share(x_ref, g_ref, hn_even_ref, CONV_HALO, j)

    @pl.when(i % 2 == 1)
    def _():
        row_step(hn_odd_ref, hn_even_ref)

    @pl.when(jnp.logical_and(i % 2 == 0, i > 0))
    def _():
        row_step(hn_even_ref, hn_odd_ref)


def _ffn_up(x, g, w_up, conv_w, conv_b, S, tm, tf):
    M, K = x.shape
    d_ff = w_up.shape[1] // 2
    tiles, steps = M // tm, d_ff // tf

    def col(i, j):
        return jnp.where(i > 0, j, 0)

    return pl.pallas_call(
        functools.partial(_ffn_up_kernel, tiles_per_seq=S // tm),
        grid=(tiles + 1, steps),
        in_specs=[
            _share_spec(tm, K, tiles, steps),
            pl.BlockSpec((1, K), lambda i, j: (0, 0)),
            pl.BlockSpec((K, tf), lambda i, j: (0, col(i, j))),
            pl.BlockSpec((K, tf), lambda i, j: (0, steps + col(i, j))),
            pl.BlockSpec((CONV_WIDTH, tf), lambda i, j: (0, col(i, j))),
            pl.BlockSpec((1, tf), lambda i, j: (0, col(i, j))),
        ],
        out_specs=pl.BlockSpec((tm, tf), lambda i, j: (jnp.maximum(i - 1, 0), col(i, j))),
        out_shape=jax.ShapeDtypeStruct((M, d_ff), BF16),
        scratch_shapes=[pltpu.VMEM((tm + CONV_HALO, K), BF16), pltpu.VMEM((tm + CONV_HALO, K), BF16)],
        compiler_params=_params(("arbitrary", "arbitrary")),
        name="ffn_up",
    )(x, g.reshape(1, K), w_up, w_up, conv_w, conv_b.reshape(1, d_ff))


def kernel(x, ln_mix, ln_ffn, ln_f, w_qkv_sb, w_o_sb, w_qkv_dil, w_o_dil, rel_bias, w_up, conv_w, conv_b, w_down):
    B, S, D = x.shape
    depth = ln_mix.shape[0]
    h = x.reshape(B * S, D)
    for i in range(depth):
        j = i // N_MIXERS
        if i % N_MIXERS == 0:
            qkv = _norm_matmul_heads(h, ln_mix[i], w_qkv_sb[j].astype(BF16), tm=1024, tn=1536)
            o = _stick_breaking(qkv, B, S, tq=min(2048, S), heads=2)
            h = _matmul_res(o, w_o_sb[j].astype(BF16), h, tm=512)
        else:
            perm = _residue_major_permutation()
            qkv = _norm_matmul_perm(h, ln_mix[i], w_qkv_dil[j].astype(BF16), jnp.asarray(perm, BF16),
                                    natural_cols=3 * DIL_WIDTH, tm=1024, tn=1536)
            os, lses = [], []
            for g, (window, dil) in enumerate(DIL_PATTERNS):
                assert window // dil == BLOCK and S % (dil * BLOCK) == 0
                table = rel_bias[:, g * DIL_GROUP_HEADS:(g + 1) * DIL_GROUP_HEADS]
                o, lse = _dilated_group(qkv, table, B, S, g, dil)
                os.append(o)
                lses.append(lse)
            h = _combine_matmul_res(os, lses, jnp.asarray(perm.T, BF16), w_o_dil[j].astype(BF16), h, tm=512)
        act = _ffn_up(h, ln_ffn[i], w_up[i].astype(BF16), conv_w[i], conv_b[i], S, tm=1024, tf=1024)
        h = _matmul_res(act, w_down[i].astype(BF16), h, tm=512, norm_gain=ln_f if i == depth - 1 else None)
    return h.reshape(B, S, D)
```

```python
import functools
import math

import jax
import jax.numpy as jnp
import numpy as np
from jax import lax
from jax.experimental import pallas as pl
from jax.experimental.pallas import tpu as pltpu

BLOCK = 128
SB_HEADS = 16
HEAD_DIM = 128
DIL_PATTERNS = ((128, 1), (512, 4), (2048, 16))
DIL_GROUPS = len(DIL_PATTERNS)
DIL_GROUP_HEADS = 8
DIL_WIDTH = DIL_GROUP_HEADS * HEAD_DIM
REL_BUCKETS = 32
REL_MAX_DISTANCE = 2048
CONV_WIDTH = 3
RMS_EPS = 1e-6
NEG_INF = -1e30
LOG2E = math.log2(math.e)
N_MIXERS = 2

SB_EXIT_LOG = -104.0
SB_STATIC_DEPTH = 3

BF16_ROWS = 16
DIL_RES = 16
DIL_CHUNK = BF16_ROWS
DIL_TILE = DIL_RES * DIL_CHUNK
CONV_HALO = BF16_ROWS
NORM_CHUNK = 64
V7X_VMEM_BYTES = 64 * 1024 * 1024
VMEM_LIMIT = V7X_VMEM_BYTES * 7 // 8

PROJ_ROWS = 1024
QKV_COLS = 1536
FFN_COLS = 1024
RES_ROWS = 512
SB_ROWS = 2048
SB_HEADS_PER_STEP = 2

BF16 = jnp.bfloat16
F32 = jnp.float32


def _params(semantics):
    return pltpu.CompilerParams(dimension_semantics=semantics, vmem_limit_bytes=VMEM_LIMIT)


def _rms_rows(x, g):
    ms = jnp.mean(x * x, axis=-1, keepdims=True)
    return x * lax.rsqrt(ms + RMS_EPS) * g


def _norm_share(x_ref, g_ref, dst_ref, dst_row0, step):
    rows = x_ref.shape[0]
    shares = (dst_ref.shape[0] - dst_row0) // rows
    base = dst_row0 + jnp.minimum(step, shares - 1) * rows
    g = g_ref[...]
    for c in range(rows // NORM_CHUNK):
        y = _rms_rows(x_ref[c * NORM_CHUNK:(c + 1) * NORM_CHUNK, :], g)
        row = pl.multiple_of(base + c * NORM_CHUNK, BF16_ROWS)
        dst_ref[pl.ds(row, NORM_CHUNK), :] = y.astype(dst_ref.dtype)


def _row_shares(steps):
    return 1 << (steps.bit_length() - 1)


def _share_spec(tm, K, tiles, steps):
    shares = _row_shares(steps)
    return pl.BlockSpec((tm // shares, K),
                        lambda i, j: (jnp.minimum(i, tiles - 1) * shares + jnp.minimum(j, shares - 1), 0))


def _norm_matmul_kernel(x_ref, g_ref, w_ref, o_ref, hn_even_ref, hn_odd_ref):
    i, j = pl.program_id(0), pl.program_id(1)

    def project(hn_ref):
        y = jnp.dot(hn_ref[...], w_ref[...], preferred_element_type=F32).astype(o_ref.dtype)
        for c in range(o_ref.shape[0]):
            o_ref[c] = y[:, c * HEAD_DIM:(c + 1) * HEAD_DIM]

    @pl.when(i == 0)
    def _():
        _norm_share(x_ref, g_ref, hn_even_ref, 0, j)

    @pl.when(i % 2 == 1)
    def _():
        _norm_share(x_ref, g_ref, hn_odd_ref, 0, j)
        project(hn_even_ref)

    @pl.when(jnp.logical_and(i % 2 == 0, i > 0))
    def _():
        _norm_share(x_ref, g_ref, hn_even_ref, 0, j)
        project(hn_odd_ref)


def _norm_matmul_heads(x, g, w, tm, tn):
    M, K = x.shape
    N = w.shape[1]
    slabs = tn // HEAD_DIM
    tiles, steps = M // tm, N // tn

    def col(i, j):
        return jnp.where(i > 0, j, 0)

    return pl.pallas_call(
        _norm_matmul_kernel,
        grid=(tiles + 1, steps),
        in_specs=[
            _share_spec(tm, K, tiles, steps),
            pl.BlockSpec((1, K), lambda i, j: (0, 0)),
            pl.BlockSpec((K, tn), lambda i, j: (0, col(i, j))),
        ],
        out_specs=pl.BlockSpec((slabs, tm, HEAD_DIM), lambda i, j: (col(i, j), jnp.maximum(i - 1, 0), 0)),
        out_shape=jax.ShapeDtypeStruct((N // HEAD_DIM, M, HEAD_DIM), BF16),
        scratch_shapes=[pltpu.VMEM((tm, K), BF16), pltpu.VMEM((tm, K), BF16)],
        compiler_params=_params(("arbitrary", "arbitrary")),
        name="norm_matmul",
    )(x, g.reshape(1, K), w)


def _matmul_res_kernel(a_ref, w_ref, r_ref, o_ref):
    o_ref[...] = r_ref[...] + jnp.dot(a_ref[...], w_ref[...], preferred_element_type=F32)


def _matmul_res_norm_kernel(a_ref, w_ref, r_ref, g_ref, o_ref):
    y = r_ref[...] + jnp.dot(a_ref[...], w_ref[...], preferred_element_type=F32)
    o_ref[...] = _rms_rows(y, g_ref[...])


def _matmul_res(a, w, res, tm, norm_gain=None):
    M, K = a.shape
    N = w.shape[1]
    in_specs = [
        pl.BlockSpec((tm, K), lambda i: (i, 0)),
        pl.BlockSpec((K, N), lambda i: (0, 0), pipeline_mode=pl.Buffered(1)),
        pl.BlockSpec((tm, N), lambda i: (i, 0)),
    ]
    args = [a, w, res]
    if norm_gain is not None:
        in_specs.append(pl.BlockSpec((1, N), lambda i: (0, 0)))
        args.append(norm_gain.reshape(1, N))
    return pl.pallas_call(
        _matmul_res_kernel if norm_gain is None else _matmul_res_norm_kernel,
        grid=(M // tm,),
        in_specs=in_specs,
        out_specs=pl.BlockSpec((tm, N), lambda i: (i, 0)),
        out_shape=jax.ShapeDtypeStruct((M, N), F32),
        compiler_params=_params(("parallel",)),
        name="matmul_res" if norm_gain is None else "matmul_res_norm",
    )(*args)


def _sb_kernel(q_ref, k_ref, v_ref, o_ref, acc_ref, spent_ref, *, chains, heads):
    scale = HEAD_DIM ** -0.5

    row = lax.broadcasted_iota(jnp.int32, (BLOCK, BLOCK), 0)
    col = lax.broadcasted_iota(jnp.int32, (BLOCK, BLOCK), 1)
    causal = col < row
    suffix = (row >= col).astype(BF16)
    base = pl.program_id(2) * chains
    contract_last = (((1,), (1,)), ((), ()))

    def mask_first_block(x):
        head = jnp.where(causal, x[:BLOCK], 0.0)
        return head if x.shape[0] == BLOCK else jnp.concatenate([head, x[BLOCK:]], axis=0)

    def key_block(h, rows, kb, on_diagonal):
        start = pl.multiple_of(kb * BLOCK, BLOCK)
        k = k_ref[h, pl.ds(start, BLOCK), :]
        v = v_ref[h, pl.ds(start, BLOCK), :]
        z = lax.dot_general(q_ref[h, rows, :], k, contract_last, preferred_element_type=F32) * scale
        drop = jnp.maximum(z, 0.0) + jnp.log(1.0 + jnp.exp2(jnp.abs(z) * -LOG2E))
        if on_diagonal:
            drop = mask_first_block(drop)
        upto = jnp.dot(drop.astype(BF16), suffix, preferred_element_type=F32)
        a = jnp.exp(z - (upto + spent_ref[h, rows, :]))
        if on_diagonal:
            a = mask_first_block(a)
        acc_ref[h, rows, :] += jnp.dot(a.astype(BF16), v, preferred_element_type=F32)
        spent_ref[h, rows, :] += upto[:, 0:1]

    def retire_if_no_keys(h, rows, kb):
        spent_ref[h, rows, :] = jnp.where(kb >= 0, spent_ref[h, rows, :], -NEG_INF)

    acc_ref[...] = jnp.zeros_like(acc_ref)
    spent_ref[...] = jnp.zeros_like(spent_ref)
    for first in range(chains - 1, -SB_STATIC_DEPTH, -1):
        lo, hi = max(first, 0), min(first + SB_STATIC_DEPTH - 1, chains - 1)
        rows = slice(lo * BLOCK, (hi + 1) * BLOCK)
        kb = base + first
        for h in range(heads):
            if first >= 0:
                key_block(h, rows, kb, True)
            else:
                retire_if_no_keys(h, rows, kb)
                key_block(h, rows, jnp.maximum(kb, 0), False)

    def cond(carry):
        _, least_spent = carry
        return least_spent < -SB_EXIT_LOG

    def body(carry):
        delta, _ = carry
        for c in range(chains):
            rows = slice(c * BLOCK, (c + 1) * BLOCK)
            kb = base + c - delta
            for h in range(heads):
                retire_if_no_keys(h, rows, kb)
                key_block(h, rows, jnp.maximum(kb, 0), False)
        return delta + 1, jnp.min(spent_ref[...])

    lax.while_loop(cond, body, (jnp.int32(SB_STATIC_DEPTH), jnp.min(spent_ref[...])))
    for h in range(heads):
        o_ref[:, h * HEAD_DIM:(h + 1) * HEAD_DIM] = acc_ref[h].astype(o_ref.dtype)


def _stick_breaking(qkv, B, S, tq, heads):
    H = SB_HEADS
    nq = S // tq
    hb = H // heads
    return pl.pallas_call(
        functools.partial(_sb_kernel, chains=tq // BLOCK, heads=heads),
        grid=(B, hb, nq),
        in_specs=[
            pl.BlockSpec((heads, tq, HEAD_DIM), lambda b, h, i: (h, b * nq + i, 0)),
            pl.BlockSpec((heads, S, HEAD_DIM), lambda b, h, i: (hb + h, b, 0)),
            pl.BlockSpec((heads, S, HEAD_DIM), lambda b, h, i: (2 * hb + h, b, 0)),
        ],
        out_specs=pl.BlockSpec((tq, heads * HEAD_DIM), lambda b, h, i: (b * nq + i, h)),
        out_shape=jax.ShapeDtypeStruct((B * S, H * HEAD_DIM), BF16),
        scratch_shapes=[pltpu.VMEM((heads, tq, HEAD_DIM), F32), pltpu.VMEM((heads, tq, BLOCK), F32)],
        compiler_params=_params(("parallel", "parallel", "arbitrary")),
        name="stick_breaking",
    )(qkv, qkv, qkv)


def _residue_major_permutation():
    i = np.arange(DIL_TILE)
    p = np.zeros((DIL_TILE, DIL_TILE), np.float32)
    p[i, (i % DIL_CHUNK) * DIL_RES + i // DIL_CHUNK] = 1.0
    return p


def _norm_matmul_perm_kernel(x_ref, g_ref, p_ref, w_ref, o_ref, hn_even_ref, hn_odd_ref, hnp_ref,
                             *, natural_tiles):
    i, j = pl.program_id(0), pl.program_id(1)

    def permute(hn_ref):
        for t in range(hn_ref.shape[0] // DIL_TILE):
            rows = slice(t * DIL_TILE, (t + 1) * DIL_TILE)
            hnp_ref[rows, :] = jnp.dot(p_ref[...], hn_ref[rows, :],
                                       preferred_element_type=F32).astype(hnp_ref.dtype)

    def project(lhs_ref):
        o_ref[...] = jnp.dot(lhs_ref[...], w_ref[...], preferred_element_type=F32).astype(o_ref.dtype)

    def row_step(filling_ref, finished_ref):
        @pl.when(j == 0)
        def _():
            permute(finished_ref)

        @pl.when(j < natural_tiles)
        def _():
            _norm_share(x_ref, g_ref, filling_ref, 0, j)
            project(finished_ref)

        @pl.when(j >= natural_tiles)
        def _():
            _norm_share(x_ref, g_ref, filling_ref, 0, j)
            project(hnp_ref)

    @pl.when(i == 0)
    def _():
        _norm_share(x_ref, g_ref, hn_even_ref, 0, j)

    @pl.when(i % 2 == 1)
    def _():
        row_step(hn_odd_ref, hn_even_ref)

    @pl.when(jnp.logical_and(i % 2 == 0, i > 0))
    def _():
        row_step(hn_even_ref, hn_odd_ref)


def _norm_matmul_perm(x, g, w, perm, natural_cols, tm, tn):
    M, K = x.shape
    N = w.shape[1]
    tiles, steps = M // tm, N // tn

    def col(i, j):
        return jnp.where(i > 0, j, 0)

    return pl.pallas_call(
        functools.partial(_norm_matmul_perm_kernel, natural_tiles=natural_cols // tn),
        grid=(tiles + 1, steps),
        in_specs=[
            _share_spec(tm, K, tiles, steps),
            pl.BlockSpec((1, K), lambda i, j: (0, 0)),
            pl.BlockSpec((DIL_TILE, DIL_TILE), lambda i, j: (0, 0)),
            pl.BlockSpec((K, tn), lambda i, j: (0, col(i, j))),
        ],
        out_specs=pl.BlockSpec((tm, tn), lambda i, j: (jnp.maximum(i - 1, 0), col(i, j))),
        out_shape=jax.ShapeDtypeStruct((M, N), BF16),
        scratch_shapes=[pltpu.VMEM((tm, K), BF16), pltpu.VMEM((tm, K), BF16), pltpu.VMEM((tm, K), BF16)],
        compiler_params=_params(("arbitrary", "arbitrary")),
        name="norm_matmul_perm",
    )(x, g.reshape(1, K), perm, w)


def _rel_bucket_table(dil, order):
    order = np.asarray(order)
    qi = order[:, None]
    kc = np.concatenate([order, BLOCK + order])[None, :]
    rel = BLOCK + qi - kc
    n = jnp.asarray(np.maximum(rel, 0) * dil)
    max_exact = REL_BUCKETS // 2
    nf = jnp.maximum(n, 1).astype(F32)
    large = max_exact + (jnp.log(nf / max_exact) / math.log(REL_MAX_DISTANCE / max_exact)
                         * (REL_BUCKETS - max_exact)).astype(jnp.int32)
    large = jnp.minimum(large, REL_BUCKETS - 1)
    bucket = jnp.where(n < max_exact, n, large).astype(jnp.int32)
    return jnp.where(jnp.asarray((rel >= 0) & (rel <= BLOCK)), bucket, -1)


class _NaturalRows:
    @staticmethod
    def load(ref, j, cols):
        return ref[j * BLOCK:(j + 1) * BLOCK, cols]

    @staticmethod
    def load_prev(ref, cols):
        return ref[:, cols]

    @staticmethod
    def store(ref, j, cols, val):
        ref[j * BLOCK:(j + 1) * BLOCK, cols] = val


class _Residue16Rows:
    per = BLOCK // DIL_CHUNK

    @classmethod
    def load(cls, ref, j, cols):
        return jnp.concatenate([ref[cls.per * j + t, :, cols] for t in range(cls.per)], axis=0)

    @classmethod
    def load_prev(cls, ref, cols):
        return cls.load(ref, 0, cols)

    @classmethod
    def store(cls, ref, j, cols, val):
        for t in range(cls.per):
            ref[cls.per * j + t, :, cols] = val[t * DIL_CHUNK:(t + 1) * DIL_CHUNK]


class _Residue4Rows:
    per = BLOCK // 4 // DIL_CHUNK
    order = [4 * m + a for a in range(4) for m in range(BLOCK // 4)]

    @classmethod
    def load(cls, ref, j, cols):
        return jnp.concatenate([ref[cls.per * j + t, a, :, cols] for a in range(4) for t in range(cls.per)],
                               axis=0)

    @classmethod
    def load_prev(cls, ref, cols):
        return cls.load(ref, 0, cols)

    @classmethod
    def store(cls, ref, j, cols, val):
        for a in range(4):
            for t in range(cls.per):
                piece = (a * cls.per + t) * DIL_CHUNK
                ref[cls.per * j + t, a, :, cols] = val[piece:piece + DIL_CHUNK]


def _dil_kernel(bucket_ref, table_ref, q_ref, kc_ref, kp_ref, vc_ref, vp_ref, o_ref, lse_ref, bias_ref,
                *, sub_blocks, rows):
    scale = HEAD_DIM ** -0.5
    first_step = jnp.logical_and(jnp.logical_and(pl.program_id(0) == 0, pl.program_id(1) == 0),
                                 pl.program_id(2) == 0)

    @pl.when(first_step)
    def _():
        bucket = bucket_ref[...]
        for h in range(DIL_GROUP_HEADS):
            b = jnp.full(bucket.shape, NEG_INF, F32)
            for t in range(REL_BUCKETS):
                b = jnp.where(bucket == t, table_ref[t, h], b)
            bias_ref[h] = b

    lane = lax.broadcasted_iota(jnp.int32, (BLOCK, 128), 1)
    keep_keys = jnp.logical_or(lax.broadcasted_iota(jnp.int32, (BLOCK, 2 * BLOCK), 1) >= BLOCK,
                               pl.program_id(2) > 0)
    ones = jnp.ones((2 * BLOCK, HEAD_DIM), BF16)
    contract_last = (((1,), (1,)), ((), ()))
    all_lanes = slice(None)

    def prev_and_cur(cur_ref, prev_ref, j, cols):
        prev = rows.load_prev(prev_ref, cols) if j == 0 else rows.load(cur_ref, j - 1, cols)
        return jnp.concatenate([prev, rows.load(cur_ref, j, cols)], axis=0)

    for j in range(sub_blocks):
        stats = jnp.zeros((BLOCK, 128), F32)
        for h in range(DIL_GROUP_HEADS):
            cols = slice(h * HEAD_DIM, (h + 1) * HEAD_DIM)
            q = rows.load(q_ref, j, cols)
            k = prev_and_cur(kc_ref, kp_ref, j, cols)
            v = prev_and_cur(vc_ref, vp_ref, j, cols)
            s = lax.dot_general(q, k, contract_last, preferred_element_type=F32) * scale + bias_ref[h]
            if j == 0:
                s = jnp.where(keep_keys, s, NEG_INF)
            m = jnp.max(s, axis=-1, keepdims=True)
            p = jnp.exp(s - m).astype(BF16)
            ol = jnp.dot(p, jnp.concatenate([v, ones], axis=1), preferred_element_type=F32)
            o, l = ol[:, :HEAD_DIM], ol[:, HEAD_DIM:]
            rows.store(o_ref, j, cols, (o / l).astype(o_ref.dtype))
            stats = jnp.where(lane == h, m + jnp.log(l), stats)
        rows.store(lse_ref, j, all_lanes, stats)


def _dilated_group(qkv, rel_table, B, S, g, dil):
    W, C = DIL_WIDTH, qkv.shape[1]
    tiles = S // DIL_TILE
    col0 = g * 3
    if dil == 1:
        rows, n_res = _NaturalRows, 1
        blocks = min(4, S // BLOCK)
        steps = S // (blocks * BLOCK)
        views = [(B * S, C), (B * S, W), (B * S, 128)]
        cur_shape, prev_shape = (blocks * BLOCK,), (BLOCK,)
        cur_idx = lambda b, r, i: (b * steps + i,)
        prev_idx = lambda b, r, i: (b * (S // BLOCK) + jnp.maximum(i * blocks - 1, 0),)
    else:
        rows = _Residue16Rows if dil == DIL_RES else _Residue4Rows
        assert dil in (4, DIL_RES) and DIL_RES == 16
        per = rows.per
        t_step = min(4 * per, tiles)
        blocks, steps = t_step // per, tiles // t_step
        if dil == DIL_RES:
            n_res = DIL_RES
            views = [(B * tiles, DIL_RES, DIL_CHUNK, c) for c in (C, W, 128)]
            cur_shape, prev_shape = (t_step, None, DIL_CHUNK), (per, None, DIL_CHUNK)
            cur_idx = lambda b, r, i: (b * steps + i, r, 0)
            prev_idx = lambda b, r, i: (b * (tiles // per) + jnp.maximum(i * blocks - 1, 0), r, 0)
        else:
            n_res = 4
            views = [(B * tiles, 4, 4, DIL_CHUNK, c) for c in (C, W, 128)]
            cur_shape, prev_shape = (t_step, 4, None, DIL_CHUNK), (per, 4, None, DIL_CHUNK)
            cur_idx = lambda b, r, i: (b * steps + i, 0, r, 0)
            prev_idx = lambda b, r, i: (b * (tiles // per) + jnp.maximum(i * blocks - 1, 0), 0, r, 0)

    def spec(shape, idx, width, col):
        return pl.BlockSpec(shape + (width,), lambda b, r, i: idx(b, r, i) + (col,))

    qkv_v = qkv.reshape(views[0])
    order = getattr(rows, "order", list(range(BLOCK)))
    o, lse = pl.pallas_call(
        functools.partial(_dil_kernel, sub_blocks=blocks, rows=rows),
        grid=(B, n_res, steps),
        in_specs=[
            pl.BlockSpec((BLOCK, 2 * BLOCK), lambda b, r, i: (0, 0)),
            pl.BlockSpec(memory_space=pltpu.SMEM),
            spec(cur_shape, cur_idx, W, col0),
            spec(cur_shape, cur_idx, W, col0 + 1), spec(prev_shape, prev_idx, W, col0 + 1),
            spec(cur_shape, cur_idx, W, col0 + 2), spec(prev_shape, prev_idx, W, col0 + 2),
        ],
        out_specs=[spec(cur_shape, cur_idx, W, 0), spec(cur_shape, cur_idx, 128, 0)],
        out_shape=[jax.ShapeDtypeStruct(views[1], BF16), jax.ShapeDtypeStruct(views[2], F32)],
        scratch_shapes=[pltpu.VMEM((DIL_GROUP_HEADS, BLOCK, 2 * BLOCK), F32)],
        compiler_params=_params(("arbitrary", "arbitrary", "arbitrary")),
        name=f"dilated_group{g}",
    )(_rel_bucket_table(dil, order), rel_table, qkv_v, qkv_v, qkv_v, qkv_v, qkv_v)
    return o.reshape(B * S, W), lse.reshape(B * S, 128)


def _bf16_pieces(x):
    hi = x.astype(BF16)
    rest = x - hi.astype(F32)
    mid = rest.astype(BF16)
    return [hi, mid, (rest - mid.astype(F32)).astype(BF16)]


def _combine_matmul_res_kernel(o0_ref, o1_ref, o2_ref, l0_ref, l1_ref, l2_ref, pt_ref, w_ref, r_ref, out_ref,
                               comb_ref):
    pt = pt_ref[...]
    for t in range(out_ref.shape[0] // DIL_TILE):
        rows = slice(t * DIL_TILE, (t + 1) * DIL_TILE)
        pieces = jnp.concatenate(_bf16_pieces(l1_ref[rows, :]) + _bf16_pieces(l2_ref[rows, :]), axis=1)
        lse = jnp.dot(pt, pieces, preferred_element_type=F32)
        l0 = l0_ref[rows, :]
        l1 = lse[:, 0:128] + lse[:, 128:256] + lse[:, 256:384]
        l2 = lse[:, 384:512] + lse[:, 512:640] + lse[:, 640:768]
        mx = jnp.maximum(jnp.maximum(l0, l1), l2)
        e0, e1, e2 = jnp.exp(l0 - mx), jnp.exp(l1 - mx), jnp.exp(l2 - mx)
        den = e0 + e1 + e2
        w0, w1, w2 = e0 / den, e1 / den, e2 / den
        o1 = jnp.dot(pt, o1_ref[rows, :], preferred_element_type=F32)
        o2 = jnp.dot(pt, o2_ref[rows, :], preferred_element_type=F32)
        for h in range(DIL_GROUP_HEADS):
            cols = slice(h * HEAD_DIM, (h + 1) * HEAD_DIM)
            c = (w0[:, h:h + 1] * o0_ref[rows, cols].astype(F32)
                 + w1[:, h:h + 1] * o1[:, cols] + w2[:, h:h + 1] * o2[:, cols])
            comb_ref[rows, cols] = c.astype(comb_ref.dtype)
    out_ref[...] = r_ref[...] + jnp.dot(comb_ref[...], w_ref[...], preferred_element_type=F32)


def _combine_matmul_res(os, lses, unperm, w, res, tm):
    M, K = os[0].shape
    N = w.shape[1]
    o_spec = pl.BlockSpec((tm, K), lambda i: (i, 0))
    l_spec = pl.BlockSpec((tm, 128), lambda i: (i, 0))
    row_block = pl.BlockSpec((tm, N), lambda i: (i, 0))
    return pl.pallas_call(
        _combine_matmul_res_kernel,
        grid=(M // tm,),
        in_specs=[o_spec, o_spec, o_spec, l_spec, l_spec, l_spec,
                  pl.BlockSpec((DIL_TILE, DIL_TILE), lambda i: (0, 0)),
                  pl.BlockSpec((K, N), lambda i: (0, 0), pipeline_mode=pl.Buffered(1)),
                  row_block],
        out_specs=row_block,
        out_shape=jax.ShapeDtypeStruct((M, N), F32),
        scratch_shapes=[pltpu.VMEM((tm, K), BF16)],
        compiler_params=_params(("parallel",)),
        name="combine_matmul_res",
    )(*os, *lses, unperm, w, res)


def _ffn_up_kernel(x_ref, g_ref, wg_ref, wv_ref, cw_ref, cb_ref, o_ref, hn_even_ref, hn_odd_ref,
                   *, tiles_per_seq):
    i, j = pl.program_id(0), pl.program_id(1)
    tm = hn_even_ref.shape[0] - CONV_HALO

    def gated(hn_ref):
        gate = jnp.dot(hn_ref[...], wg_ref[...], preferred_element_type=F32)
        val = jnp.dot(hn_ref[CONV_HALO:, :], wv_ref[...], preferred_element_type=F32)
        conv = cb_ref[...] + gate[CONV_HALO:] * cw_ref[CONV_WIDTH - 1:CONV_WIDTH, :]
        for back in range(1, CONV_WIDTH):
            tap = CONV_WIDTH - 1 - back
            conv = conv + pltpu.roll(gate, back, axis=0)[CONV_HALO:] * cw_ref[tap:tap + 1, :]
        act = conv / (1.0 + jnp.exp(-conv)) * val
        o_ref[...] = act.astype(o_ref.dtype)

    def row_step(filling_ref, finished_ref):
        tail = finished_ref[tm:tm + CONV_HALO, :]
        filling_ref[0:CONV_HALO, :] = jnp.where(i % tiles_per_seq == 0, jnp.zeros_like(tail), tail)
        _norm_share(x_ref, g_ref, filling_ref, CONV_HALO, j)
        gated(finished_ref)

    @pl.when(i == 0)
    def _():
        hn_even_ref[0:CONV_HALO, :] = jnp.zeros((CONV_HALO, hn_even_ref.shape[1]), hn_even_ref.dtype)
        _norm_share(x_ref, g_ref, hn_even_ref, CONV_HALO, j)

    @pl.when(i % 2 == 1)
    def _():
        row_step(hn_odd_ref, hn_even_ref)

    @pl.when(jnp.logical_and(i % 2 == 0, i > 0))
    def _():
        row_step(hn_even_ref, hn_odd_ref)


def _ffn_up(x, g, w_up, conv_w, conv_b, S, tm, tf):
    M, K = x.shape
    d_ff = w_up.shape[1] // 2
    tiles, steps = M // tm, d_ff // tf

    def col(i, j):
        return jnp.where(i > 0, j, 0)

    return pl.pallas_call(
        functools.partial(_ffn_up_kernel, tiles_per_seq=S // tm),
        grid=(tiles + 1, steps),
        in_specs=[
            _share_spec(tm, K, tiles, steps),
            pl.BlockSpec((1, K), lambda i, j: (0, 0)),
            pl.BlockSpec((K, tf), lambda i, j: (0, col(i, j))),
            pl.BlockSpec((K, tf), lambda i, j: (0, steps + col(i, j))),
            pl.BlockSpec((CONV_WIDTH, tf), lambda i, j: (0, col(i, j))),
            pl.BlockSpec((1, tf), lambda i, j: (0, col(i, j))),
        ],
        out_specs=pl.BlockSpec((tm, tf), lambda i, j: (jnp.maximum(i - 1, 0), col(i, j))),
        out_shape=jax.ShapeDtypeStruct((M, d_ff), BF16),
        scratch_shapes=[pltpu.VMEM((tm + CONV_HALO, K), BF16), pltpu.VMEM((tm + CONV_HALO, K), BF16)],
        compiler_params=_params(("arbitrary", "arbitrary")),
        name="ffn_up",
    )(x, g.reshape(1, K), w_up, w_up, conv_w, conv_b.reshape(1, d_ff))


def kernel(x, ln_mix, ln_ffn, ln_f, w_qkv_sb, w_o_sb, w_qkv_dil, w_o_dil, rel_bias, w_up, conv_w, conv_b, w_down):
    B, S, D = x.shape
    depth = ln_mix.shape[0]
    assert S % PROJ_ROWS == 0 and S % min(SB_ROWS, S) == 0 and S % (DIL_RES * BLOCK) == 0
    h = x.reshape(B * S, D)
    for i in range(depth):
        j = i // N_MIXERS
        if i % N_MIXERS == 0:
            qkv = _norm_matmul_heads(h, ln_mix[i], w_qkv_sb[j].astype(BF16), tm=PROJ_ROWS, tn=QKV_COLS)
            o = _stick_breaking(qkv, B, S, tq=min(SB_ROWS, S), heads=SB_HEADS_PER_STEP)
            h = _matmul_res(o, w_o_sb[j].astype(BF16), h, tm=RES_ROWS)
        else:
            perm = _residue_major_permutation()
            qkv = _norm_matmul_perm(h, ln_mix[i], w_qkv_dil[j].astype(BF16), jnp.asarray(perm, BF16),
                                    natural_cols=3 * DIL_WIDTH, tm=PROJ_ROWS, tn=QKV_COLS)
            os, lses = [], []
            for g, (window, dil) in enumerate(DIL_PATTERNS):
                assert window // dil == BLOCK and S % (dil * BLOCK) == 0
                table = rel_bias[:, g * DIL_GROUP_HEADS:(g + 1) * DIL_GROUP_HEADS]
                o, lse = _dilated_group(qkv, table, B, S, g, dil)
                os.append(o)
                lses.append(lse)
            h = _combine_matmul_res(os, lses, jnp.asarray(perm.T, BF16), w_o_dil[j].astype(BF16), h, tm=RES_ROWS)
        act = _ffn_up(h, ln_ffn[i], w_up[i].astype(BF16), conv_w[i], conv_b[i], S, tm=PROJ_ROWS, tf=FFN_COLS)
        h = _matmul_res(act, w_down[i].astype(BF16), h, tm=RES_ROWS, norm_gain=ln_f if i == depth - 1 else None)
    return h.reshape(B, S, D)
```

```python
import functools
import math

import jax
import jax.numpy as jnp
import numpy as np
from jax import lax
from jax.experimental import pallas as pl
from jax.experimental.pallas import tpu as pltpu

BLOCK = 128
SB_HEADS = 16
HEAD_DIM = 128
DIL_PATTERNS = ((128, 1), (512, 4), (2048, 16))
DIL_GROUPS = len(DIL_PATTERNS)
DIL_GROUP_HEADS = 8
DIL_WIDTH = DIL_GROUP_HEADS * HEAD_DIM
REL_BUCKETS = 32
REL_MAX_DISTANCE = 2048
CONV_WIDTH = 3
RMS_EPS = 1e-6
NEG_INF = -1e30
LOG2E = math.log2(math.e)
N_MIXERS = 2

SB_EXIT_LOG = -104.0
SB_STATIC_DEPTH = 3

BF16_ROWS = 16
DIL_RES = 16
DIL_CHUNK = BF16_ROWS
DIL_TILE = DIL_RES * DIL_CHUNK
CONV_HALO = BF16_ROWS
NORM_CHUNK = 64
V7X_VMEM_BYTES = 64 * 1024 * 1024
VMEM_LIMIT = V7X_VMEM_BYTES * 7 // 8

PROJ_ROWS = 1024
QKV_COLS = 1536
FFN_COLS = 1024
RES_ROWS = 512
SB_ROWS = 2048
SB_HEADS_PER_STEP = 2

BF16 = jnp.bfloat16
F32 = jnp.float32


def _params(semantics):
    return pltpu.CompilerParams(dimension_semantics=semantics, vmem_limit_bytes=VMEM_LIMIT)


def _rms_rows(x, g):
    ms = jnp.mean(x * x, axis=-1, keepdims=True)
    return x * lax.rsqrt(ms + RMS_EPS) * g


def _norm_share(x_ref, g_ref, dst_ref, dst_row0, step):
    rows = x_ref.shape[0]
    shares = (dst_ref.shape[0] - dst_row0) // rows
    base = dst_row0 + jnp.minimum(step, shares - 1) * rows
    g = g_ref[...]
    for c in range(rows // NORM_CHUNK):
        y = _rms_rows(x_ref[c * NORM_CHUNK:(c + 1) * NORM_CHUNK, :], g)
        row = pl.multiple_of(base + c * NORM_CHUNK, BF16_ROWS)
        dst_ref[pl.ds(row, NORM_CHUNK), :] = y.astype(dst_ref.dtype)


def _row_shares(steps):
    return 1 << (steps.bit_length() - 1)


def _share_spec(tm, K, tiles, steps):
    shares = _row_shares(steps)
    return pl.BlockSpec((tm // shares, K),
                        lambda i, j: (jnp.minimum(i, tiles - 1) * shares + jnp.minimum(j, shares - 1), 0))


def _norm_matmul_kernel(x_ref, g_ref, w_ref, cs_ref, o_ref, hn_even_ref, hn_odd_ref):
    i, j = pl.program_id(0), pl.program_id(1)

    def project(hn_ref):
        y = (jnp.dot(hn_ref[...], w_ref[...], preferred_element_type=F32) * cs_ref[...]).astype(o_ref.dtype)
        for c in range(o_ref.shape[0]):
            o_ref[c] = y[:, c * HEAD_DIM:(c + 1) * HEAD_DIM]

    @pl.when(i == 0)
    def _():
        _norm_share(x_ref, g_ref, hn_even_ref, 0, j)

    @pl.when(i % 2 == 1)
    def _():
        _norm_share(x_ref, g_ref, hn_odd_ref, 0, j)
        project(hn_even_ref)

    @pl.when(jnp.logical_and(i % 2 == 0, i > 0))
    def _():
        _norm_share(x_ref, g_ref, hn_even_ref, 0, j)
        project(hn_odd_ref)


def _norm_matmul_heads(x, g, w, col_scale, tm, tn):
    M, K = x.shape
    N = w.shape[1]
    slabs = tn // HEAD_DIM
    tiles, steps = M // tm, N // tn

    def col(i, j):
        return jnp.where(i > 0, j, 0)

    return pl.pallas_call(
        _norm_matmul_kernel,
        grid=(tiles + 1, steps),
        in_specs=[
            _share_spec(tm, K, tiles, steps),
            pl.BlockSpec((1, K), lambda i, j: (0, 0)),
            pl.BlockSpec((K, tn), lambda i, j: (0, col(i, j))),
            pl.BlockSpec((1, tn), lambda i, j: (0, col(i, j))),
        ],
        out_specs=pl.BlockSpec((slabs, tm, HEAD_DIM), lambda i, j: (col(i, j), jnp.maximum(i - 1, 0), 0)),
        out_shape=jax.ShapeDtypeStruct((N // HEAD_DIM, M, HEAD_DIM), BF16),
        scratch_shapes=[pltpu.VMEM((tm, K), BF16), pltpu.VMEM((tm, K), BF16)],
        compiler_params=_params(("arbitrary", "arbitrary")),
        name="norm_matmul",
    )(x, g.reshape(1, K), w, col_scale.reshape(1, N))


def _matmul_res_kernel(a_ref, w_ref, r_ref, o_ref):
    o_ref[...] = r_ref[...] + jnp.dot(a_ref[...], w_ref[...], preferred_element_type=F32)


def _matmul_res_norm_kernel(a_ref, w_ref, r_ref, g_ref, o_ref):
    y = r_ref[...] + jnp.dot(a_ref[...], w_ref[...], preferred_element_type=F32)
    o_ref[...] = _rms_rows(y, g_ref[...])


def _matmul_res(a, w, res, tm, norm_gain=None):
    M, K = a.shape
    N = w.shape[1]
    in_specs = [
        pl.BlockSpec((tm, K), lambda i: (i, 0)),
        pl.BlockSpec((K, N), lambda i: (0, 0), pipeline_mode=pl.Buffered(1)),
        pl.BlockSpec((tm, N), lambda i: (i, 0)),
    ]
    args = [a, w, res]
    if norm_gain is not None:
        in_specs.append(pl.BlockSpec((1, N), lambda i: (0, 0)))
        args.append(norm_gain.reshape(1, N))
    return pl.pallas_call(
        _matmul_res_kernel if norm_gain is None else _matmul_res_norm_kernel,
        grid=(M // tm,),
        in_specs=in_specs,
        out_specs=pl.BlockSpec((tm, N), lambda i: (i, 0)),
        out_shape=jax.ShapeDtypeStruct((M, N), F32),
        compiler_params=_params(("parallel",)),
        name="matmul_res" if norm_gain is None else "matmul_res_norm",
    )(*args)


def _sb_kernel(q_ref, k_ref, v_ref, o_ref, acc_ref, spent_ref, *, chains, heads):
    row = lax.broadcasted_iota(jnp.int32, (BLOCK, BLOCK), 0)
    col = lax.broadcasted_iota(jnp.int32, (BLOCK, BLOCK), 1)
    causal = col < row
    suffix = (row >= col).astype(BF16)
    base = pl.program_id(2) * chains
    contract_last = (((1,), (1,)), ((), ()))

    def mask_first_block(x):
        head = jnp.where(causal, x[:BLOCK], 0.0)
        return head if x.shape[0] == BLOCK else jnp.concatenate([head, x[BLOCK:]], axis=0)

    def key_block(h, rows, kb, on_diagonal):
        start = pl.multiple_of(kb * BLOCK, BLOCK)
        k = k_ref[h, pl.ds(start, BLOCK), :]
        v = v_ref[h, pl.ds(start, BLOCK), :]
        z = lax.dot_general(q_ref[h, rows, :], k, contract_last, preferred_element_type=F32)
        drop = jnp.maximum(z, 0.0) + jnp.log(1.0 + jnp.exp2(jnp.abs(z) * -LOG2E))
        if on_diagonal:
            drop = mask_first_block(drop)
        upto = jnp.dot(drop.astype(BF16), suffix, preferred_element_type=F32)
        a = jnp.exp(z - (upto + spent_ref[h, rows, :]))
        if on_diagonal:
            a = mask_first_block(a)
        acc_ref[h, rows, :] += jnp.dot(a.astype(BF16), v, preferred_element_type=F32)
        spent_ref[h, rows, :] += upto[:, 0:1]

    def retire_if_no_keys(h, rows, kb):
        spent_ref[h, rows, :] = jnp.where(kb >= 0, spent_ref[h, rows, :], -NEG_INF)

    acc_ref[...] = jnp.zeros_like(acc_ref)
    spent_ref[...] = jnp.zeros_like(spent_ref)
    for first in range(chains - 1, -SB_STATIC_DEPTH, -1):
        lo, hi = max(first, 0), min(first + SB_STATIC_DEPTH - 1, chains - 1)
        rows = slice(lo * BLOCK, (hi + 1) * BLOCK)
        kb = base + first
        for h in range(heads):
            if first >= 0:
                key_block(h, rows, kb, True)
            else:
                retire_if_no_keys(h, rows, kb)
                key_block(h, rows, jnp.maximum(kb, 0), False)

    def cond(carry):
        _, least_spent = carry
        return least_spent < -SB_EXIT_LOG

    def body(carry):
        delta, _ = carry
        for c in range(chains):
            rows = slice(c * BLOCK, (c + 1) * BLOCK)
            kb = base + c - delta
            for h in range(heads):
                retire_if_no_keys(h, rows, kb)
                key_block(h, rows, jnp.maximum(kb, 0), False)
        return delta + 1, jnp.min(spent_ref[...])

    lax.while_loop(cond, body, (jnp.int32(SB_STATIC_DEPTH), jnp.min(spent_ref[...])))
    for h in range(heads):
        o_ref[:, h * HEAD_DIM:(h + 1) * HEAD_DIM] = acc_ref[h].astype(o_ref.dtype)


def _stick_breaking(qkv, B, S, tq, heads):
    H = SB_HEADS
    nq = S // tq
    hb = H // heads
    return pl.pallas_call(
        functools.partial(_sb_kernel, chains=tq // BLOCK, heads=heads),
        grid=(B, hb, nq),
        in_specs=[
            pl.BlockSpec((heads, tq, HEAD_DIM), lambda b, h, i: (h, b * nq + i, 0)),
            pl.BlockSpec((heads, S, HEAD_DIM), lambda b, h, i: (hb + h, b, 0)),
            pl.BlockSpec((heads, S, HEAD_DIM), lambda b, h, i: (2 * hb + h, b, 0)),
        ],
        out_specs=pl.BlockSpec((tq, heads * HEAD_DIM), lambda b, h, i: (b * nq + i, h)),
        out_shape=jax.ShapeDtypeStruct((B * S, H * HEAD_DIM), BF16),
        scratch_shapes=[pltpu.VMEM((heads, tq, HEAD_DIM), F32), pltpu.VMEM((heads, tq, BLOCK), F32)],
        compiler_params=_params(("parallel", "parallel", "arbitrary")),
        name="stick_breaking",
    )(qkv, qkv, qkv)


def _residue_major_permutation():
    i = np.arange(DIL_TILE)
    p = np.zeros((DIL_TILE, DIL_TILE), np.float32)
    p[i, (i % DIL_CHUNK) * DIL_RES + i // DIL_CHUNK] = 1.0
    return p


def _norm_matmul_perm_kernel(x_ref, g_ref, p_ref, w_ref, cs_ref, o_ref, hn_even_ref, hn_odd_ref, hnp_ref,
                             *, natural_tiles):
    i, j = pl.program_id(0), pl.program_id(1)

    def permute(hn_ref):
        for t in range(hn_ref.shape[0] // DIL_TILE):
            rows = slice(t * DIL_TILE, (t + 1) * DIL_TILE)
            hnp_ref[rows, :] = jnp.dot(p_ref[...], hn_ref[rows, :],
                                       preferred_element_type=F32).astype(hnp_ref.dtype)

    def project(lhs_ref):
        y = jnp.dot(lhs_ref[...], w_ref[...], preferred_element_type=F32) * cs_ref[...]
        o_ref[...] = y.astype(o_ref.dtype)

    def row_step(filling_ref, finished_ref):
        @pl.when(j == 0)
        def _():
            permute(finished_ref)

        @pl.when(j < natural_tiles)
        def _():
            _norm_share(x_ref, g_ref, filling_ref, 0, j)
            project(finished_ref)

        @pl.when(j >= natural_tiles)
        def _():
            _norm_share(x_ref, g_ref, filling_ref, 0, j)
            project(hnp_ref)

    @pl.when(i == 0)
    def _():
        _norm_share(x_ref, g_ref, hn_even_ref, 0, j)

    @pl.when(i % 2 == 1)
    def _():
        row_step(hn_odd_ref, hn_even_ref)

    @pl.when(jnp.logical_and(i % 2 == 0, i > 0))
    def _():
        row_step(hn_even_ref, hn_odd_ref)


def _norm_matmul_perm(x, g, w, col_scale, perm, natural_cols, tm, tn):
    M, K = x.shape
    N = w.shape[1]
    tiles, steps = M // tm, N // tn

    def col(i, j):
        return jnp.where(i > 0, j, 0)

    return pl.pallas_call(
        functools.partial(_norm_matmul_perm_kernel, natural_tiles=natural_cols // tn),
        grid=(tiles + 1, steps),
        in_specs=[
            _share_spec(tm, K, tiles, steps),
            pl.BlockSpec((1, K), lambda i, j: (0, 0)),
            pl.BlockSpec((DIL_TILE, DIL_TILE), lambda i, j: (0, 0)),
            pl.BlockSpec((K, tn), lambda i, j: (0, col(i, j))),
            pl.BlockSpec((1, tn), lambda i, j: (0, col(i, j))),
        ],
        out_specs=pl.BlockSpec((tm, tn), lambda i, j: (jnp.maximum(i - 1, 0), col(i, j))),
        out_shape=jax.ShapeDtypeStruct((M, N), BF16),
        scratch_shapes=[pltpu.VMEM((tm, K), BF16), pltpu.VMEM((tm, K), BF16), pltpu.VMEM((tm, K), BF16)],
        compiler_params=_params(("arbitrary", "arbitrary")),
        name="norm_matmul_perm",
    )(x, g.reshape(1, K), perm, w, col_scale.reshape(1, N))


def _rel_bucket_table(dil, order):
    order = np.asarray(order)
    qi = order[:, None]
    kc = np.concatenate([order, BLOCK + order])[None, :]
    rel = BLOCK + qi - kc
    n = jnp.asarray(np.maximum(rel, 0) * dil)
    max_exact = REL_BUCKETS // 2
    nf = jnp.maximum(n, 1).astype(F32)
    large = max_exact + (jnp.log(nf / max_exact) / math.log(REL_MAX_DISTANCE / max_exact)
                         * (REL_BUCKETS - max_exact)).astype(jnp.int32)
    large = jnp.minimum(large, REL_BUCKETS - 1)
    bucket = jnp.where(n < max_exact, n, large).astype(jnp.int32)
    return jnp.where(jnp.asarray((rel >= 0) & (rel <= BLOCK)), bucket, -1)


class _NaturalRows:
    @staticmethod
    def load(ref, j, cols):
        return ref[j * BLOCK:(j + 1) * BLOCK, cols]

    @staticmethod
    def load_prev(ref, cols):
        return ref[:, cols]

    @staticmethod
    def store(ref, j, cols, val):
        ref[j * BLOCK:(j + 1) * BLOCK, cols] = val


class _Residue16Rows:
    per = BLOCK // DIL_CHUNK

    @classmethod
    def load(cls, ref, j, cols):
        return jnp.concatenate([ref[cls.per * j + t, :, cols] for t in range(cls.per)], axis=0)

    @classmethod
    def load_prev(cls, ref, cols):
        return cls.load(ref, 0, cols)

    @classmethod
    def store(cls, ref, j, cols, val):
        for t in range(cls.per):
            ref[cls.per * j + t, :, cols] = val[t * DIL_CHUNK:(t + 1) * DIL_CHUNK]


class _Residue4Rows:
    per = BLOCK // 4 // DIL_CHUNK
    order = [4 * m + a for a in range(4) for m in range(BLOCK // 4)]

    @classmethod
    def load(cls, ref, j, cols):
        return jnp.concatenate([ref[cls.per * j + t, a, :, cols] for a in range(4) for t in range(cls.per)],
                               axis=0)

    @classmethod
    def load_prev(cls, ref, cols):
        return cls.load(ref, 0, cols)

    @classmethod
    def store(cls, ref, j, cols, val):
        for a in range(4):
            for t in range(cls.per):
                piece = (a * cls.per + t) * DIL_CHUNK
                ref[cls.per * j + t, a, :, cols] = val[piece:piece + DIL_CHUNK]


def _dil_kernel(bucket_ref, table_ref, q_ref, kc_ref, kp_ref, vc_ref, vp_ref, o_ref, lse_ref, bias_ref,
                *, sub_blocks, rows):
    first_step = jnp.logical_and(jnp.logical_and(pl.program_id(0) == 0, pl.program_id(1) == 0),
                                 pl.program_id(2) == 0)

    @pl.when(first_step)
    def _():
        bucket = bucket_ref[...]
        for h in range(DIL_GROUP_HEADS):
            b = jnp.full(bucket.shape, NEG_INF, F32)
            for t in range(REL_BUCKETS):
                b = jnp.where(bucket == t, table_ref[t, h], b)
            bias_ref[h] = b

    lane = lax.broadcasted_iota(jnp.int32, (BLOCK, 128), 1)
    keep_keys = jnp.logical_or(lax.broadcasted_iota(jnp.int32, (BLOCK, 2 * BLOCK), 1) >= BLOCK,
                               pl.program_id(2) > 0)
    ones = jnp.ones((2 * BLOCK, HEAD_DIM), BF16)
    contract_last = (((1,), (1,)), ((), ()))
    all_lanes = slice(None)

    def prev_and_cur(cur_ref, prev_ref, j, cols):
        prev = rows.load_prev(prev_ref, cols) if j == 0 else rows.load(cur_ref, j - 1, cols)
        return jnp.concatenate([prev, rows.load(cur_ref, j, cols)], axis=0)

    for j in range(sub_blocks):
        stats = jnp.zeros((BLOCK, 128), F32)
        for h in range(DIL_GROUP_HEADS):
            cols = slice(h * HEAD_DIM, (h + 1) * HEAD_DIM)
            q = rows.load(q_ref, j, cols)
            k = prev_and_cur(kc_ref, kp_ref, j, cols)
            v = prev_and_cur(vc_ref, vp_ref, j, cols)
            s = lax.dot_general(q, k, contract_last, preferred_element_type=F32) + bias_ref[h]
            if j == 0:
                s = jnp.where(keep_keys, s, NEG_INF)
            m = jnp.max(s, axis=-1, keepdims=True)
            p = jnp.exp(s - m).astype(BF16)
            ol = jnp.dot(p, jnp.concatenate([v, ones], axis=1), preferred_element_type=F32)
            o, l = ol[:, :HEAD_DIM], ol[:, HEAD_DIM:]
            rows.store(o_ref, j, cols, (o / l).astype(o_ref.dtype))
            stats = jnp.where(lane == h, m + jnp.log(l), stats)
        rows.store(lse_ref, j, all_lanes, stats)


def _dilated_group(qkv, rel_table, B, S, g, dil):
    W, C = DIL_WIDTH, qkv.shape[1]
    tiles = S // DIL_TILE
    col0 = g * 3
    if dil == 1:
        rows, n_res = _NaturalRows, 1
        blocks = min(4, S // BLOCK)
        steps = S // (blocks * BLOCK)
        views = [(B * S, C), (B * S, W), (B * S, 128)]
        cur_shape, prev_shape = (blocks * BLOCK,), (BLOCK,)
        cur_idx = lambda b, r, i: (b * steps + i,)
        prev_idx = lambda b, r, i: (b * (S // BLOCK) + jnp.maximum(i * blocks - 1, 0),)
    else:
        rows = _Residue16Rows if dil == DIL_RES else _Residue4Rows
        assert dil in (4, DIL_RES) and DIL_RES == 16
        per = rows.per
        t_step = min(4 * per, tiles)
        blocks, steps = t_step // per, tiles // t_step
        if dil == DIL_RES:
            n_res = DIL_RES
            views = [(B * tiles, DIL_RES, DIL_CHUNK, c) for c in (C, W, 128)]
            cur_shape, prev_shape = (t_step, None, DIL_CHUNK), (per, None, DIL_CHUNK)
            cur_idx = lambda b, r, i: (b * steps + i, r, 0)
            prev_idx = lambda b, r, i: (b * (tiles // per) + jnp.maximum(i * blocks - 1, 0), r, 0)
        else:
            n_res = 4
            views = [(B * tiles, 4, 4, DIL_CHUNK, c) for c in (C, W, 128)]
            cur_shape, prev_shape = (t_step, 4, None, DIL_CHUNK), (per, 4, None, DIL_CHUNK)
            cur_idx = lambda b, r, i: (b * steps + i, 0, r, 0)
            prev_idx = lambda b, r, i: (b * (tiles // per) + jnp.maximum(i * blocks - 1, 0), 0, r, 0)

    def spec(shape, idx, width, col):
        return pl.BlockSpec(shape + (width,), lambda b, r, i: idx(b, r, i) + (col,))

    qkv_v = qkv.reshape(views[0])
    order = getattr(rows, "order", list(range(BLOCK)))
    o, lse = pl.pallas_call(
        functools.partial(_dil_kernel, sub_blocks=blocks, rows=rows),
        grid=(B, n_res, steps),
        in_specs=[
            pl.BlockSpec((BLOCK, 2 * BLOCK), lambda b, r, i: (0, 0)),
            pl.BlockSpec(memory_space=pltpu.SMEM),
            spec(cur_shape, cur_idx, W, col0),
            spec(cur_shape, cur_idx, W, col0 + 1), spec(prev_shape, prev_idx, W, col0 + 1),
            spec(cur_shape, cur_idx, W, col0 + 2), spec(prev_shape, prev_idx, W, col0 + 2),
        ],
        out_specs=[spec(cur_shape, cur_idx, W, 0), spec(cur_shape, cur_idx, 128, 0)],
        out_shape=[jax.ShapeDtypeStruct(views[1], BF16), jax.ShapeDtypeStruct(views[2], F32)],
        scratch_shapes=[pltpu.VMEM((DIL_GROUP_HEADS, BLOCK, 2 * BLOCK), F32)],
        compiler_params=_params(("arbitrary", "arbitrary", "arbitrary")),
        name=f"dilated_group{g}",
    )(_rel_bucket_table(dil, order), rel_table, qkv_v, qkv_v, qkv_v, qkv_v, qkv_v)
    return o.reshape(B * S, W), lse.reshape(B * S, 128)


def _bf16_pieces(x):
    hi = x.astype(BF16)
    rest = x - hi.astype(F32)
    mid = rest.astype(BF16)
    return [hi, mid, (rest - mid.astype(F32)).astype(BF16)]


def _combine_matmul_res_kernel(o0_ref, o1_ref, o2_ref, l0_ref, l1_ref, l2_ref, pt_ref, w_ref, r_ref, out_ref,
                               comb_ref):
    pt = pt_ref[...]
    for t in range(out_ref.shape[0] // DIL_TILE):
        rows = slice(t * DIL_TILE, (t + 1) * DIL_TILE)
        pieces = jnp.concatenate(_bf16_pieces(l1_ref[rows, :]) + _bf16_pieces(l2_ref[rows, :]), axis=1)
        lse = jnp.dot(pt, pieces, preferred_element_type=F32)
        l0 = l0_ref[rows, :]
        l1 = lse[:, 0:128] + lse[:, 128:256] + lse[:, 256:384]
        l2 = lse[:, 384:512] + lse[:, 512:640] + lse[:, 640:768]
        mx = jnp.maximum(jnp.maximum(l0, l1), l2)
        e0, e1, e2 = jnp.exp(l0 - mx), jnp.exp(l1 - mx), jnp.exp(l2 - mx)
        den = e0 + e1 + e2
        w0, w1, w2 = e0 / den, e1 / den, e2 / den
        o1 = jnp.dot(pt, o1_ref[rows, :], preferred_element_type=F32)
        o2 = jnp.dot(pt, o2_ref[rows, :], preferred_element_type=F32)
        for h in range(DIL_GROUP_HEADS):
            cols = slice(h * HEAD_DIM, (h + 1) * HEAD_DIM)
            c = (w0[:, h:h + 1] * o0_ref[rows, cols].astype(F32)
                 + w1[:, h:h + 1] * o1[:, cols] + w2[:, h:h + 1] * o2[:, cols])
            comb_ref[rows, cols] = c.astype(comb_ref.dtype)
    out_ref[...] = r_ref[...] + jnp.dot(comb_ref[...], w_ref[...], preferred_element_type=F32)


def _combine_matmul_res(os, lses, unperm, w, res, tm):
    M, K = os[0].shape
    N = w.shape[1]
    o_spec = pl.BlockSpec((tm, K), lambda i: (i, 0))
    l_spec = pl.BlockSpec((tm, 128), lambda i: (i, 0))
    row_block = pl.BlockSpec((tm, N), lambda i: (i, 0))
    return pl.pallas_call(
        _combine_matmul_res_kernel,
        grid=(M // tm,),
        in_specs=[o_spec, o_spec, o_spec, l_spec, l_spec, l_spec,
                  pl.BlockSpec((DIL_TILE, DIL_TILE), lambda i: (0, 0)),
                  pl.BlockSpec((K, N), lambda i: (0, 0), pipeline_mode=pl.Buffered(1)),
                  row_block],
        out_specs=row_block,
        out_shape=jax.ShapeDtypeStruct((M, N), F32),
        scratch_shapes=[pltpu.VMEM((tm, K), BF16)],
        compiler_params=_params(("parallel",)),
        name="combine_matmul_res",
    )(*os, *lses, unperm, w, res)


def _ffn_up_kernel(x_ref, g_ref, wg_ref, wv_ref, cw_ref, cb_ref, o_ref, hn_even_ref, hn_odd_ref,
                   *, tiles_per_seq):
    i, j = pl.program_id(0), pl.program_id(1)
    tm = hn_even_ref.shape[0] - CONV_HALO

    def gated(hn_ref):
        gate = jnp.dot(hn_ref[...], wg_ref[...], preferred_element_type=F32)
        val = jnp.dot(hn_ref[CONV_HALO:, :], wv_ref[...], preferred_element_type=F32)
        conv = cb_ref[...] + gate[CONV_HALO:] * cw_ref[CONV_WIDTH - 1:CONV_WIDTH, :]
        for back in range(1, CONV_WIDTH):
            tap = CONV_WIDTH - 1 - back
            conv = conv + pltpu.roll(gate, back, axis=0)[CONV_HALO:] * cw_ref[tap:tap + 1, :]
        act = conv / (1.0 + jnp.exp(-conv)) * val
        o_ref[...] = act.astype(o_ref.dtype)

    def row_step(filling_ref, finished_ref):
        tail = finished_ref[tm:tm + CONV_HALO, :]
        filling_ref[0:CONV_HALO, :] = jnp.where(i % tiles_per_seq == 0, jnp.zeros_like(tail), tail)
        _norm_share(x_ref, g_ref, filling_ref, CONV_HALO, j)
        gated(finished_ref)

    @pl.when(i == 0)
    def _():
        hn_even_ref[0:CONV_HALO, :] = jnp.zeros((CONV_HALO, hn_even_ref.shape[1]), hn_even_ref.dtype)
        _norm_share(x_ref, g_ref, hn_even_ref, CONV_HALO, j)

    @pl.when(i % 2 == 1)
    def _():
        row_step(hn_odd_ref, hn_even_ref)

    @pl.when(jnp.logical_and(i % 2 == 0, i > 0))
    def _():
        row_step(hn_even_ref, hn_odd_ref)


def _ffn_up(x, g, w_up, conv_w, conv_b, S, tm, tf):
    M, K = x.shape
    d_ff = w_up.shape[1] // 2
    tiles, steps = M // tm, d_ff // tf

    def col(i, j):
        return jnp.where(i > 0, j, 0)

    return pl.pallas_call(
        functools.partial(_ffn_up_kernel, tiles_per_seq=S // tm),
        grid=(tiles + 1, steps),
        in_specs=[
            _share_spec(tm, K, tiles, steps),
            pl.BlockSpec((1, K), lambda i, j: (0, 0)),
            pl.BlockSpec((K, tf), lambda i, j: (0, col(i, j))),
            pl.BlockSpec((K, tf), lambda i, j: (0, steps + col(i, j))),
            pl.BlockSpec((CONV_WIDTH, tf), lambda i, j: (0, col(i, j))),
            pl.BlockSpec((1, tf), lambda i, j: (0, col(i, j))),
        ],
        out_specs=pl.BlockSpec((tm, tf), lambda i, j: (jnp.maximum(i - 1, 0), col(i, j))),
        out_shape=jax.ShapeDtypeStruct((M, d_ff), BF16),
        scratch_shapes=[pltpu.VMEM((tm + CONV_HALO, K), BF16), pltpu.VMEM((tm + CONV_HALO, K), BF16)],
        compiler_params=_params(("arbitrary", "arbitrary")),
        name="ffn_up",
    )(x, g.reshape(1, K), w_up, w_up, conv_w, conv_b.reshape(1, d_ff))


def _score_scale_on_q(n_cols, q_starts, q_width):
    cs = np.ones((n_cols,), np.float32)
    for start in q_starts:
        cs[start:start + q_width] = HEAD_DIM ** -0.5
    return jnp.asarray(cs)


def kernel(x, ln_mix, ln_ffn, ln_f, w_qkv_sb, w_o_sb, w_qkv_dil, w_o_dil, rel_bias, w_up, conv_w, conv_b, w_down):
    B, S, D = x.shape
    depth = ln_mix.shape[0]
    assert S % PROJ_ROWS == 0 and S % min(SB_ROWS, S) == 0 and S % (DIL_RES * BLOCK) == 0
    h = x.reshape(B * S, D)
    for i in range(depth):
        j = i // N_MIXERS
        if i % N_MIXERS == 0:
            q_scale = _score_scale_on_q(w_qkv_sb.shape[2], [0], SB_HEADS * HEAD_DIM)
            qkv = _norm_matmul_heads(h, ln_mix[i], w_qkv_sb[j].astype(BF16), q_scale, tm=PROJ_ROWS, tn=QKV_COLS)
            o = _stick_breaking(qkv, B, S, tq=min(SB_ROWS, S), heads=SB_HEADS_PER_STEP)
            h = _matmul_res(o, w_o_sb[j].astype(BF16), h, tm=RES_ROWS)
        else:
            perm = _residue_major_permutation()
            q_scale = _score_scale_on_q(w_qkv_dil.shape[2], [g * 3 * DIL_WIDTH for g in range(DIL_GROUPS)], DIL_WIDTH)
            qkv = _norm_matmul_perm(h, ln_mix[i], w_qkv_dil[j].astype(BF16), q_scale, jnp.asarray(perm, BF16),
                                    natural_cols=3 * DIL_WIDTH, tm=PROJ_ROWS, tn=QKV_COLS)
            os, lses = [], []
            for g, (window, dil) in enumerate(DIL_PATTERNS):
                assert window // dil == BLOCK and S % (dil * BLOCK) == 0
                table = rel_bias[:, g * DIL_GROUP_HEADS:(g + 1) * DIL_GROUP_HEADS]
                o, lse = _dilated_group(qkv, table, B, S, g, dil)
                os.append(o)
                lses.append(lse)
            h = _combine_matmul_res(os, lses, jnp.asarray(perm.T, BF16), w_o_dil[j].astype(BF16), h, tm=RES_ROWS)
        act = _ffn_up(h, ln_ffn[i], w_up[i].astype(BF16), conv_w[i], conv_b[i], S, tm=PROJ_ROWS, tf=FFN_COLS)
        h = _matmul_res(act, w_down[i].astype(BF16), h, tm=RES_ROWS, norm_gain=ln_f if i == depth - 1 else None)
    return h.reshape(B, S, D)
```

```python
import functools
import math

import jax
import jax.numpy as jnp
import numpy as np
from jax import lax
from jax.experimental import pallas as pl
from jax.experimental.pallas import tpu as pltpu

BLOCK = 128
SB_HEADS = 16
HEAD_DIM = 128
DIL_PATTERNS = ((128, 1), (512, 4), (2048, 16))
DIL_GROUPS = len(DIL_PATTERNS)
DIL_GROUP_HEADS = 8
DIL_WIDTH = DIL_GROUP_HEADS * HEAD_DIM
REL_BUCKETS = 32
REL_MAX_DISTANCE = 2048
CONV_WIDTH = 3
RMS_EPS = 1e-6
NEG_INF = -1e30
LOG2E = math.log2(math.e)
N_MIXERS = 2

SB_EXIT_LOG = -104.0
SB_STATIC_DEPTH = 3

BF16_ROWS = 16
DIL_RES = 16
DIL_CHUNK = BF16_ROWS
DIL_TILE = DIL_RES * DIL_CHUNK
CONV_HALO = BF16_ROWS
NORM_CHUNK = 64
V7X_VMEM_BYTES = 64 * 1024 * 1024
VMEM_LIMIT = V7X_VMEM_BYTES * 7 // 8

PROJ_ROWS = 1024
QKV_COLS = 1536
FFN_COLS = 1024
RES_ROWS = 512
SB_ROWS = 2048
SB_HEADS_PER_STEP = 2
DIL_BLOCKS_PER_STEP = 8

BF16 = jnp.bfloat16
F32 = jnp.float32


def _params(semantics):
    return pltpu.CompilerParams(dimension_semantics=semantics, vmem_limit_bytes=VMEM_LIMIT)


def _rms_rows(x, g):
    ms = jnp.mean(x * x, axis=-1, keepdims=True)
    return x * lax.rsqrt(ms + RMS_EPS) * g


def _norm_share(x_ref, g_ref, dst_ref, dst_row0, step):
    rows = x_ref.shape[0]
    shares = (dst_ref.shape[0] - dst_row0) // rows
    base = dst_row0 + jnp.minimum(step, shares - 1) * rows
    g = g_ref[...]
    for c in range(rows // NORM_CHUNK):
        y = _rms_rows(x_ref[c * NORM_CHUNK:(c + 1) * NORM_CHUNK, :], g)
        row = pl.multiple_of(base + c * NORM_CHUNK, BF16_ROWS)
        dst_ref[pl.ds(row, NORM_CHUNK), :] = y.astype(dst_ref.dtype)


def _row_shares(steps):
    return 1 << (steps.bit_length() - 1)


def _share_spec(tm, K, tiles, steps):
    shares = _row_shares(steps)
    return pl.BlockSpec((tm // shares, K),
                        lambda i, j: (jnp.minimum(i, tiles - 1) * shares + jnp.minimum(j, shares - 1), 0))


def _norm_matmul_kernel(x_ref, g_ref, w_ref, cs_ref, o_ref, hn_even_ref, hn_odd_ref):
    i, j = pl.program_id(0), pl.program_id(1)

    def project(hn_ref):
        y = (jnp.dot(hn_ref[...], w_ref[...], preferred_element_type=F32) * cs_ref[...]).astype(o_ref.dtype)
        for c in range(o_ref.shape[0]):
            o_ref[c] = y[:, c * HEAD_DIM:(c + 1) * HEAD_DIM]

    @pl.when(i == 0)
    def _():
        _norm_share(x_ref, g_ref, hn_even_ref, 0, j)

    @pl.when(i % 2 == 1)
    def _():
        _norm_share(x_ref, g_ref, hn_odd_ref, 0, j)
        project(hn_even_ref)

    @pl.when(jnp.logical_and(i % 2 == 0, i > 0))
    def _():
        _norm_share(x_ref, g_ref, hn_even_ref, 0, j)
        project(hn_odd_ref)


def _norm_matmul_heads(x, g, w, col_scale, tm, tn):
    M, K = x.shape
    N = w.shape[1]
    slabs = tn // HEAD_DIM
    tiles, steps = M // tm, N // tn

    def col(i, j):
        return jnp.where(i > 0, j, 0)

    return pl.pallas_call(
        _norm_matmul_kernel,
        grid=(tiles + 1, steps),
        in_specs=[
            _share_spec(tm, K, tiles, steps),
            pl.BlockSpec((1, K), lambda i, j: (0, 0)),
            pl.BlockSpec((K, tn), lambda i, j: (0, col(i, j))),
            pl.BlockSpec((1, tn), lambda i, j: (0, col(i, j))),
        ],
        out_specs=pl.BlockSpec((slabs, tm, HEAD_DIM), lambda i, j: (col(i, j), jnp.maximum(i - 1, 0), 0)),
        out_shape=jax.ShapeDtypeStruct((N // HEAD_DIM, M, HEAD_DIM), BF16),
        scratch_shapes=[pltpu.VMEM((tm, K), BF16), pltpu.VMEM((tm, K), BF16)],
        compiler_params=_params(("arbitrary", "arbitrary")),
        name="norm_matmul",
    )(x, g.reshape(1, K), w, col_scale.reshape(1, N))


def _matmul_res_kernel(a_ref, w_ref, r_ref, o_ref):
    o_ref[...] = r_ref[...] + jnp.dot(a_ref[...], w_ref[...], preferred_element_type=F32)


def _matmul_res_norm_kernel(a_ref, w_ref, r_ref, g_ref, o_ref):
    y = r_ref[...] + jnp.dot(a_ref[...], w_ref[...], preferred_element_type=F32)
    o_ref[...] = _rms_rows(y, g_ref[...])


def _matmul_res(a, w, res, tm, norm_gain=None):
    M, K = a.shape
    N = w.shape[1]
    in_specs = [
        pl.BlockSpec((tm, K), lambda i: (i, 0)),
        pl.BlockSpec((K, N), lambda i: (0, 0), pipeline_mode=pl.Buffered(1)),
        pl.BlockSpec((tm, N), lambda i: (i, 0)),
    ]
    args = [a, w, res]
    if norm_gain is not None:
        in_specs.append(pl.BlockSpec((1, N), lambda i: (0, 0)))
        args.append(norm_gain.reshape(1, N))
    return pl.pallas_call(
        _matmul_res_kernel if norm_gain is None else _matmul_res_norm_kernel,
        grid=(M // tm,),
        in_specs=in_specs,
        out_specs=pl.BlockSpec((tm, N), lambda i: (i, 0)),
        out_shape=jax.ShapeDtypeStruct((M, N), F32),
        compiler_params=_params(("parallel",)),
        name="matmul_res" if norm_gain is None else "matmul_res_norm",
    )(*args)


def _sb_kernel(q_ref, k_ref, v_ref, o_ref, acc_ref, spent_ref, *, chains, heads):
    row = lax.broadcasted_iota(jnp.int32, (BLOCK, BLOCK), 0)
    col = lax.broadcasted_iota(jnp.int32, (BLOCK, BLOCK), 1)
    causal = col < row
    suffix = (row >= col).astype(BF16)
    base = pl.program_id(2) * chains
    contract_last = (((1,), (1,)), ((), ()))

    def mask_first_block(x):
        head = jnp.where(causal, x[:BLOCK], 0.0)
        return head if x.shape[0] == BLOCK else jnp.concatenate([head, x[BLOCK:]], axis=0)

    def key_block(h, rows, kb, on_diagonal):
        start = pl.multiple_of(kb * BLOCK, BLOCK)
        k = k_ref[h, pl.ds(start, BLOCK), :]
        v = v_ref[h, pl.ds(start, BLOCK), :]
        z = lax.dot_general(q_ref[h, rows, :], k, contract_last, preferred_element_type=F32)
        drop = jnp.maximum(z, 0.0) + jnp.log(1.0 + jnp.exp2(jnp.abs(z) * -LOG2E))
        if on_diagonal:
            drop = mask_first_block(drop)
        upto = jnp.dot(drop.astype(BF16), suffix, preferred_element_type=F32)
        a = jnp.exp(z - (upto + spent_ref[h, rows, :]))
        if on_diagonal:
            a = mask_first_block(a)
        acc_ref[h, rows, :] += jnp.dot(a.astype(BF16), v, preferred_element_type=F32)
        spent_ref[h, rows, :] += upto[:, 0:1]

    def retire_if_no_keys(h, rows, kb):
        spent_ref[h, rows, :] = jnp.where(kb >= 0, spent_ref[h, rows, :], -NEG_INF)

    acc_ref[...] = jnp.zeros_like(acc_ref)
    spent_ref[...] = jnp.zeros_like(spent_ref)
    for first in range(chains - 1, -SB_STATIC_DEPTH, -1):
        lo, hi = max(first, 0), min(first + SB_STATIC_DEPTH - 1, chains - 1)
        rows = slice(lo * BLOCK, (hi + 1) * BLOCK)
        kb = base + first
        for h in range(heads):
            if first >= 0:
                key_block(h, rows, kb, True)
            else:
                retire_if_no_keys(h, rows, kb)
                key_block(h, rows, jnp.maximum(kb, 0), False)

    def cond(carry):
        _, least_spent = carry
        return least_spent < -SB_EXIT_LOG

    def body(carry):
        delta, _ = carry
        for c in range(chains):
            rows = slice(c * BLOCK, (c + 1) * BLOCK)
            kb = base + c - delta
            for h in range(heads):
                retire_if_no_keys(h, rows, kb)
                key_block(h, rows, jnp.maximum(kb, 0), False)
        return delta + 1, jnp.min(spent_ref[...])

    lax.while_loop(cond, body, (jnp.int32(SB_STATIC_DEPTH), jnp.min(spent_ref[...])))
    for h in range(heads):
        o_ref[:, h * HEAD_DIM:(h + 1) * HEAD_DIM] = acc_ref[h].astype(o_ref.dtype)


def _stick_breaking(qkv, B, S, tq, heads):
    H = SB_HEADS
    nq = S // tq
    hb = H // heads
    return pl.pallas_call(
        functools.partial(_sb_kernel, chains=tq // BLOCK, heads=heads),
        grid=(B, hb, nq),
        in_specs=[
            pl.BlockSpec((heads, tq, HEAD_DIM), lambda b, h, i: (h, b * nq + i, 0)),
            pl.BlockSpec((heads, S, HEAD_DIM), lambda b, h, i: (hb + h, b, 0)),
            pl.BlockSpec((heads, S, HEAD_DIM), lambda b, h, i: (2 * hb + h, b, 0)),
        ],
        out_specs=pl.BlockSpec((tq, heads * HEAD_DIM), lambda b, h, i: (b * nq + i, h)),
        out_shape=jax.ShapeDtypeStruct((B * S, H * HEAD_DIM), BF16),
        scratch_shapes=[pltpu.VMEM((heads, tq, HEAD_DIM), F32), pltpu.VMEM((heads, tq, BLOCK), F32)],
        compiler_params=_params(("parallel", "parallel", "arbitrary")),
        name="stick_breaking",
    )(qkv, qkv, qkv)


def _residue_major_permutation():
    i = np.arange(DIL_TILE)
    p = np.zeros((DIL_TILE, DIL_TILE), np.float32)
    p[i, (i % DIL_CHUNK) * DIL_RES + i // DIL_CHUNK] = 1.0
    return p


def _norm_matmul_perm_kernel(x_ref, g_ref, p_ref, w_ref, cs_ref, o_ref, hn_even_ref, hn_odd_ref, hnp_ref,
                             *, natural_tiles):
    i, j = pl.program_id(0), pl.program_id(1)

    def permute(hn_ref):
        for t in range(hn_ref.shape[0] // DIL_TILE):
            rows = slice(t * DIL_TILE, (t + 1) * DIL_TILE)
            hnp_ref[rows, :] = jnp.dot(p_ref[...], hn_ref[rows, :],
                                       preferred_element_type=F32).astype(hnp_ref.dtype)

    def project(lhs_ref):
        y = jnp.dot(lhs_ref[...], w_ref[...], preferred_element_type=F32) * cs_ref[...]
        o_ref[...] = y.astype(o_ref.dtype)

    def row_step(filling_ref, finished_ref):
        @pl.when(j == 0)
        def _():
            permute(finished_ref)

        @pl.when(j < natural_tiles)
        def _():
            _norm_share(x_ref, g_ref, filling_ref, 0, j)
            project(finished_ref)

        @pl.when(j >= natural_tiles)
        def _():
            _norm_share(x_ref, g_ref, filling_ref, 0, j)
            project(hnp_ref)

    @pl.when(i == 0)
    def _():
        _norm_share(x_ref, g_ref, hn_even_ref, 0, j)

    @pl.when(i % 2 == 1)
    def _():
        row_step(hn_odd_ref, hn_even_ref)

    @pl.when(jnp.logical_and(i % 2 == 0, i > 0))
    def _():
        row_step(hn_even_ref, hn_odd_ref)


def _norm_matmul_perm(x, g, w, col_scale, perm, natural_cols, tm, tn):
    M, K = x.shape
    N = w.shape[1]
    tiles, steps = M // tm, N // tn

    def col(i, j):
        return jnp.where(i > 0, j, 0)

    return pl.pallas_call(
        functools.partial(_norm_matmul_perm_kernel, natural_tiles=natural_cols // tn),
        grid=(tiles + 1, steps),
        in_specs=[
            _share_spec(tm, K, tiles, steps),
            pl.BlockSpec((1, K), lambda i, j: (0, 0)),
            pl.BlockSpec((DIL_TILE, DIL_TILE), lambda i, j: (0, 0)),
            pl.BlockSpec((K, tn), lambda i, j: (0, col(i, j))),
            pl.BlockSpec((1, tn), lambda i, j: (0, col(i, j))),
        ],
        out_specs=pl.BlockSpec((tm, tn), lambda i, j: (jnp.maximum(i - 1, 0), col(i, j))),
        out_shape=jax.ShapeDtypeStruct((M, N), BF16),
        scratch_shapes=[pltpu.VMEM((tm, K), BF16), pltpu.VMEM((tm, K), BF16), pltpu.VMEM((tm, K), BF16)],
        compiler_params=_params(("arbitrary", "arbitrary")),
        name="norm_matmul_perm",
    )(x, g.reshape(1, K), perm, w, col_scale.reshape(1, N))


def _rel_bucket_table(dil, order):
    order = np.asarray(order)
    qi = order[:, None]
    kc = np.concatenate([order, BLOCK + order])[None, :]
    rel = BLOCK + qi - kc
    n = jnp.asarray(np.maximum(rel, 0) * dil)
    max_exact = REL_BUCKETS // 2
    nf = jnp.maximum(n, 1).astype(F32)
    large = max_exact + (jnp.log(nf / max_exact) / math.log(REL_MAX_DISTANCE / max_exact)
                         * (REL_BUCKETS - max_exact)).astype(jnp.int32)
    large = jnp.minimum(large, REL_BUCKETS - 1)
    bucket = jnp.where(n < max_exact, n, large).astype(jnp.int32)
    return jnp.where(jnp.asarray((rel >= 0) & (rel <= BLOCK)), bucket, -1)


class _NaturalRows:
    @staticmethod
    def load(ref, j, cols):
        return ref[j * BLOCK:(j + 1) * BLOCK, cols]

    @staticmethod
    def load_prev(ref, cols):
        return ref[:, cols]

    @staticmethod
    def store(ref, j, cols, val):
        ref[j * BLOCK:(j + 1) * BLOCK, cols] = val


class _Residue16Rows:
    per = BLOCK // DIL_CHUNK

    @classmethod
    def load(cls, ref, j, cols):
        return jnp.concatenate([ref[cls.per * j + t, :, cols] for t in range(cls.per)], axis=0)

    @classmethod
    def load_prev(cls, ref, cols):
        return cls.load(ref, 0, cols)

    @classmethod
    def store(cls, ref, j, cols, val):
        for t in range(cls.per):
            ref[cls.per * j + t, :, cols] = val[t * DIL_CHUNK:(t + 1) * DIL_CHUNK]


class _Residue4Rows:
    per = BLOCK // 4 // DIL_CHUNK
    order = [4 * m + a for a in range(4) for m in range(BLOCK // 4)]

    @classmethod
    def load(cls, ref, j, cols):
        return jnp.concatenate([ref[cls.per * j + t, a, :, cols] for a in range(4) for t in range(cls.per)],
                               axis=0)

    @classmethod
    def load_prev(cls, ref, cols):
        return cls.load(ref, 0, cols)

    @classmethod
    def store(cls, ref, j, cols, val):
        for a in range(4):
            for t in range(cls.per):
                piece = (a * cls.per + t) * DIL_CHUNK
                ref[cls.per * j + t, a, :, cols] = val[piece:piece + DIL_CHUNK]


def _dil_kernel(bucket_ref, table_ref, q_ref, kc_ref, kp_ref, vc_ref, vp_ref, o_ref, lse_ref, bias_ref,
                *, sub_blocks, rows):
    first_step = jnp.logical_and(jnp.logical_and(pl.program_id(0) == 0, pl.program_id(1) == 0),
                                 pl.program_id(2) == 0)

    @pl.when(first_step)
    def _():
        bucket = bucket_ref[...]
        for h in range(DIL_GROUP_HEADS):
            b = jnp.full(bucket.shape, NEG_INF, F32)
            for t in range(REL_BUCKETS):
                b = jnp.where(bucket == t, table_ref[t, h], b)
            bias_ref[h] = b

    lane = lax.broadcasted_iota(jnp.int32, (BLOCK, 128), 1)
    keep_keys = jnp.logical_or(lax.broadcasted_iota(jnp.int32, (BLOCK, 2 * BLOCK), 1) >= BLOCK,
                               pl.program_id(2) > 0)
    ones = jnp.ones((2 * BLOCK, HEAD_DIM), BF16)
    contract_last = (((1,), (1,)), ((), ()))
    all_lanes = slice(None)

    def prev_and_cur(cur_ref, prev_ref, j, cols):
        prev = rows.load_prev(prev_ref, cols) if j == 0 else rows.load(cur_ref, j - 1, cols)
        return jnp.concatenate([prev, rows.load(cur_ref, j, cols)], axis=0)

    for j in range(sub_blocks):
        stats = jnp.zeros((BLOCK, 128), F32)
        for h in range(DIL_GROUP_HEADS):
            cols = slice(h * HEAD_DIM, (h + 1) * HEAD_DIM)
            q = rows.load(q_ref, j, cols)
            k = prev_and_cur(kc_ref, kp_ref, j, cols)
            v = prev_and_cur(vc_ref, vp_ref, j, cols)
            s = lax.dot_general(q, k, contract_last, preferred_element_type=F32) + bias_ref[h]
            if j == 0:
                s = jnp.where(keep_keys, s, NEG_INF)
            m = jnp.max(s, axis=-1, keepdims=True)
            p = jnp.exp(s - m).astype(BF16)
            ol = jnp.dot(p, jnp.concatenate([v, ones], axis=1), preferred_element_type=F32)
            o, l = ol[:, :HEAD_DIM], ol[:, HEAD_DIM:]
            rows.store(o_ref, j, cols, (o / l).astype(o_ref.dtype))
            stats = jnp.where(lane == h, m + jnp.log(l), stats)
        rows.store(lse_ref, j, all_lanes, stats)


def _dilated_group(qkv, rel_table, B, S, g, dil):
    W, C = DIL_WIDTH, qkv.shape[1]
    tiles = S // DIL_TILE
    col0 = g * 3
    if dil == 1:
        rows, n_res = _NaturalRows, 1
        blocks = min(DIL_BLOCKS_PER_STEP, S // BLOCK)
        steps = S // (blocks * BLOCK)
        views = [(B * S, C), (B * S, W), (B * S, 128)]
        cur_shape, prev_shape = (blocks * BLOCK,), (BLOCK,)
        cur_idx = lambda b, r, i: (b * steps + i,)
        prev_idx = lambda b, r, i: (b * (S // BLOCK) + jnp.maximum(i * blocks - 1, 0),)
    else:
        rows = _Residue16Rows if dil == DIL_RES else _Residue4Rows
        assert dil in (4, DIL_RES) and DIL_RES == 16
        per = rows.per
        t_step = min(DIL_BLOCKS_PER_STEP * per, tiles)
        blocks, steps = t_step // per, tiles // t_step
        if dil == DIL_RES:
            n_res = DIL_RES
            views = [(B * tiles, DIL_RES, DIL_CHUNK, c) for c in (C, W, 128)]
            cur_shape, prev_shape = (t_step, None, DIL_CHUNK), (per, None, DIL_CHUNK)
            cur_idx = lambda b, r, i: (b * steps + i, r, 0)
            prev_idx = lambda b, r, i: (b * (tiles // per) + jnp.maximum(i * blocks - 1, 0), r, 0)
        else:
            n_res = 4
            views = [(B * tiles, 4, 4, DIL_CHUNK, c) for c in (C, W, 128)]
            cur_shape, prev_shape = (t_step, 4, None, DIL_CHUNK), (per, 4, None, DIL_CHUNK)
            cur_idx = lambda b, r, i: (b * steps + i, 0, r, 0)
            prev_idx = lambda b, r, i: (b * (tiles // per) + jnp.maximum(i * blocks - 1, 0), 0, r, 0)

    def spec(shape, idx, width, col):
        return pl.BlockSpec(shape + (width,), lambda b, r, i: idx(b, r, i) + (col,))

    qkv_v = qkv.reshape(views[0])
    order = getattr(rows, "order", list(range(BLOCK)))
    o, lse = pl.pallas_call(
        functools.partial(_dil_kernel, sub_blocks=blocks, rows=rows),
        grid=(B, n_res, steps),
        in_specs=[
            pl.BlockSpec((BLOCK, 2 * BLOCK), lambda b, r, i: (0, 0)),
            pl.BlockSpec(memory_space=pltpu.SMEM),
            spec(cur_shape, cur_idx, W, col0),
            spec(cur_shape, cur_idx, W, col0 + 1), spec(prev_shape, prev_idx, W, col0 + 1),
            spec(cur_shape, cur_idx, W, col0 + 2), spec(prev_shape, prev_idx, W, col0 + 2),
        ],
        out_specs=[spec(cur_shape, cur_idx, W, 0), spec(cur_shape, cur_idx, 128, 0)],
        out_shape=[jax.ShapeDtypeStruct(views[1], BF16), jax.ShapeDtypeStruct(views[2], F32)],
        scratch_shapes=[pltpu.VMEM((DIL_GROUP_HEADS, BLOCK, 2 * BLOCK), F32)],
        compiler_params=_params(("arbitrary", "arbitrary", "arbitrary")),
        name=f"dilated_group{g}",
    )(_rel_bucket_table(dil, order), rel_table, qkv_v, qkv_v, qkv_v, qkv_v, qkv_v)
    return o.reshape(B * S, W), lse.reshape(B * S, 128)


def _bf16_pieces(x):
    hi = x.astype(BF16)
    rest = x - hi.astype(F32)
    mid = rest.astype(BF16)
    return [hi, mid, (rest - mid.astype(F32)).astype(BF16)]


def _combine_matmul_res_kernel(o0_ref, o1_ref, o2_ref, l0_ref, l1_ref, l2_ref, pt_ref, w_ref, r_ref, out_ref,
                               comb_ref):
    pt = pt_ref[...]
    for t in range(out_ref.shape[0] // DIL_TILE):
        rows = slice(t * DIL_TILE, (t + 1) * DIL_TILE)
        pieces = jnp.concatenate(_bf16_pieces(l1_ref[rows, :]) + _bf16_pieces(l2_ref[rows, :]), axis=1)
        lse = jnp.dot(pt, pieces, preferred_element_type=F32)
        l0 = l0_ref[rows, :]
        l1 = lse[:, 0:128] + lse[:, 128:256] + lse[:, 256:384]
        l2 = lse[:, 384:512] + lse[:, 512:640] + lse[:, 640:768]
        mx = jnp.maximum(jnp.maximum(l0, l1), l2)
        e0, e1, e2 = jnp.exp(l0 - mx), jnp.exp(l1 - mx), jnp.exp(l2 - mx)
        den = e0 + e1 + e2
        w0, w1, w2 = e0 / den, e1 / den, e2 / den
        o1 = jnp.dot(pt, o1_ref[rows, :], preferred_element_type=F32)
        o2 = jnp.dot(pt, o2_ref[rows, :], preferred_element_type=F32)
        for h in range(DIL_GROUP_HEADS):
            cols = slice(h * HEAD_DIM, (h + 1) * HEAD_DIM)
            c = (w0[:, h:h + 1] * o0_ref[rows, cols].astype(F32)
                 + w1[:, h:h + 1] * o1[:, cols] + w2[:, h:h + 1] * o2[:, cols])
            comb_ref[rows, cols] = c.astype(comb_ref.dtype)
    out_ref[...] = r_ref[...] + jnp.dot(comb_ref[...], w_ref[...], preferred_element_type=F32)


def _combine_matmul_res(os, lses, unperm, w, res, tm):
    M, K = os[0].shape
    N = w.shape[1]
    o_spec = pl.BlockSpec((tm, K), lambda i: (i, 0))
    l_spec = pl.BlockSpec((tm, 128), lambda i: (i, 0))
    row_block = pl.BlockSpec((tm, N), lambda i: (i, 0))
    return pl.pallas_call(
        _combine_matmul_res_kernel,
        grid=(M // tm,),
        in_specs=[o_spec, o_spec, o_spec, l_spec, l_spec, l_spec,
                  pl.BlockSpec((DIL_TILE, DIL_TILE), lambda i: (0, 0)),
                  pl.BlockSpec((K, N), lambda i: (0, 0), pipeline_mode=pl.Buffered(1)),
                  row_block],
        out_specs=row_block,
        out_shape=jax.ShapeDtypeStruct((M, N), F32),
        scratch_shapes=[pltpu.VMEM((tm, K), BF16)],
        compiler_params=_params(("parallel",)),
        name="combine_matmul_res",
    )(*os, *lses, unperm, w, res)


def _ffn_up_kernel(x_ref, g_ref, wg_ref, wv_ref, cw_ref, cb_ref, o_ref, hn_even_ref, hn_odd_ref,
                   *, tiles_per_seq):
    i, j = pl.program_id(0), pl.program_id(1)
    tm = hn_even_ref.shape[0] - CONV_HALO

    def gated(hn_ref):
        gate = jnp.dot(hn_ref[...], wg_ref[...], preferred_element_type=F32)
        val = jnp.dot(hn_ref[CONV_HALO:, :], wv_ref[...], preferred_element_type=F32)
        conv = cb_ref[...] + gate[CONV_HALO:] * cw_ref[CONV_WIDTH - 1:CONV_WIDTH, :]
        for back in range(1, CONV_WIDTH):
            tap = CONV_WIDTH - 1 - back
            conv = conv + pltpu.roll(gate, back, axis=0)[CONV_HALO:] * cw_ref[tap:tap + 1, :]
        act = conv / (1.0 + jnp.exp(-conv)) * val
        o_ref[...] = act.astype(o_ref.dtype)

    def row_step(filling_ref, finished_ref):
        tail = finished_ref[tm:tm + CONV_HALO, :]
        filling_ref[0:CONV_HALO, :] = jnp.where(i % tiles_per_seq == 0, jnp.zeros_like(tail), tail)
        _norm_share(x_ref, g_ref, filling_ref, CONV_HALO, j)
        gated(finished_ref)

    @pl.when(i == 0)
    def _():
        hn_even_ref[0:CONV_HALO, :] = jnp.zeros((CONV_HALO, hn_even_ref.shape[1]), hn_even_ref.dtype)
        _norm_share(x_ref, g_ref, hn_even_ref, CONV_HALO, j)

    @pl.when(i % 2 == 1)
    def _():
        row_step(hn_odd_ref, hn_even_ref)

    @pl.when(jnp.logical_and(i % 2 == 0, i > 0))
    def _():
        row_step(hn_even_ref, hn_odd_ref)


def _ffn_up(x, g, w_up, conv_w, conv_b, S, tm, tf):
    M, K = x.shape
    d_ff = w_up.shape[1] // 2
    tiles, steps = M // tm, d_ff // tf

    def col(i, j):
        return jnp.where(i > 0, j, 0)

    return pl.pallas_call(
        functools.partial(_ffn_up_kernel, tiles_per_seq=S // tm),
        grid=(tiles + 1, steps),
        in_specs=[
            _share_spec(tm, K, tiles, steps),
            pl.BlockSpec((1, K), lambda i, j: (0, 0)),
            pl.BlockSpec((K, tf), lambda i, j: (0, col(i, j))),
            pl.BlockSpec((K, tf), lambda i, j: (0, steps + col(i, j))),
            pl.BlockSpec((CONV_WIDTH, tf), lambda i, j: (0, col(i, j))),
            pl.BlockSpec((1, tf), lambda i, j: (0, col(i, j))),
        ],
        out_specs=pl.BlockSpec((tm, tf), lambda i, j: (jnp.maximum(i - 1, 0), col(i, j))),
        out_shape=jax.ShapeDtypeStruct((M, d_ff), BF16),
        scratch_shapes=[pltpu.VMEM((tm + CONV_HALO, K), BF16), pltpu.VMEM((tm + CONV_HALO, K), BF16)],
        compiler_params=_params(("arbitrary", "arbitrary")),
        name="ffn_up",
    )(x, g.reshape(1, K), w_up, w_up, conv_w, conv_b.reshape(1, d_ff))


def _score_scale_on_q(n_cols, q_starts, q_width):
    cs = np.ones((n_cols,), np.float32)
    for start in q_starts:
        cs[start:start + q_width] = HEAD_DIM ** -0.5
    return jnp.asarray(cs)


def kernel(x, ln_mix, ln_ffn, ln_f, w_qkv_sb, w_o_sb, w_qkv_dil, w_o_dil, rel_bias, w_up, conv_w, conv_b, w_down):
    B, S, D = x.shape
    depth = ln_mix.shape[0]
    assert S % PROJ_ROWS == 0 and S % min(SB_ROWS, S) == 0 and S % (DIL_RES * BLOCK) == 0
    h = x.reshape(B * S, D)
    for i in range(depth):
        j = i // N_MIXERS
        if i % N_MIXERS == 0:
            q_scale = _score_scale_on_q(w_qkv_sb.shape[2], [0], SB_HEADS * HEAD_DIM)
            qkv = _norm_matmul_heads(h, ln_mix[i], w_qkv_sb[j].astype(BF16), q_scale, tm=PROJ_ROWS, tn=QKV_COLS)
            o = _stick_breaking(qkv, B, S, tq=min(SB_ROWS, S), heads=SB_HEADS_PER_STEP)
            h = _matmul_res(o, w_o_sb[j].astype(BF16), h, tm=RES_ROWS)
        else:
            perm = _residue_major_permutation()
            q_scale = _score_scale_on_q(w_qkv_dil.shape[2], [g * 3 * DIL_WIDTH for g in range(DIL_GROUPS)], DIL_WIDTH)
            qkv = _norm_matmul_perm(h, ln_mix[i], w_qkv_dil[j].astype(BF16), q_scale, jnp.asarray(perm, BF16),
                                    natural_cols=3 * DIL_WIDTH, tm=PROJ_ROWS, tn=QKV_COLS)
            os, lses = [], []
            for g, (window, dil) in enumerate(DIL_PATTERNS):
                assert window // dil == BLOCK and S % (dil * BLOCK) == 0
                table = rel_bias[:, g * DIL_GROUP_HEADS:(g + 1) * DIL_GROUP_HEADS]
                o, lse = _dilated_group(qkv, table, B, S, g, dil)
                os.append(o)
                lses.append(lse)
            h = _combine_matmul_res(os, lses, jnp.asarray(perm.T, BF16), w_o_dil[j].astype(BF16), h, tm=RES_ROWS)
        act = _ffn_up(h, ln_ffn[i], w_up[i].astype(BF16), conv_w[i], conv_b[i], S, tm=PROJ_ROWS, tf=FFN_COLS)
        h = _matmul_res(act, w_down[i].astype(BF16), h, tm=RES_ROWS, norm_gain=ln_f if i == depth - 1 else None)
    return h.reshape(B, S, D)
```

```python
import functools
import math

import jax
import jax.numpy as jnp
import numpy as np
from jax import lax
from jax.experimental import pallas as pl
from jax.experimental.pallas import tpu as pltpu

BLOCK = 128
SB_HEADS = 16
HEAD_DIM = 128
DIL_PATTERNS = ((128, 1), (512, 4), (2048, 16))
DIL_GROUPS = len(DIL_PATTERNS)
DIL_GROUP_HEADS = 8
DIL_WIDTH = DIL_GROUP_HEADS * HEAD_DIM
REL_BUCKETS = 32
REL_MAX_DISTANCE = 2048
CONV_WIDTH = 3
RMS_EPS = 1e-6
NEG_INF = -1e30
LOG2E = math.log2(math.e)
N_MIXERS = 2

SB_EXIT_LOG = -104.0
SB_STATIC_DEPTH = 3

BF16_ROWS = 16
DIL_RES = 16
DIL_CHUNK = BF16_ROWS
DIL_TILE = DIL_RES * DIL_CHUNK
CONV_HALO = BF16_ROWS
NORM_CHUNK = 64
V7X_VMEM_BYTES = 64 * 1024 * 1024
VMEM_LIMIT = V7X_VMEM_BYTES * 7 // 8

PROJ_ROWS = 1024
QKV_COLS = 1536
FFN_COLS = 1024
RES_ROWS = 512
SB_ROWS = 2048
SB_HEADS_PER_STEP = 2
DIL_BLOCKS_PER_STEP = 16

BF16 = jnp.bfloat16
F32 = jnp.float32


def _params(semantics):
    return pltpu.CompilerParams(dimension_semantics=semantics, vmem_limit_bytes=VMEM_LIMIT)


def _rms_rows(x, g):
    ms = jnp.mean(x * x, axis=-1, keepdims=True)
    return x * lax.rsqrt(ms + RMS_EPS) * g


def _norm_share(x_ref, g_ref, dst_ref, dst_row0, step):
    rows = x_ref.shape[0]
    shares = (dst_ref.shape[0] - dst_row0) // rows
    base = dst_row0 + jnp.minimum(step, shares - 1) * rows
    g = g_ref[...]
    for c in range(rows // NORM_CHUNK):
        y = _rms_rows(x_ref[c * NORM_CHUNK:(c + 1) * NORM_CHUNK, :], g)
        row = pl.multiple_of(base + c * NORM_CHUNK, BF16_ROWS)
        dst_ref[pl.ds(row, NORM_CHUNK), :] = y.astype(dst_ref.dtype)


def _row_shares(steps):
    return 1 << (steps.bit_length() - 1)


def _share_spec(tm, K, tiles, steps):
    shares = _row_shares(steps)
    return pl.BlockSpec((tm // shares, K),
                        lambda i, j: (jnp.minimum(i, tiles - 1) * shares + jnp.minimum(j, shares - 1), 0))


def _norm_matmul_kernel(x_ref, g_ref, w_ref, cs_ref, o_ref, hn_even_ref, hn_odd_ref):
    i, j = pl.program_id(0), pl.program_id(1)

    def project(hn_ref):
        y = (jnp.dot(hn_ref[...], w_ref[...], preferred_element_type=F32) * cs_ref[...]).astype(o_ref.dtype)
        for c in range(o_ref.shape[0]):
            o_ref[c] = y[:, c * HEAD_DIM:(c + 1) * HEAD_DIM]

    @pl.when(i == 0)
    def _():
        _norm_share(x_ref, g_ref, hn_even_ref, 0, j)

    @pl.when(i % 2 == 1)
    def _():
        _norm_share(x_ref, g_ref, hn_odd_ref, 0, j)
        project(hn_even_ref)

    @pl.when(jnp.logical_and(i % 2 == 0, i > 0))
    def _():
        _norm_share(x_ref, g_ref, hn_even_ref, 0, j)
        project(hn_odd_ref)


def _norm_matmul_heads(x, g, w, col_scale, tm, tn):
    M, K = x.shape
    N = w.shape[1]
    slabs = tn // HEAD_DIM
    tiles, steps = M // tm, N // tn

    def col(i, j):
        return jnp.where(i > 0, j, 0)

    return pl.pallas_call(
        _norm_matmul_kernel,
        grid=(tiles + 1, steps),
        in_specs=[
            _share_spec(tm, K, tiles, steps),
            pl.BlockSpec((1, K), lambda i, j: (0, 0)),
            pl.BlockSpec((K, tn), lambda i, j: (0, col(i, j))),
            pl.BlockSpec((1, tn), lambda i, j: (0, col(i, j))),
        ],
        out_specs=pl.BlockSpec((slabs, tm, HEAD_DIM), lambda i, j: (col(i, j), jnp.maximum(i - 1, 0), 0)),
        out_shape=jax.ShapeDtypeStruct((N // HEAD_DIM, M, HEAD_DIM), BF16),
        scratch_shapes=[pltpu.VMEM((tm, K), BF16), pltpu.VMEM((tm, K), BF16)],
        compiler_params=_params(("arbitrary", "arbitrary")),
        name="norm_matmul",
    )(x, g.reshape(1, K), w, col_scale.reshape(1, N))


def _matmul_res_kernel(a_ref, w_ref, r_ref, o_ref):
    o_ref[...] = r_ref[...] + jnp.dot(a_ref[...], w_ref[...], preferred_element_type=F32)


def _matmul_res_norm_kernel(a_ref, w_ref, r_ref, g_ref, o_ref):
    y = r_ref[...] + jnp.dot(a_ref[...], w_ref[...], preferred_element_type=F32)
    o_ref[...] = _rms_rows(y, g_ref[...])


def _matmul_res(a, w, res, tm, norm_gain=None):
    M, K = a.shape
    N = w.shape[1]
    in_specs = [
        pl.BlockSpec((tm, K), lambda i: (i, 0)),
        pl.BlockSpec((K, N), lambda i: (0, 0), pipeline_mode=pl.Buffered(1)),
        pl.BlockSpec((tm, N), lambda i: (i, 0)),
    ]
    args = [a, w, res]
    if norm_gain is not None:
        in_specs.append(pl.BlockSpec((1, N), lambda i: (0, 0)))
        args.append(norm_gain.reshape(1, N))
    return pl.pallas_call(
        _matmul_res_kernel if norm_gain is None else _matmul_res_norm_kernel,
        grid=(M // tm,),
        in_specs=in_specs,
        out_specs=pl.BlockSpec((tm, N), lambda i: (i, 0)),
        out_shape=jax.ShapeDtypeStruct((M, N), F32),
        compiler_params=_params(("parallel",)),
        name="matmul_res" if norm_gain is None else "matmul_res_norm",
    )(*args)


def _sb_kernel(q_ref, k_ref, v_ref, o_ref, acc_ref, spent_ref, *, chains, heads):
    row = lax.broadcasted_iota(jnp.int32, (BLOCK, BLOCK), 0)
    col = lax.broadcasted_iota(jnp.int32, (BLOCK, BLOCK), 1)
    causal = col < row
    suffix = (row >= col).astype(BF16)
    base = pl.program_id(2) * chains
    contract_last = (((1,), (1,)), ((), ()))

    def mask_first_block(x):
        head = jnp.where(causal, x[:BLOCK], 0.0)
        return head if x.shape[0] == BLOCK else jnp.concatenate([head, x[BLOCK:]], axis=0)

    def key_block(h, rows, kb, on_diagonal):
        start = pl.multiple_of(kb * BLOCK, BLOCK)
        k = k_ref[h, pl.ds(start, BLOCK), :]
        v = v_ref[h, pl.ds(start, BLOCK), :]
        z = lax.dot_general(q_ref[h, rows, :], k, contract_last, preferred_element_type=F32)
        drop = jnp.maximum(z, 0.0) + jnp.log(1.0 + jnp.exp2(jnp.abs(z) * -LOG2E))
        if on_diagonal:
            drop = mask_first_block(drop)
        upto = jnp.dot(drop.astype(BF16), suffix, preferred_element_type=F32)
        a = jnp.exp(z - (upto + spent_ref[h, rows, :]))
        if on_diagonal:
            a = mask_first_block(a)
        acc_ref[h, rows, :] += jnp.dot(a.astype(BF16), v, preferred_element_type=F32)
        spent_ref[h, rows, :] += upto[:, 0:1]

    def retire_if_no_keys(h, rows, kb):
        spent_ref[h, rows, :] = jnp.where(kb >= 0, spent_ref[h, rows, :], -NEG_INF)

    acc_ref[...] = jnp.zeros_like(acc_ref)
    spent_ref[...] = jnp.zeros_like(spent_ref)
    for first in range(chains - 1, -SB_STATIC_DEPTH, -1):
        lo, hi = max(first, 0), min(first + SB_STATIC_DEPTH - 1, chains - 1)
        rows = slice(lo * BLOCK, (hi + 1) * BLOCK)
        kb = base + first
        for h in range(heads):
            if first >= 0:
                key_block(h, rows, kb, True)
            else:
                retire_if_no_keys(h, rows, kb)
                key_block(h, rows, jnp.maximum(kb, 0), False)

    def cond(carry):
        _, least_spent = carry
        return least_spent < -SB_EXIT_LOG

    def body(carry):
        delta, _ = carry
        for c in range(chains):
            rows = slice(c * BLOCK, (c + 1) * BLOCK)
            kb = base + c - delta
            for h in range(heads):
                retire_if_no_keys(h, rows, kb)
                key_block(h, rows, jnp.maximum(kb, 0), False)
        return delta + 1, jnp.min(spent_ref[...])

    lax.while_loop(cond, body, (jnp.int32(SB_STATIC_DEPTH), jnp.min(spent_ref[...])))
    for h in range(heads):
        o_ref[:, h * HEAD_DIM:(h + 1) * HEAD_DIM] = acc_ref[h].astype(o_ref.dtype)


def _stick_breaking(qkv, B, S, tq, heads):
    H = SB_HEADS
    nq = S // tq
    hb = H // heads
    return pl.pallas_call(
        functools.partial(_sb_kernel, chains=tq // BLOCK, heads=heads),
        grid=(B, hb, nq),
        in_specs=[
            pl.BlockSpec((heads, tq, HEAD_DIM), lambda b, h, i: (h, b * nq + i, 0)),
            pl.BlockSpec((heads, S, HEAD_DIM), lambda b, h, i: (hb + h, b, 0)),
            pl.BlockSpec((heads, S, HEAD_DIM), lambda b, h, i: (2 * hb + h, b, 0)),
        ],
        out_specs=pl.BlockSpec((tq, heads * HEAD_DIM), lambda b, h, i: (b * nq + i, h)),
        out_shape=jax.ShapeDtypeStruct((B * S, H * HEAD_DIM), BF16),
        scratch_shapes=[pltpu.VMEM((heads, tq, HEAD_DIM), F32), pltpu.VMEM((heads, tq, BLOCK), F32)],
        compiler_params=_params(("parallel", "parallel", "arbitrary")),
        name="stick_breaking",
    )(qkv, qkv, qkv)


def _residue_major_permutation():
    i = np.arange(DIL_TILE)
    p = np.zeros((DIL_TILE, DIL_TILE), np.float32)
    p[i, (i % DIL_CHUNK) * DIL_RES + i // DIL_CHUNK] = 1.0
    return p


def _norm_matmul_perm_kernel(x_ref, g_ref, p_ref, w_ref, cs_ref, o_ref, hn_even_ref, hn_odd_ref, hnp_ref,
                             *, natural_tiles):
    i, j = pl.program_id(0), pl.program_id(1)

    def permute(hn_ref):
        for t in range(hn_ref.shape[0] // DIL_TILE):
            rows = slice(t * DIL_TILE, (t + 1) * DIL_TILE)
            hnp_ref[rows, :] = jnp.dot(p_ref[...], hn_ref[rows, :],
                                       preferred_element_type=F32).astype(hnp_ref.dtype)

    def project(lhs_ref):
        y = jnp.dot(lhs_ref[...], w_ref[...], preferred_element_type=F32) * cs_ref[...]
        o_ref[...] = y.astype(o_ref.dtype)

    def row_step(filling_ref, finished_ref):
        @pl.when(j == 0)
        def _():
            permute(finished_ref)

        @pl.when(j < natural_tiles)
        def _():
            _norm_share(x_ref, g_ref, filling_ref, 0, j)
            project(finished_ref)

        @pl.when(j >= natural_tiles)
        def _():
            _norm_share(x_ref, g_ref, filling_ref, 0, j)
            project(hnp_ref)

    @pl.when(i == 0)
    def _():
        _norm_share(x_ref, g_ref, hn_even_ref, 0, j)

    @pl.when(i % 2 == 1)
    def _():
        row_step(hn_odd_ref, hn_even_ref)

    @pl.when(jnp.logical_and(i % 2 == 0, i > 0))
    def _():
        row_step(hn_even_ref, hn_odd_ref)


def _norm_matmul_perm(x, g, w, col_scale, perm, natural_cols, tm, tn):
    M, K = x.shape
    N = w.shape[1]
    tiles, steps = M // tm, N // tn

    def col(i, j):
        return jnp.where(i > 0, j, 0)

    return pl.pallas_call(
        functools.partial(_norm_matmul_perm_kernel, natural_tiles=natural_cols // tn),
        grid=(tiles + 1, steps),
        in_specs=[
            _share_spec(tm, K, tiles, steps),
            pl.BlockSpec((1, K), lambda i, j: (0, 0)),
            pl.BlockSpec((DIL_TILE, DIL_TILE), lambda i, j: (0, 0)),
            pl.BlockSpec((K, tn), lambda i, j: (0, col(i, j))),
            pl.BlockSpec((1, tn), lambda i, j: (0, col(i, j))),
        ],
        out_specs=pl.BlockSpec((tm, tn), lambda i, j: (jnp.maximum(i - 1, 0), col(i, j))),
        out_shape=jax.ShapeDtypeStruct((M, N), BF16),
        scratch_shapes=[pltpu.VMEM((tm, K), BF16), pltpu.VMEM((tm, K), BF16), pltpu.VMEM((tm, K), BF16)],
        compiler_params=_params(("arbitrary", "arbitrary")),
        name="norm_matmul_perm",
    )(x, g.reshape(1, K), perm, w, col_scale.reshape(1, N))


def _rel_bucket_table(dil, order):
    order = np.asarray(order)
    qi = order[:, None]
    kc = np.concatenate([order, BLOCK + order])[None, :]
    rel = BLOCK + qi - kc
    n = jnp.asarray(np.maximum(rel, 0) * dil)
    max_exact = REL_BUCKETS // 2
    nf = jnp.maximum(n, 1).astype(F32)
    large = max_exact + (jnp.log(nf / max_exact) / math.log(REL_MAX_DISTANCE / max_exact)
                         * (REL_BUCKETS - max_exact)).astype(jnp.int32)
    large = jnp.minimum(large, REL_BUCKETS - 1)
    bucket = jnp.where(n < max_exact, n, large).astype(jnp.int32)
    return jnp.where(jnp.asarray((rel >= 0) & (rel <= BLOCK)), bucket, -1)


class _NaturalRows:
    @staticmethod
    def load(ref, j, cols):
        return ref[j * BLOCK:(j + 1) * BLOCK, cols]

    @staticmethod
    def load_prev(ref, cols):
        return ref[:, cols]

    @staticmethod
    def store(ref, j, cols, val):
        ref[j * BLOCK:(j + 1) * BLOCK, cols] = val


class _Residue16Rows:
    per = BLOCK // DIL_CHUNK

    @classmethod
    def load(cls, ref, j, cols):
        return jnp.concatenate([ref[cls.per * j + t, :, cols] for t in range(cls.per)], axis=0)

    @classmethod
    def load_prev(cls, ref, cols):
        return cls.load(ref, 0, cols)

    @classmethod
    def store(cls, ref, j, cols, val):
        for t in range(cls.per):
            ref[cls.per * j + t, :, cols] = val[t * DIL_CHUNK:(t + 1) * DIL_CHUNK]


class _Residue4Rows:
    per = BLOCK // 4 // DIL_CHUNK
    order = [4 * m + a for a in range(4) for m in range(BLOCK // 4)]

    @classmethod
    def load(cls, ref, j, cols):
        return jnp.concatenate([ref[cls.per * j + t, a, :, cols] for a in range(4) for t in range(cls.per)],
                               axis=0)

    @classmethod
    def load_prev(cls, ref, cols):
        return cls.load(ref, 0, cols)

    @classmethod
    def store(cls, ref, j, cols, val):
        for a in range(4):
            for t in range(cls.per):
                piece = (a * cls.per + t) * DIL_CHUNK
                ref[cls.per * j + t, a, :, cols] = val[piece:piece + DIL_CHUNK]


def _dil_kernel(bucket_ref, table_ref, q_ref, kc_ref, kp_ref, vc_ref, vp_ref, o_ref, lse_ref, bias_ref,
                *, sub_blocks, rows):
    first_step = jnp.logical_and(jnp.logical_and(pl.program_id(0) == 0, pl.program_id(1) == 0),
                                 pl.program_id(2) == 0)

    @pl.when(first_step)
    def _():
        bucket = bucket_ref[...]
        for h in range(DIL_GROUP_HEADS):
            b = jnp.full(bucket.shape, NEG_INF, F32)
            for t in range(REL_BUCKETS):
                b = jnp.where(bucket == t, table_ref[t, h], b)
            bias_ref[h] = b

    lane = lax.broadcasted_iota(jnp.int32, (BLOCK, 128), 1)
    keep_keys = jnp.logical_or(lax.broadcasted_iota(jnp.int32, (BLOCK, 2 * BLOCK), 1) >= BLOCK,
                               pl.program_id(2) > 0)
    ones = jnp.ones((2 * BLOCK, HEAD_DIM), BF16)
    contract_last = (((1,), (1,)), ((), ()))
    all_lanes = slice(None)

    def prev_and_cur(cur_ref, prev_ref, j, cols):
        prev = rows.load_prev(prev_ref, cols) if j == 0 else rows.load(cur_ref, j - 1, cols)
        return jnp.concatenate([prev, rows.load(cur_ref, j, cols)], axis=0)

    for j in range(sub_blocks):
        stats = jnp.zeros((BLOCK, 128), F32)
        for h in range(DIL_GROUP_HEADS):
            cols = slice(h * HEAD_DIM, (h + 1) * HEAD_DIM)
            q = rows.load(q_ref, j, cols)
            k = prev_and_cur(kc_ref, kp_ref, j, cols)
            v = prev_and_cur(vc_ref, vp_ref, j, cols)
            s = lax.dot_general(q, k, contract_last, preferred_element_type=F32) + bias_ref[h]
            if j == 0:
                s = jnp.where(keep_keys, s, NEG_INF)
            m = jnp.max(s, axis=-1, keepdims=True)
            p = jnp.exp(s - m).astype(BF16)
            ol = jnp.dot(p, jnp.concatenate([v, ones], axis=1), preferred_element_type=F32)
            o, l = ol[:, :HEAD_DIM], ol[:, HEAD_DIM:]
            rows.store(o_ref, j, cols, (o / l).astype(o_ref.dtype))
            stats = jnp.where(lane == h, m + jnp.log(l), stats)
        rows.store(lse_ref, j, all_lanes, stats)


def _dilated_group(qkv, rel_table, B, S, g, dil):
    W, C = DIL_WIDTH, qkv.shape[1]
    tiles = S // DIL_TILE
    col0 = g * 3
    if dil == 1:
        rows, n_res = _NaturalRows, 1
        blocks = min(DIL_BLOCKS_PER_STEP, S // BLOCK)
        steps = S // (blocks * BLOCK)
        views = [(B * S, C), (B * S, W), (B * S, 128)]
        cur_shape, prev_shape = (blocks * BLOCK,), (BLOCK,)
        cur_idx = lambda b, r, i: (b * steps + i,)
        prev_idx = lambda b, r, i: (b * (S // BLOCK) + jnp.maximum(i * blocks - 1, 0),)
    else:
        rows = _Residue16Rows if dil == DIL_RES else _Residue4Rows
        assert dil in (4, DIL_RES) and DIL_RES == 16
        per = rows.per
        t_step = min(DIL_BLOCKS_PER_STEP * per, tiles)
        blocks, steps = t_step // per, tiles // t_step
        if dil == DIL_RES:
            n_res = DIL_RES
            views = [(B * tiles, DIL_RES, DIL_CHUNK, c) for c in (C, W, 128)]
            cur_shape, prev_shape = (t_step, None, DIL_CHUNK), (per, None, DIL_CHUNK)
            cur_idx = lambda b, r, i: (b * steps + i, r, 0)
            prev_idx = lambda b, r, i: (b * (tiles // per) + jnp.maximum(i * blocks - 1, 0), r, 0)
        else:
            n_res = 4
            views = [(B * tiles, 4, 4, DIL_CHUNK, c) for c in (C, W, 128)]
            cur_shape, prev_shape = (t_step, 4, None, DIL_CHUNK), (per, 4, None, DIL_CHUNK)
            cur_idx = lambda b, r, i: (b * steps + i, 0, r, 0)
            prev_idx = lambda b, r, i: (b * (tiles // per) + jnp.maximum(i * blocks - 1, 0), 0, r, 0)

    def spec(shape, idx, width, col):
        return pl.BlockSpec(shape + (width,), lambda b, r, i: idx(b, r, i) + (col,))

    qkv_v = qkv.reshape(views[0])
    order = getattr(rows, "order", list(range(BLOCK)))
    o, lse = pl.pallas_call(
        functools.partial(_dil_kernel, sub_blocks=blocks, rows=rows),
        grid=(B, n_res, steps),
        in_specs=[
            pl.BlockSpec((BLOCK, 2 * BLOCK), lambda b, r, i: (0, 0)),
            pl.BlockSpec(memory_space=pltpu.SMEM),
            spec(cur_shape, cur_idx, W, col0),
            spec(cur_shape, cur_idx, W, col0 + 1), spec(prev_shape, prev_idx, W, col0 + 1),
            spec(cur_shape, cur_idx, W, col0 + 2), spec(prev_shape, prev_idx, W, col0 + 2),
        ],
        out_specs=[spec(cur_shape, cur_idx, W, 0), spec(cur_shape, cur_idx, 128, 0)],
        out_shape=[jax.ShapeDtypeStruct(views[1], BF16), jax.ShapeDtypeStruct(views[2], F32)],
        scratch_shapes=[pltpu.VMEM((DIL_GROUP_HEADS, BLOCK, 2 * BLOCK), F32)],
        compiler_params=_params(("arbitrary", "arbitrary", "arbitrary")),
        name=f"dilated_group{g}",
    )(_rel_bucket_table(dil, order), rel_table, qkv_v, qkv_v, qkv_v, qkv_v, qkv_v)
    return o.reshape(B * S, W), lse.reshape(B * S, 128)


def _bf16_pieces(x):
    hi = x.astype(BF16)
    rest = x - hi.astype(F32)
    mid = rest.astype(BF16)
    return [hi, mid, (rest - mid.astype(F32)).astype(BF16)]


def _combine_matmul_res_kernel(o0_ref, o1_ref, o2_ref, l0_ref, l1_ref, l2_ref, pt_ref, w_ref, r_ref, out_ref,
                               comb_ref):
    pt = pt_ref[...]
    for t in range(out_ref.shape[0] // DIL_TILE):
        rows = slice(t * DIL_TILE, (t + 1) * DIL_TILE)
        pieces = jnp.concatenate(_bf16_pieces(l1_ref[rows, :]) + _bf16_pieces(l2_ref[rows, :]), axis=1)
        lse = jnp.dot(pt, pieces, preferred_element_type=F32)
        l0 = l0_ref[rows, :]
        l1 = lse[:, 0:128] + lse[:, 128:256] + lse[:, 256:384]
        l2 = lse[:, 384:512] + lse[:, 512:640] + lse[:, 640:768]
        mx = jnp.maximum(jnp.maximum(l0, l1), l2)
        e0, e1, e2 = jnp.exp(l0 - mx), jnp.exp(l1 - mx), jnp.exp(l2 - mx)
        den = e0 + e1 + e2
        w0, w1, w2 = e0 / den, e1 / den, e2 / den
        o1 = jnp.dot(pt, o1_ref[rows, :], preferred_element_type=F32)
        o2 = jnp.dot(pt, o2_ref[rows, :], preferred_element_type=F32)
        for h in range(DIL_GROUP_HEADS):
            cols = slice(h * HEAD_DIM, (h + 1) * HEAD_DIM)
            c = (w0[:, h:h + 1] * o0_ref[rows, cols].astype(F32)
                 + w1[:, h:h + 1] * o1[:, cols] + w2[:, h:h + 1] * o2[:, cols])
            comb_ref[rows, cols] = c.astype(comb_ref.dtype)
    out_ref[...] = r_ref[...] + jnp.dot(comb_ref[...], w_ref[...], preferred_element_type=F32)


def _combine_matmul_res(os, lses, unperm, w, res, tm):
    M, K = os[0].shape
    N = w.shape[1]
    o_spec = pl.BlockSpec((tm, K), lambda i: (i, 0))
    l_spec = pl.BlockSpec((tm, 128), lambda i: (i, 0))
    row_block = pl.BlockSpec((tm, N), lambda i: (i, 0))
    return pl.pallas_call(
        _combine_matmul_res_kernel,
        grid=(M // tm,),
        in_specs=[o_spec, o_spec, o_spec, l_spec, l_spec, l_spec,
                  pl.BlockSpec((DIL_TILE, DIL_TILE), lambda i: (0, 0)),
                  pl.BlockSpec((K, N), lambda i: (0, 0), pipeline_mode=pl.Buffered(1)),
                  row_block],
        out_specs=row_block,
        out_shape=jax.ShapeDtypeStruct((M, N), F32),
        scratch_shapes=[pltpu.VMEM((tm, K), BF16)],
        compiler_params=_params(("parallel",)),
        name="combine_matmul_res",
    )(*os, *lses, unperm, w, res)


def _ffn_up_kernel(x_ref, g_ref, wg_ref, wv_ref, cw_ref, cb_ref, o_ref, hn_even_ref, hn_odd_ref,
                   *, tiles_per_seq):
    i, j = pl.program_id(0), pl.program_id(1)
    tm = hn_even_ref.shape[0] - CONV_HALO

    def gated(hn_ref):
        gate = jnp.dot(hn_ref[...], wg_ref[...], preferred_element_type=F32)
        val = jnp.dot(hn_ref[CONV_HALO:, :], wv_ref[...], preferred_element_type=F32)
        conv = cb_ref[...] + gate[CONV_HALO:] * cw_ref[CONV_WIDTH - 1:CONV_WIDTH, :]
        for back in range(1, CONV_WIDTH):
            tap = CONV_WIDTH - 1 - back
            conv = conv + pltpu.roll(gate, back, axis=0)[CONV_HALO:] * cw_ref[tap:tap + 1, :]
        act = conv / (1.0 + jnp.exp(-conv)) * val
        o_ref[...] = act.astype(o_ref.dtype)

    def row_step(filling_ref, finished_ref):
        tail = finished_ref[tm:tm + CONV_HALO, :]
        filling_ref[0:CONV_HALO, :] = jnp.where(i % tiles_per_seq == 0, jnp.zeros_like(tail), tail)
        _norm_share(x_ref, g_ref, filling_ref, CONV_HALO, j)
        gated(finished_ref)

    @pl.when(i == 0)
    def _():
        hn_even_ref[0:CONV_HALO, :] = jnp.zeros((CONV_HALO, hn_even_ref.shape[1]), hn_even_ref.dtype)
        _norm_share(x_ref, g_ref, hn_even_ref, CONV_HALO, j)

    @pl.when(i % 2 == 1)
    def _():
        row_step(hn_odd_ref, hn_even_ref)

    @pl.when(jnp.logical_and(i % 2 == 0, i > 0))
    def _():
        row_step(hn_even_ref, hn_odd_ref)


def _ffn_up(x, g, w_up, conv_w, conv_b, S, tm, tf):
    M, K = x.shape
    d_ff = w_up.shape[1] // 2
    tiles, steps = M // tm, d_ff // tf

    def col(i, j):
        return jnp.where(i > 0, j, 0)

    return pl.pallas_call(
        functools.partial(_ffn_up_kernel, tiles_per_seq=S // tm),
        grid=(tiles + 1, steps),
        in_specs=[
            _share_spec(tm, K, tiles, steps),
            pl.BlockSpec((1, K), lambda i, j: (0, 0)),
            pl.BlockSpec((K, tf), lambda i, j: (0, col(i, j))),
            pl.BlockSpec((K, tf), lambda i, j: (0, steps + col(i, j))),
            pl.BlockSpec((CONV_WIDTH, tf), lambda i, j: (0, col(i, j))),
            pl.BlockSpec((1, tf), lambda i, j: (0, col(i, j))),
        ],
        out_specs=pl.BlockSpec((tm, tf), lambda i, j: (jnp.maximum(i - 1, 0), col(i, j))),
        out_shape=jax.ShapeDtypeStruct((M, d_ff), BF16),
        scratch_shapes=[pltpu.VMEM((tm + CONV_HALO, K), BF16), pltpu.VMEM((tm + CONV_HALO, K), BF16)],
        compiler_params=_params(("arbitrary", "arbitrary")),
        name="ffn_up",
    )(x, g.reshape(1, K), w_up, w_up, conv_w, conv_b.reshape(1, d_ff))


def _score_scale_on_q(n_cols, q_starts, q_width):
    cs = np.ones((n_cols,), np.float32)
    for start in q_starts:
        cs[start:start + q_width] = HEAD_DIM ** -0.5
    return jnp.asarray(cs)


def kernel(x, ln_mix, ln_ffn, ln_f, w_qkv_sb, w_o_sb, w_qkv_dil, w_o_dil, rel_bias, w_up, conv_w, conv_b, w_down):
    B, S, D = x.shape
    depth = ln_mix.shape[0]
    assert S % PROJ_ROWS == 0 and S % min(SB_ROWS, S) == 0 and S % (DIL_RES * BLOCK) == 0
    h = x.reshape(B * S, D)
    for i in range(depth):
        j = i // N_MIXERS
        if i % N_MIXERS == 0:
            q_scale = _score_scale_on_q(w_qkv_sb.shape[2], [0], SB_HEADS * HEAD_DIM)
            qkv = _norm_matmul_heads(h, ln_mix[i], w_qkv_sb[j].astype(BF16), q_scale, tm=PROJ_ROWS, tn=QKV_COLS)
            o = _stick_breaking(qkv, B, S, tq=min(SB_ROWS, S), heads=SB_HEADS_PER_STEP)
            h = _matmul_res(o, w_o_sb[j].astype(BF16), h, tm=RES_ROWS)
        else:
            perm = _residue_major_permutation()
            q_scale = _score_scale_on_q(w_qkv_dil.shape[2], [g * 3 * DIL_WIDTH for g in range(DIL_GROUPS)], DIL_WIDTH)
            qkv = _norm_matmul_perm(h, ln_mix[i], w_qkv_dil[j].astype(BF16), q_scale, jnp.asarray(perm, BF16),
                                    natural_cols=3 * DIL_WIDTH, tm=PROJ_ROWS, tn=QKV_COLS)
            os, lses = [], []
            for g, (window, dil) in enumerate(DIL_PATTERNS):
                assert window // dil == BLOCK and S % (dil * BLOCK) == 0
                table = rel_bias[:, g * DIL_GROUP_HEADS:(g + 1) * DIL_GROUP_HEADS]
                o, lse = _dilated_group(qkv, table, B, S, g, dil)
                os.append(o)
                lses.append(lse)
            h = _combine_matmul_res(os, lses, jnp.asarray(perm.T, BF16), w_o_dil[j].astype(BF16), h, tm=RES_ROWS)
        act = _ffn_up(h, ln_ffn[i], w_up[i].astype(BF16), conv_w[i], conv_b[i], S, tm=PROJ_ROWS, tf=FFN_COLS)
        h = _matmul_res(act, w_down[i].astype(BF16), h, tm=RES_ROWS, norm_gain=ln_f if i == depth - 1 else None)
    return h.reshape(B, S, D)
```

```python
import functools
import math

import jax
import jax.numpy as jnp
import numpy as np
from jax import lax
from jax.experimental import pallas as pl
from jax.experimental.pallas import tpu as pltpu

BLOCK = 128
SB_HEADS = 16
HEAD_DIM = 128
DIL_PATTERNS = ((128, 1), (512, 4), (2048, 16))
DIL_GROUPS = len(DIL_PATTERNS)
DIL_GROUP_HEADS = 8
DIL_WIDTH = DIL_GROUP_HEADS * HEAD_DIM
REL_BUCKETS = 32
REL_MAX_DISTANCE = 2048
CONV_WIDTH = 3
RMS_EPS = 1e-6
NEG_INF = -1e30
LOG2E = math.log2(math.e)
N_MIXERS = 2

SB_EXIT_LOG = -104.0
SB_STATIC_DEPTH = 3

BF16_ROWS = 16
DIL_RES = 16
DIL_CHUNK = BF16_ROWS
DIL_TILE = DIL_RES * DIL_CHUNK
CONV_HALO = BF16_ROWS
NORM_CHUNK = 64
V7X_VMEM_BYTES = 64 * 1024 * 1024
VMEM_LIMIT = V7X_VMEM_BYTES * 7 // 8

PROJ_ROWS = 1024
QKV_COLS = 1536
FFN_COLS = 1024
RES_ROWS = 512
SB_ROWS = 4096
SB_HEADS_PER_STEP = 2
DIL_BLOCKS_PER_STEP = 16

BF16 = jnp.bfloat16
F32 = jnp.float32


def _params(semantics):
    return pltpu.CompilerParams(dimension_semantics=semantics, vmem_limit_bytes=VMEM_LIMIT)


def _rms_rows(x, g):
    ms = jnp.mean(x * x, axis=-1, keepdims=True)
    return x * lax.rsqrt(ms + RMS_EPS) * g


def _norm_share(x_ref, g_ref, dst_ref, dst_row0, step):
    rows = x_ref.shape[0]
    shares = (dst_ref.shape[0] - dst_row0) // rows
    base = dst_row0 + jnp.minimum(step, shares - 1) * rows
    g = g_ref[...]
    for c in range(rows // NORM_CHUNK):
        y = _rms_rows(x_ref[c * NORM_CHUNK:(c + 1) * NORM_CHUNK, :], g)
        row = pl.multiple_of(base + c * NORM_CHUNK, BF16_ROWS)
        dst_ref[pl.ds(row, NORM_CHUNK), :] = y.astype(dst_ref.dtype)


def _row_shares(steps):
    return 1 << (steps.bit_length() - 1)


def _share_spec(tm, K, tiles, steps):
    shares = _row_shares(steps)
    return pl.BlockSpec((tm // shares, K),
                        lambda i, j: (jnp.minimum(i, tiles - 1) * shares + jnp.minimum(j, shares - 1), 0))


def _norm_matmul_kernel(x_ref, g_ref, w_ref, cs_ref, o_ref, hn_even_ref, hn_odd_ref):
    i, j = pl.program_id(0), pl.program_id(1)

    def project(hn_ref):
        y = (jnp.dot(hn_ref[...], w_ref[...], preferred_element_type=F32) * cs_ref[...]).astype(o_ref.dtype)
        for c in range(o_ref.shape[0]):
            o_ref[c] = y[:, c * HEAD_DIM:(c + 1) * HEAD_DIM]

    @pl.when(i == 0)
    def _():
        _norm_share(x_ref, g_ref, hn_even_ref, 0, j)

    @pl.when(i % 2 == 1)
    def _():
        _norm_share(x_ref, g_ref, hn_odd_ref, 0, j)
        project(hn_even_ref)

    @pl.when(jnp.logical_and(i % 2 == 0, i > 0))
    def _():
        _norm_share(x_ref, g_ref, hn_even_ref, 0, j)
        project(hn_odd_ref)


def _norm_matmul_heads(x, g, w, col_scale, tm, tn):
    M, K = x.shape
    N = w.shape[1]
    slabs = tn // HEAD_DIM
    tiles, steps = M // tm, N // tn

    def col(i, j):
        return jnp.where(i > 0, j, 0)

    return pl.pallas_call(
        _norm_matmul_kernel,
        grid=(tiles + 1, steps),
        in_specs=[
            _share_spec(tm, K, tiles, steps),
            pl.BlockSpec((1, K), lambda i, j: (0, 0)),
            pl.BlockSpec((K, tn), lambda i, j: (0, col(i, j))),
            pl.BlockSpec((1, tn), lambda i, j: (0, col(i, j))),
        ],
        out_specs=pl.BlockSpec((slabs, tm, HEAD_DIM), lambda i, j: (col(i, j), jnp.maximum(i - 1, 0), 0)),
        out_shape=jax.ShapeDtypeStruct((N // HEAD_DIM, M, HEAD_DIM), BF16),
        scratch_shapes=[pltpu.VMEM((tm, K), BF16), pltpu.VMEM((tm, K), BF16)],
        compiler_params=_params(("arbitrary", "arbitrary")),
        name="norm_matmul",
    )(x, g.reshape(1, K), w, col_scale.reshape(1, N))


def _matmul_res_kernel(a_ref, w_ref, r_ref, o_ref):
    o_ref[...] = r_ref[...] + jnp.dot(a_ref[...], w_ref[...], preferred_element_type=F32)


def _matmul_res_norm_kernel(a_ref, w_ref, r_ref, g_ref, o_ref):
    y = r_ref[...] + jnp.dot(a_ref[...], w_ref[...], preferred_element_type=F32)
    o_ref[...] = _rms_rows(y, g_ref[...])


def _matmul_res(a, w, res, tm, norm_gain=None):
    M, K = a.shape
    N = w.shape[1]
    in_specs = [
        pl.BlockSpec((tm, K), lambda i: (i, 0)),
        pl.BlockSpec((K, N), lambda i: (0, 0), pipeline_mode=pl.Buffered(1)),
        pl.BlockSpec((tm, N), lambda i: (i, 0)),
    ]
    args = [a, w, res]
    if norm_gain is not None:
        in_specs.append(pl.BlockSpec((1, N), lambda i: (0, 0)))
        args.append(norm_gain.reshape(1, N))
    return pl.pallas_call(
        _matmul_res_kernel if norm_gain is None else _matmul_res_norm_kernel,
        grid=(M // tm,),
        in_specs=in_specs,
        out_specs=pl.BlockSpec((tm, N), lambda i: (i, 0)),
        out_shape=jax.ShapeDtypeStruct((M, N), F32),
        compiler_params=_params(("parallel",)),
        name="matmul_res" if norm_gain is None else "matmul_res_norm",
    )(*args)


def _sb_kernel(q_ref, k_ref, v_ref, o_ref, acc_ref, spent_ref, *, chains, heads):
    row = lax.broadcasted_iota(jnp.int32, (BLOCK, BLOCK), 0)
    col = lax.broadcasted_iota(jnp.int32, (BLOCK, BLOCK), 1)
    causal = col < row
    suffix = (row >= col).astype(BF16)
    base = pl.program_id(2) * chains
    contract_last = (((1,), (1,)), ((), ()))

    def mask_first_block(x):
        head = jnp.where(causal, x[:BLOCK], 0.0)
        return head if x.shape[0] == BLOCK else jnp.concatenate([head, x[BLOCK:]], axis=0)

    def key_block(h, rows, kb, on_diagonal):
        start = pl.multiple_of(kb * BLOCK, BLOCK)
        k = k_ref[h, pl.ds(start, BLOCK), :]
        v = v_ref[h, pl.ds(start, BLOCK), :]
        z = lax.dot_general(q_ref[h, rows, :], k, contract_last, preferred_element_type=F32)
        drop = jnp.maximum(z, 0.0) + jnp.log(1.0 + jnp.exp2(jnp.abs(z) * -LOG2E))
        if on_diagonal:
            drop = mask_first_block(drop)
        upto = jnp.dot(drop.astype(BF16), suffix, preferred_element_type=F32)
        a = jnp.exp(z - (upto + spent_ref[h, rows, :]))
        if on_diagonal:
            a = mask_first_block(a)
        acc_ref[h, rows, :] += jnp.dot(a.astype(BF16), v, preferred_element_type=F32)
        spent_ref[h, rows, :] += upto[:, 0:1]

    def retire_if_no_keys(h, rows, kb):
        spent_ref[h, rows, :] = jnp.where(kb >= 0, spent_ref[h, rows, :], -NEG_INF)

    acc_ref[...] = jnp.zeros_like(acc_ref)
    spent_ref[...] = jnp.zeros_like(spent_ref)
    for first in range(chains - 1, -SB_STATIC_DEPTH, -1):
        lo, hi = max(first, 0), min(first + SB_STATIC_DEPTH - 1, chains - 1)
        rows = slice(lo * BLOCK, (hi + 1) * BLOCK)
        kb = base + first
        for h in range(heads):
            if first >= 0:
                key_block(h, rows, kb, True)
            else:
                retire_if_no_keys(h, rows, kb)
                key_block(h, rows, jnp.maximum(kb, 0), False)

    def cond(carry):
        _, least_spent = carry
        return least_spent < -SB_EXIT_LOG

    def body(carry):
        delta, _ = carry
        for c in range(chains):
            rows = slice(c * BLOCK, (c + 1) * BLOCK)
            kb = base + c - delta
            for h in range(heads):
                retire_if_no_keys(h, rows, kb)
                key_block(h, rows, jnp.maximum(kb, 0), False)
        return delta + 1, jnp.min(spent_ref[...])

    lax.while_loop(cond, body, (jnp.int32(SB_STATIC_DEPTH), jnp.min(spent_ref[...])))
    for h in range(heads):
        o_ref[:, h * HEAD_DIM:(h + 1) * HEAD_DIM] = acc_ref[h].astype(o_ref.dtype)


def _stick_breaking(qkv, B, S, tq, heads):
    H = SB_HEADS
    nq = S // tq
    hb = H // heads
    return pl.pallas_call(
        functools.partial(_sb_kernel, chains=tq // BLOCK, heads=heads),
        grid=(B, hb, nq),
        in_specs=[
            pl.BlockSpec((heads, tq, HEAD_DIM), lambda b, h, i: (h, b * nq + i, 0)),
            pl.BlockSpec((heads, S, HEAD_DIM), lambda b, h, i: (hb + h, b, 0)),
            pl.BlockSpec((heads, S, HEAD_DIM), lambda b, h, i: (2 * hb + h, b, 0)),
        ],
        out_specs=pl.BlockSpec((tq, heads * HEAD_DIM), lambda b, h, i: (b * nq + i, h)),
        out_shape=jax.ShapeDtypeStruct((B * S, H * HEAD_DIM), BF16),
        scratch_shapes=[pltpu.VMEM((heads, tq, HEAD_DIM), F32), pltpu.VMEM((heads, tq, BLOCK), F32)],
        compiler_params=_params(("parallel", "parallel", "arbitrary")),
        name="stick_breaking",
    )(qkv, qkv, qkv)


def _residue_major_permutation():
    i = np.arange(DIL_TILE)
    p = np.zeros((DIL_TILE, DIL_TILE), np.float32)
    p[i, (i % DIL_CHUNK) * DIL_RES + i // DIL_CHUNK] = 1.0
    return p


def _norm_matmul_perm_kernel(x_ref, g_ref, p_ref, w_ref, cs_ref, o_ref, hn_even_ref, hn_odd_ref, hnp_ref,
                             *, natural_tiles):
    i, j = pl.program_id(0), pl.program_id(1)

    def permute(hn_ref):
        for t in range(hn_ref.shape[0] // DIL_TILE):
            rows = slice(t * DIL_TILE, (t + 1) * DIL_TILE)
            hnp_ref[rows, :] = jnp.dot(p_ref[...], hn_ref[rows, :],
                                       preferred_element_type=F32).astype(hnp_ref.dtype)

    def project(lhs_ref):
        y = jnp.dot(lhs_ref[...], w_ref[...], preferred_element_type=F32) * cs_ref[...]
        o_ref[...] = y.astype(o_ref.dtype)

    def row_step(filling_ref, finished_ref):
        @pl.when(j == 0)
        def _():
            permute(finished_ref)

        @pl.when(j < natural_tiles)
        def _():
            _norm_share(x_ref, g_ref, filling_ref, 0, j)
            project(finished_ref)

        @pl.when(j >= natural_tiles)
        def _():
            _norm_share(x_ref, g_ref, filling_ref, 0, j)
            project(hnp_ref)

    @pl.when(i == 0)
    def _():
        _norm_share(x_ref, g_ref, hn_even_ref, 0, j)

    @pl.when(i % 2 == 1)
    def _():
        row_step(hn_odd_ref, hn_even_ref)

    @pl.when(jnp.logical_and(i % 2 == 0, i > 0))
    def _():
        row_step(hn_even_ref, hn_odd_ref)


def _norm_matmul_perm(x, g, w, col_scale, perm, natural_cols, tm, tn):
    M, K = x.shape
    N = w.shape[1]
    tiles, steps = M // tm, N // tn

    def col(i, j):
        return jnp.where(i > 0, j, 0)

    return pl.pallas_call(
        functools.partial(_norm_matmul_perm_kernel, natural_tiles=natural_cols // tn),
        grid=(tiles + 1, steps),
        in_specs=[
            _share_spec(tm, K, tiles, steps),
            pl.BlockSpec((1, K), lambda i, j: (0, 0)),
            pl.BlockSpec((DIL_TILE, DIL_TILE), lambda i, j: (0, 0)),
            pl.BlockSpec((K, tn), lambda i, j: (0, col(i, j))),
            pl.BlockSpec((1, tn), lambda i, j: (0, col(i, j))),
        ],
        out_specs=pl.BlockSpec((tm, tn), lambda i, j: (jnp.maximum(i - 1, 0), col(i, j))),
        out_shape=jax.ShapeDtypeStruct((M, N), BF16),
        scratch_shapes=[pltpu.VMEM((tm, K), BF16), pltpu.VMEM((tm, K), BF16), pltpu.VMEM((tm, K), BF16)],
        compiler_params=_params(("arbitrary", "arbitrary")),
        name="norm_matmul_perm",
    )(x, g.reshape(1, K), perm, w, col_scale.reshape(1, N))


def _rel_bucket_table(dil, order):
    order = np.asarray(order)
    qi = order[:, None]
    kc = np.concatenate([order, BLOCK + order])[None, :]
    rel = BLOCK + qi - kc
    n = jnp.asarray(np.maximum(rel, 0) * dil)
    max_exact = REL_BUCKETS // 2
    nf = jnp.maximum(n, 1).astype(F32)
    large = max_exact + (jnp.log(nf / max_exact) / math.log(REL_MAX_DISTANCE / max_exact)
                         * (REL_BUCKETS - max_exact)).astype(jnp.int32)
    large = jnp.minimum(large, REL_BUCKETS - 1)
    bucket = jnp.where(n < max_exact, n, large).astype(jnp.int32)
    return jnp.where(jnp.asarray((rel >= 0) & (rel <= BLOCK)), bucket, -1)


class _NaturalRows:
    @staticmethod
    def load(ref, j, cols):
        return ref[j * BLOCK:(j + 1) * BLOCK, cols]

    @staticmethod
    def load_prev(ref, cols):
        return ref[:, cols]

    @staticmethod
    def store(ref, j, cols, val):
        ref[j * BLOCK:(j + 1) * BLOCK, cols] = val


class _Residue16Rows:
    per = BLOCK // DIL_CHUNK

    @classmethod
    def load(cls, ref, j, cols):
        return jnp.concatenate([ref[cls.per * j + t, :, cols] for t in range(cls.per)], axis=0)

    @classmethod
    def load_prev(cls, ref, cols):
        return cls.load(ref, 0, cols)

    @classmethod
    def store(cls, ref, j, cols, val):
        for t in range(cls.per):
            ref[cls.per * j + t, :, cols] = val[t * DIL_CHUNK:(t + 1) * DIL_CHUNK]


class _Residue4Rows:
    per = BLOCK // 4 // DIL_CHUNK
    order = [4 * m + a for a in range(4) for m in range(BLOCK // 4)]

    @classmethod
    def load(cls, ref, j, cols):
        return jnp.concatenate([ref[cls.per * j + t, a, :, cols] for a in range(4) for t in range(cls.per)],
                               axis=0)

    @classmethod
    def load_prev(cls, ref, cols):
        return cls.load(ref, 0, cols)

    @classmethod
    def store(cls, ref, j, cols, val):
        for a in range(4):
            for t in range(cls.per):
                piece = (a * cls.per + t) * DIL_CHUNK
                ref[cls.per * j + t, a, :, cols] = val[piece:piece + DIL_CHUNK]


def _dil_kernel(bucket_ref, table_ref, q_ref, kc_ref, kp_ref, vc_ref, vp_ref, o_ref, lse_ref, bias_ref,
                *, sub_blocks, rows):
    first_step = jnp.logical_and(jnp.logical_and(pl.program_id(0) == 0, pl.program_id(1) == 0),
                                 pl.program_id(2) == 0)

    @pl.when(first_step)
    def _():
        bucket = bucket_ref[...]
        for h in range(DIL_GROUP_HEADS):
            b = jnp.full(bucket.shape, NEG_INF, F32)
            for t in range(REL_BUCKETS):
                b = jnp.where(bucket == t, table_ref[t, h], b)
            bias_ref[h] = b

    lane = lax.broadcasted_iota(jnp.int32, (BLOCK, 128), 1)
    keep_keys = jnp.logical_or(lax.broadcasted_iota(jnp.int32, (BLOCK, 2 * BLOCK), 1) >= BLOCK,
                               pl.program_id(2) > 0)
    ones = jnp.ones((2 * BLOCK, HEAD_DIM), BF16)
    contract_last = (((1,), (1,)), ((), ()))
    all_lanes = slice(None)

    def prev_and_cur(cur_ref, prev_ref, j, cols):
        prev = rows.load_prev(prev_ref, cols) if j == 0 else rows.load(cur_ref, j - 1, cols)
        return jnp.concatenate([prev, rows.load(cur_ref, j, cols)], axis=0)

    for j in range(sub_blocks):
        stats = jnp.zeros((BLOCK, 128), F32)
        for h in range(DIL_GROUP_HEADS):
            cols = slice(h * HEAD_DIM, (h + 1) * HEAD_DIM)
            q = rows.load(q_ref, j, cols)
            k = prev_and_cur(kc_ref, kp_ref, j, cols)
            v = prev_and_cur(vc_ref, vp_ref, j, cols)
            s = lax.dot_general(q, k, contract_last, preferred_element_type=F32) + bias_ref[h]
            if j == 0:
                s = jnp.where(keep_keys, s, NEG_INF)
            m = jnp.max(s, axis=-1, keepdims=True)
            p = jnp.exp(s - m).astype(BF16)
            ol = jnp.dot(p, jnp.concatenate([v, ones], axis=1), preferred_element_type=F32)
            o, l = ol[:, :HEAD_DIM], ol[:, HEAD_DIM:]
            rows.store(o_ref, j, cols, (o / l).astype(o_ref.dtype))
            stats = jnp.where(lane == h, m + jnp.log(l), stats)
        rows.store(lse_ref, j, all_lanes, stats)


def _dilated_group(qkv, rel_table, B, S, g, dil):
    W, C = DIL_WIDTH, qkv.shape[1]
    tiles = S // DIL_TILE
    col0 = g * 3
    if dil == 1:
        rows, n_res = _NaturalRows, 1
        blocks = min(DIL_BLOCKS_PER_STEP, S // BLOCK)
        steps = S // (blocks * BLOCK)
        views = [(B * S, C), (B * S, W), (B * S, 128)]
        cur_shape, prev_shape = (blocks * BLOCK,), (BLOCK,)
        cur_idx = lambda b, r, i: (b * steps + i,)
        prev_idx = lambda b, r, i: (b * (S // BLOCK) + jnp.maximum(i * blocks - 1, 0),)
    else:
        rows = _Residue16Rows if dil == DIL_RES else _Residue4Rows
        assert dil in (4, DIL_RES) and DIL_RES == 16
        per = rows.per
        t_step = min(DIL_BLOCKS_PER_STEP * per, tiles)
        blocks, steps = t_step // per, tiles // t_step
        if dil == DIL_RES:
            n_res = DIL_RES
            views = [(B * tiles, DIL_RES, DIL_CHUNK, c) for c in (C, W, 128)]
            cur_shape, prev_shape = (t_step, None, DIL_CHUNK), (per, None, DIL_CHUNK)
            cur_idx = lambda b, r, i: (b * steps + i, r, 0)
            prev_idx = lambda b, r, i: (b * (tiles // per) + jnp.maximum(i * blocks - 1, 0), r, 0)
        else:
            n_res = 4
            views = [(B * tiles, 4, 4, DIL_CHUNK, c) for c in (C, W, 128)]
            cur_shape, prev_shape = (t_step, 4, None, DIL_CHUNK), (per, 4, None, DIL_CHUNK)
            cur_idx = lambda b, r, i: (b * steps + i, 0, r, 0)
            prev_idx = lambda b, r, i: (b * (tiles // per) + jnp.maximum(i * blocks - 1, 0), 0, r, 0)

    def spec(shape, idx, width, col):
        return pl.BlockSpec(shape + (width,), lambda b, r, i: idx(b, r, i) + (col,))

    qkv_v = qkv.reshape(views[0])
    order = getattr(rows, "order", list(range(BLOCK)))
    o, lse = pl.pallas_call(
        functools.partial(_dil_kernel, sub_blocks=blocks, rows=rows),
        grid=(B, n_res, steps),
        in_specs=[
            pl.BlockSpec((BLOCK, 2 * BLOCK), lambda b, r, i: (0, 0)),
            pl.BlockSpec(memory_space=pltpu.SMEM),
            spec(cur_shape, cur_idx, W, col0),
            spec(cur_shape, cur_idx, W, col0 + 1), spec(prev_shape, prev_idx, W, col0 + 1),
            spec(cur_shape, cur_idx, W, col0 + 2), spec(prev_shape, prev_idx, W, col0 + 2),
        ],
        out_specs=[spec(cur_shape, cur_idx, W, 0), spec(cur_shape, cur_idx, 128, 0)],
        out_shape=[jax.ShapeDtypeStruct(views[1], BF16), jax.ShapeDtypeStruct(views[2], F32)],
        scratch_shapes=[pltpu.VMEM((DIL_GROUP_HEADS, BLOCK, 2 * BLOCK), F32)],
        compiler_params=_params(("arbitrary", "arbitrary", "arbitrary")),
        name=f"dilated_group{g}",
    )(_rel_bucket_table(dil, order), rel_table, qkv_v, qkv_v, qkv_v, qkv_v, qkv_v)
    return o.reshape(B * S, W), lse.reshape(B * S, 128)


def _bf16_pieces(x):
    hi = x.astype(BF16)
    rest = x - hi.astype(F32)
    mid = rest.astype(BF16)
    return [hi, mid, (rest - mid.astype(F32)).astype(BF16)]


def _combine_matmul_res_kernel(o0_ref, o1_ref, o2_ref, l0_ref, l1_ref, l2_ref, pt_ref, w_ref, r_ref, out_ref,
                               comb_ref):
    pt = pt_ref[...]
    for t in range(out_ref.shape[0] // DIL_TILE):
        rows = slice(t * DIL_TILE, (t + 1) * DIL_TILE)
        pieces = jnp.concatenate(_bf16_pieces(l1_ref[rows, :]) + _bf16_pieces(l2_ref[rows, :]), axis=1)
        lse = jnp.dot(pt, pieces, preferred_element_type=F32)
        l0 = l0_ref[rows, :]
        l1 = lse[:, 0:128] + lse[:, 128:256] + lse[:, 256:384]
        l2 = lse[:, 384:512] + lse[:, 512:640] + lse[:, 640:768]
        mx = jnp.maximum(jnp.maximum(l0, l1), l2)
        e0, e1, e2 = jnp.exp(l0 - mx), jnp.exp(l1 - mx), jnp.exp(l2 - mx)
        den = e0 + e1 + e2
        w0, w1, w2 = e0 / den, e1 / den, e2 / den
        o1 = jnp.dot(pt, o1_ref[rows, :], preferred_element_type=F32)
        o2 = jnp.dot(pt, o2_ref[rows, :], preferred_element_type=F32)
        for h in range(DIL_GROUP_HEADS):
            cols = slice(h * HEAD_DIM, (h + 1) * HEAD_DIM)
            c = (w0[:, h:h + 1] * o0_ref[rows, cols].astype(F32)
                 + w1[:, h:h + 1] * o1[:, cols] + w2[:, h:h + 1] * o2[:, cols])
            comb_ref[rows, cols] = c.astype(comb_ref.dtype)
    out_ref[...] = r_ref[...] + jnp.dot(comb_ref[...], w_ref[...], preferred_element_type=F32)


def _combine_matmul_res(os, lses, unperm, w, res, tm):
    M, K = os[0].shape
    N = w.shape[1]
    o_spec = pl.BlockSpec((tm, K), lambda i: (i, 0))
    l_spec = pl.BlockSpec((tm, 128), lambda i: (i, 0))
    row_block = pl.BlockSpec((tm, N), lambda i: (i, 0))
    return pl.pallas_call(
        _combine_matmul_res_kernel,
        grid=(M // tm,),
        in_specs=[o_spec, o_spec, o_spec, l_spec, l_spec, l_spec,
                  pl.BlockSpec((DIL_TILE, DIL_TILE), lambda i: (0, 0)),
                  pl.BlockSpec((K, N), lambda i: (0, 0), pipeline_mode=pl.Buffered(1)),
                  row_block],
        out_specs=row_block,
        out_shape=jax.ShapeDtypeStruct((M, N), F32),
        scratch_shapes=[pltpu.VMEM((tm, K), BF16)],
        compiler_params=_params(("parallel",)),
        name="combine_matmul_res",
    )(*os, *lses, unperm, w, res)


def _ffn_up_kernel(x_ref, g_ref, wg_ref, wv_ref, cw_ref, cb_ref, o_ref, hn_even_ref, hn_odd_ref,
                   *, tiles_per_seq):
    i, j = pl.program_id(0), pl.program_id(1)
    tm = hn_even_ref.shape[0] - CONV_HALO

    def gated(hn_ref):
        gate = jnp.dot(hn_ref[...], wg_ref[...], preferred_element_type=F32)
        val = jnp.dot(hn_ref[CONV_HALO:, :], wv_ref[...], preferred_element_type=F32)
        conv = cb_ref[...] + gate[CONV_HALO:] * cw_ref[CONV_WIDTH - 1:CONV_WIDTH, :]
        for back in range(1, CONV_WIDTH):
            tap = CONV_WIDTH - 1 - back
            conv = conv + pltpu.roll(gate, back, axis=0)[CONV_HALO:] * cw_ref[tap:tap + 1, :]
        act = conv / (1.0 + jnp.exp(-conv)) * val
        o_ref[...] = act.astype(o_ref.dtype)

    def row_step(filling_ref, finished_ref):
        tail = finished_ref[tm:tm + CONV_HALO, :]
        filling_ref[0:CONV_HALO, :] = jnp.where(i % tiles_per_seq == 0, jnp.zeros_like(tail), tail)
        _norm_share(x_ref, g_ref, filling_ref, CONV_HALO, j)
        gated(finished_ref)

    @pl.when(i == 0)
    def _():
        hn_even_ref[0:CONV_HALO, :] = jnp.zeros((CONV_HALO, hn_even_ref.shape[1]), hn_even_ref.dtype)
        _norm_share(x_ref, g_ref, hn_even_ref, CONV_HALO, j)

    @pl.when(i % 2 == 1)
    def _():
        row_step(hn_odd_ref, hn_even_ref)

    @pl.when(jnp.logical_and(i % 2 == 0, i > 0))
    def _():
        row_step(hn_even_ref, hn_odd_ref)


def _ffn_up(x, g, w_up, conv_w, conv_b, S, tm, tf):
    M, K = x.shape
    d_ff = w_up.shape[1] // 2
    tiles, steps = M // tm, d_ff // tf

    def col(i, j):
        return jnp.where(i > 0, j, 0)

    return pl.pallas_call(
        functools.partial(_ffn_up_kernel, tiles_per_seq=S // tm),
        grid=(tiles + 1, steps),
        in_specs=[
            _share_spec(tm, K, tiles, steps),
            pl.BlockSpec((1, K), lambda i, j: (0, 0)),
            pl.BlockSpec((K, tf), lambda i, j: (0, col(i, j))),
            pl.BlockSpec((K, tf), lambda i, j: (0, steps + col(i, j))),
            pl.BlockSpec((CONV_WIDTH, tf), lambda i, j: (0, col(i, j))),
            pl.BlockSpec((1, tf), lambda i, j: (0, col(i, j))),
        ],
        out_specs=pl.BlockSpec((tm, tf), lambda i, j: (jnp.maximum(i - 1, 0), col(i, j))),
        out_shape=jax.ShapeDtypeStruct((M, d_ff), BF16),
        scratch_shapes=[pltpu.VMEM((tm + CONV_HALO, K), BF16), pltpu.VMEM((tm + CONV_HALO, K), BF16)],
        compiler_params=_params(("arbitrary", "arbitrary")),
        name="ffn_up",
    )(x, g.reshape(1, K), w_up, w_up, conv_w, conv_b.reshape(1, d_ff))


def _score_scale_on_q(n_cols, q_starts, q_width):
    cs = np.ones((n_cols,), np.float32)
    for start in q_starts:
        cs[start:start + q_width] = HEAD_DIM ** -0.5
    return jnp.asarray(cs)


def kernel(x, ln_mix, ln_ffn, ln_f, w_qkv_sb, w_o_sb, w_qkv_dil, w_o_dil, rel_bias, w_up, conv_w, conv_b, w_down):
    B, S, D = x.shape
    depth = ln_mix.shape[0]
    assert S % PROJ_ROWS == 0 and S % min(SB_ROWS, S) == 0 and S % (DIL_RES * BLOCK) == 0
    h = x.reshape(B * S, D)
    for i in range(depth):
        j = i // N_MIXERS
        if i % N_MIXERS == 0:
            q_scale = _score_scale_on_q(w_qkv_sb.shape[2], [0], SB_HEADS * HEAD_DIM)
            qkv = _norm_matmul_heads(h, ln_mix[i], w_qkv_sb[j].astype(BF16), q_scale, tm=PROJ_ROWS, tn=QKV_COLS)
            o = _stick_breaking(qkv, B, S, tq=min(SB_ROWS, S), heads=SB_HEADS_PER_STEP)
            h = _matmul_res(o, w_o_sb[j].astype(BF16), h, tm=RES_ROWS)
        else:
            perm = _residue_major_permutation()
            q_scale = _score_scale_on_q(w_qkv_dil.shape[2], [g * 3 * DIL_WIDTH for g in range(DIL_GROUPS)], DIL_WIDTH)
            qkv = _norm_matmul_perm(h, ln_mix[i], w_qkv_dil[j].astype(BF16), q_scale, jnp.asarray(perm, BF16),
                                    natural_cols=3 * DIL_WIDTH, tm=PROJ_ROWS, tn=QKV_COLS)
            os, lses = [], []
            for g, (window, dil) in enumerate(DIL_PATTERNS):
                assert window // dil == BLOCK and S % (dil * BLOCK) == 0
                table = rel_bias[:, g * DIL_GROUP_HEADS:(g + 1) * DIL_GROUP_HEADS]
                o, lse = _dilated_group(qkv, table, B, S, g, dil)
                os.append(o)
                lses.append(lse)
            h = _combine_matmul_res(os, lses, jnp.asarray(perm.T, BF16), w_o_dil[j].astype(BF16), h, tm=RES_ROWS)
        act = _ffn_up(h, ln_ffn[i], w_up[i].astype(BF16), conv_w[i], conv_b[i], S, tm=PROJ_ROWS, tf=FFN_COLS)
        h = _matmul_res(act, w_down[i].astype(BF16), h, tm=RES_ROWS, norm_gain=ln_f if i == depth - 1 else None)
    return h.reshape(B, S, D)
```

```python
import functools
import math

import jax
import jax.numpy as jnp
import numpy as np
from jax import lax
from jax.experimental import pallas as pl
from jax.experimental.pallas import tpu as pltpu

BLOCK = 128
SB_HEADS = 16
HEAD_DIM = 128
DIL_PATTERNS = ((128, 1), (512, 4), (2048, 16))
DIL_GROUPS = len(DIL_PATTERNS)
DIL_GROUP_HEADS = 8
DIL_WIDTH = DIL_GROUP_HEADS * HEAD_DIM
REL_BUCKETS = 32
REL_MAX_DISTANCE = 2048
CONV_WIDTH = 3
RMS_EPS = 1e-6
NEG_INF = -1e30
LOG2E = math.log2(math.e)
N_MIXERS = 2

SB_EXIT_LOG = -104.0
SB_STATIC_DEPTH = 3

BF16_ROWS = 16
DIL_RES = 16
DIL_CHUNK = BF16_ROWS
DIL_TILE = DIL_RES * DIL_CHUNK
CONV_HALO = BF16_ROWS
NORM_CHUNK = 64
V7X_VMEM_BYTES = 64 * 1024 * 1024
VMEM_LIMIT = V7X_VMEM_BYTES * 7 // 8

PROJ_ROWS = 1024
QKV_COLS = 1536
FFN_COLS = 1024
RES_ROWS = 512
SB_ROWS = 4096
SB_HEADS_PER_STEP = 2
DIL_BLOCKS_PER_STEP = 16

BF16 = jnp.bfloat16
F32 = jnp.float32


def _params(semantics):
    return pltpu.CompilerParams(dimension_semantics=semantics, vmem_limit_bytes=VMEM_LIMIT)


def _rms_rows(x, g):
    ms = jnp.mean(x * x, axis=-1, keepdims=True)
    return x * lax.rsqrt(ms + RMS_EPS) * g


def _norm_share(x_ref, g_ref, dst_ref, dst_row0, step):
    rows = x_ref.shape[0]
    shares = (dst_ref.shape[0] - dst_row0) // rows
    base = dst_row0 + jnp.minimum(step, shares - 1) * rows
    g = g_ref[...]
    for c in range(rows // NORM_CHUNK):
        y = _rms_rows(x_ref[c * NORM_CHUNK:(c + 1) * NORM_CHUNK, :], g)
        row = pl.multiple_of(base + c * NORM_CHUNK, BF16_ROWS)
        dst_ref[pl.ds(row, NORM_CHUNK), :] = y.astype(dst_ref.dtype)


def _row_shares(steps):
    return 1 << (steps.bit_length() - 1)


def _share_spec(tm, K, tiles, steps):
    shares = _row_shares(steps)
    return pl.BlockSpec((tm // shares, K),
                        lambda i, j: (jnp.minimum(i, tiles - 1) * shares + jnp.minimum(j, shares - 1), 0))


def _side_cast_specs(weights, tiles, steps):
    n = tiles * steps
    in_specs, out_specs, out_shapes = [], [], []
    for w in weights:
        rows = w.shape[0] // n
        assert w.shape[0] % n == 0 and rows % BF16_ROWS == 0
        index = lambda i, j: (jnp.minimum(i * steps + j, n - 1), 0)
        in_specs.append(pl.BlockSpec((rows, w.shape[1]), index))
        out_specs.append(pl.BlockSpec((rows, w.shape[1]), index))
        out_shapes.append(jax.ShapeDtypeStruct(w.shape, BF16))
    return in_specs, out_specs, out_shapes


def _side_cast(src_refs, dst_refs):
    for src_ref, dst_ref in zip(src_refs, dst_refs):
        dst_ref[...] = src_ref[...].astype(dst_ref.dtype)


def _norm_matmul_kernel(x_ref, g_ref, w_ref, cs_ref, *refs, n_side):
    side_src, (o_ref, *side_dst), (hn_even_ref, hn_odd_ref) = refs[:n_side], refs[n_side:-2], refs[-2:]
    i, j = pl.program_id(0), pl.program_id(1)
    _side_cast(side_src, side_dst)

    def project(hn_ref):
        y = (jnp.dot(hn_ref[...], w_ref[...], preferred_element_type=F32) * cs_ref[...]).astype(o_ref.dtype)
        for c in range(o_ref.shape[0]):
            o_ref[c] = y[:, c * HEAD_DIM:(c + 1) * HEAD_DIM]

    @pl.when(i == 0)
    def _():
        _norm_share(x_ref, g_ref, hn_even_ref, 0, j)

    @pl.when(i % 2 == 1)
    def _():
        _norm_share(x_ref, g_ref, hn_odd_ref, 0, j)
        project(hn_even_ref)

    @pl.when(jnp.logical_and(i % 2 == 0, i > 0))
    def _():
        _norm_share(x_ref, g_ref, hn_even_ref, 0, j)
        project(hn_odd_ref)


def _norm_matmul_heads(x, g, w, col_scale, side_weights, tm, tn):
    M, K = x.shape
    N = w.shape[1]
    slabs = tn // HEAD_DIM
    tiles, steps = M // tm, N // tn

    def col(i, j):
        return jnp.where(i > 0, j, 0)

    side_in, side_out, side_shapes = _side_cast_specs(side_weights, tiles, steps)
    qkv, *side = pl.pallas_call(
        functools.partial(_norm_matmul_kernel, n_side=len(side_weights)),
        grid=(tiles + 1, steps),
        in_specs=[
            _share_spec(tm, K, tiles, steps),
            pl.BlockSpec((1, K), lambda i, j: (0, 0)),
            pl.BlockSpec((K, tn), lambda i, j: (0, col(i, j))),
            pl.BlockSpec((1, tn), lambda i, j: (0, col(i, j))),
        ] + side_in,
        out_specs=[pl.BlockSpec((slabs, tm, HEAD_DIM), lambda i, j: (col(i, j), jnp.maximum(i - 1, 0), 0))]
        + side_out,
        out_shape=[jax.ShapeDtypeStruct((N // HEAD_DIM, M, HEAD_DIM), BF16)] + side_shapes,
        scratch_shapes=[pltpu.VMEM((tm, K), BF16), pltpu.VMEM((tm, K), BF16)],
        compiler_params=_params(("arbitrary", "arbitrary")),
        name="norm_matmul",
    )(x, g.reshape(1, K), w, col_scale.reshape(1, N), *side_weights)
    return qkv, side


def _matmul_res_kernel(a_ref, w_ref, r_ref, o_ref):
    o_ref[...] = r_ref[...] + jnp.dot(a_ref[...], w_ref[...], preferred_element_type=F32)


def _matmul_res_norm_kernel(a_ref, w_ref, r_ref, g_ref, o_ref):
    y = r_ref[...] + jnp.dot(a_ref[...], w_ref[...], preferred_element_type=F32)
    o_ref[...] = _rms_rows(y, g_ref[...])


def _matmul_res(a, w, res, tm, norm_gain=None):
    M, K = a.shape
    N = w.shape[1]
    in_specs = [
        pl.BlockSpec((tm, K), lambda i: (i, 0)),
        pl.BlockSpec((K, N), lambda i: (0, 0), pipeline_mode=pl.Buffered(1)),
        pl.BlockSpec((tm, N), lambda i: (i, 0)),
    ]
    args = [a, w, res]
    if norm_gain is not None:
        in_specs.append(pl.BlockSpec((1, N), lambda i: (0, 0)))
        args.append(norm_gain.reshape(1, N))
    return pl.pallas_call(
        _matmul_res_kernel if norm_gain is None else _matmul_res_norm_kernel,
        grid=(M // tm,),
        in_specs=in_specs,
        out_specs=pl.BlockSpec((tm, N), lambda i: (i, 0)),
        out_shape=jax.ShapeDtypeStruct((M, N), F32),
        compiler_params=_params(("parallel",)),
        name="matmul_res" if norm_gain is None else "matmul_res_norm",
    )(*args)


def _sb_kernel(q_ref, k_ref, v_ref, o_ref, acc_ref, spent_ref, *, chains, heads):
    row = lax.broadcasted_iota(jnp.int32, (BLOCK, BLOCK), 0)
    col = lax.broadcasted_iota(jnp.int32, (BLOCK, BLOCK), 1)
    causal = col < row
    suffix = (row >= col).astype(BF16)
    base = pl.program_id(2) * chains
    contract_last = (((1,), (1,)), ((), ()))

    def mask_first_block(x):
        head = jnp.where(causal, x[:BLOCK], 0.0)
        return head if x.shape[0] == BLOCK else jnp.concatenate([head, x[BLOCK:]], axis=0)

    def key_block(h, rows, kb, on_diagonal):
        start = pl.multiple_of(kb * BLOCK, BLOCK)
        k = k_ref[h, pl.ds(start, BLOCK), :]
        v = v_ref[h, pl.ds(start, BLOCK), :]
        z = lax.dot_general(q_ref[h, rows, :], k, contract_last, preferred_element_type=F32)
        drop = jnp.maximum(z, 0.0) + jnp.log(1.0 + jnp.exp2(jnp.abs(z) * -LOG2E))
        if on_diagonal:
            drop = mask_first_block(drop)
        upto = jnp.dot(drop.astype(BF16), suffix, preferred_element_type=F32)
        a = jnp.exp(z - (upto + spent_ref[h, rows, :]))
        if on_diagonal:
            a = mask_first_block(a)
        acc_ref[h, rows, :] += jnp.dot(a.astype(BF16), v, preferred_element_type=F32)
        spent_ref[h, rows, :] += upto[:, 0:1]

    def retire_if_no_keys(h, rows, kb):
        spent_ref[h, rows, :] = jnp.where(kb >= 0, spent_ref[h, rows, :], -NEG_INF)

    acc_ref[...] = jnp.zeros_like(acc_ref)
    spent_ref[...] = jnp.zeros_like(spent_ref)
    for first in range(chains - 1, -SB_STATIC_DEPTH, -1):
        lo, hi = max(first, 0), min(first + SB_STATIC_DEPTH - 1, chains - 1)
        rows = slice(lo * BLOCK, (hi + 1) * BLOCK)
        kb = base + first
        for h in range(heads):
            if first >= 0:
                key_block(h, rows, kb, True)
            else:
                retire_if_no_keys(h, rows, kb)
                key_block(h, rows, jnp.maximum(kb, 0), False)

    def cond(carry):
        _, least_spent = carry
        return least_spent < -SB_EXIT_LOG

    def body(carry):
        delta, _ = carry
        for c in range(chains):
            rows = slice(c * BLOCK, (c + 1) * BLOCK)
            kb = base + c - delta
            for h in range(heads):
                retire_if_no_keys(h, rows, kb)
                key_block(h, rows, jnp.maximum(kb, 0), False)
        return delta + 1, jnp.min(spent_ref[...])

    lax.while_loop(cond, body, (jnp.int32(SB_STATIC_DEPTH), jnp.min(spent_ref[...])))
    for h in range(heads):
        o_ref[:, h * HEAD_DIM:(h + 1) * HEAD_DIM] = acc_ref[h].astype(o_ref.dtype)


def _stick_breaking(qkv, B, S, tq, heads):
    H = SB_HEADS
    nq = S // tq
    hb = H // heads
    return pl.pallas_call(
        functools.partial(_sb_kernel, chains=tq // BLOCK, heads=heads),
        grid=(B, hb, nq),
        in_specs=[
            pl.BlockSpec((heads, tq, HEAD_DIM), lambda b, h, i: (h, b * nq + i, 0)),
            pl.BlockSpec((heads, S, HEAD_DIM), lambda b, h, i: (hb + h, b, 0)),
            pl.BlockSpec((heads, S, HEAD_DIM), lambda b, h, i: (2 * hb + h, b, 0)),
        ],
        out_specs=pl.BlockSpec((tq, heads * HEAD_DIM), lambda b, h, i: (b * nq + i, h)),
        out_shape=jax.ShapeDtypeStruct((B * S, H * HEAD_DIM), BF16),
        scratch_shapes=[pltpu.VMEM((heads, tq, HEAD_DIM), F32), pltpu.VMEM((heads, tq, BLOCK), F32)],
        compiler_params=_params(("parallel", "parallel", "arbitrary")),
        name="stick_breaking",
    )(qkv, qkv, qkv)


def _residue_major_permutation():
    i = np.arange(DIL_TILE)
    p = np.zeros((DIL_TILE, DIL_TILE), np.float32)
    p[i, (i % DIL_CHUNK) * DIL_RES + i // DIL_CHUNK] = 1.0
    return p


def _norm_matmul_perm_kernel(x_ref, g_ref, p_ref, w_ref, cs_ref, o_ref, hn_even_ref, hn_odd_ref, hnp_ref,
                             *, natural_tiles):
    i, j = pl.program_id(0), pl.program_id(1)

    def permute(hn_ref):
        for t in range(hn_ref.shape[0] // DIL_TILE):
            rows = slice(t * DIL_TILE, (t + 1) * DIL_TILE)
            hnp_ref[rows, :] = jnp.dot(p_ref[...], hn_ref[rows, :],
                                       preferred_element_type=F32).astype(hnp_ref.dtype)

    def project(lhs_ref):
        y = jnp.dot(lhs_ref[...], w_ref[...], preferred_element_type=F32) * cs_ref[...]
        o_ref[...] = y.astype(o_ref.dtype)

    def row_step(filling_ref, finished_ref):
        @pl.when(j == 0)
        def _():
            permute(finished_ref)

        @pl.when(j < natural_tiles)
        def _():
            _norm_share(x_ref, g_ref, filling_ref, 0, j)
            project(finished_ref)

        @pl.when(j >= natural_tiles)
        def _():
            _norm_share(x_ref, g_ref, filling_ref, 0, j)
            project(hnp_ref)

    @pl.when(i == 0)
    def _():
        _norm_share(x_ref, g_ref, hn_even_ref, 0, j)

    @pl.when(i % 2 == 1)
    def _():
        row_step(hn_odd_ref, hn_even_ref)

    @pl.when(jnp.logical_and(i % 2 == 0, i > 0))
    def _():
        row_step(hn_even_ref, hn_odd_ref)


def _norm_matmul_perm(x, g, w, col_scale, perm, natural_cols, tm, tn):
    M, K = x.shape
    N = w.shape[1]
    tiles, steps = M // tm, N // tn

    def col(i, j):
        return jnp.where(i > 0, j, 0)

    return pl.pallas_call(
        functools.partial(_norm_matmul_perm_kernel, natural_tiles=natural_cols // tn),
        grid=(tiles + 1, steps),
        in_specs=[
            _share_spec(tm, K, tiles, steps),
            pl.BlockSpec((1, K), lambda i, j: (0, 0)),
            pl.BlockSpec((DIL_TILE, DIL_TILE), lambda i, j: (0, 0)),
            pl.BlockSpec((K, tn), lambda i, j: (0, col(i, j))),
            pl.BlockSpec((1, tn), lambda i, j: (0, col(i, j))),
        ],
        out_specs=pl.BlockSpec((tm, tn), lambda i, j: (jnp.maximum(i - 1, 0), col(i, j))),
        out_shape=jax.ShapeDtypeStruct((M, N), BF16),
        scratch_shapes=[pltpu.VMEM((tm, K), BF16), pltpu.VMEM((tm, K), BF16), pltpu.VMEM((tm, K), BF16)],
        compiler_params=_params(("arbitrary", "arbitrary")),
        name="norm_matmul_perm",
    )(x, g.reshape(1, K), perm, w, col_scale.reshape(1, N))


def _rel_bucket_table(dil, order):
    order = np.asarray(order)
    qi = order[:, None]
    kc = np.concatenate([order, BLOCK + order])[None, :]
    rel = BLOCK + qi - kc
    n = jnp.asarray(np.maximum(rel, 0) * dil)
    max_exact = REL_BUCKETS // 2
    nf = jnp.maximum(n, 1).astype(F32)
    large = max_exact + (jnp.log(nf / max_exact) / math.log(REL_MAX_DISTANCE / max_exact)
                         * (REL_BUCKETS - max_exact)).astype(jnp.int32)
    large = jnp.minimum(large, REL_BUCKETS - 1)
    bucket = jnp.where(n < max_exact, n, large).astype(jnp.int32)
    return jnp.where(jnp.asarray((rel >= 0) & (rel <= BLOCK)), bucket, -1)


class _NaturalRows:
    @staticmethod
    def load(ref, j, cols):
        return ref[j * BLOCK:(j + 1) * BLOCK, cols]

    @staticmethod
    def load_prev(ref, cols):
        return ref[:, cols]

    @staticmethod
    def store(ref, j, cols, val):
        ref[j * BLOCK:(j + 1) * BLOCK, cols] = val


class _Residue16Rows:
    per = BLOCK // DIL_CHUNK

    @classmethod
    def load(cls, ref, j, cols):
        return jnp.concatenate([ref[cls.per * j + t, :, cols] for t in range(cls.per)], axis=0)

    @classmethod
    def load_prev(cls, ref, cols):
        return cls.load(ref, 0, cols)

    @classmethod
    def store(cls, ref, j, cols, val):
        for t in range(cls.per):
            ref[cls.per * j + t, :, cols] = val[t * DIL_CHUNK:(t + 1) * DIL_CHUNK]


class _Residue4Rows:
    per = BLOCK // 4 // DIL_CHUNK
    order = [4 * m + a for a in range(4) for m in range(BLOCK // 4)]

    @classmethod
    def load(cls, ref, j, cols):
        return jnp.concatenate([ref[cls.per * j + t, a, :, cols] for a in range(4) for t in range(cls.per)],
                               axis=0)

    @classmethod
    def load_prev(cls, ref, cols):
        return cls.load(ref, 0, cols)

    @classmethod
    def store(cls, ref, j, cols, val):
        for a in range(4):
            for t in range(cls.per):
                piece = (a * cls.per + t) * DIL_CHUNK
                ref[cls.per * j + t, a, :, cols] = val[piece:piece + DIL_CHUNK]


def _dil_kernel(bucket_ref, table_ref, q_ref, kc_ref, kp_ref, vc_ref, vp_ref, o_ref, lse_ref, bias_ref,
                *, sub_blocks, rows):
    first_step = jnp.logical_and(jnp.logical_and(pl.program_id(0) == 0, pl.program_id(1) == 0),
                                 pl.program_id(2) == 0)

    @pl.when(first_step)
    def _():
        bucket = bucket_ref[...]
        for h in range(DIL_GROUP_HEADS):
            b = jnp.full(bucket.shape, NEG_INF, F32)
            for t in range(REL_BUCKETS):
                b = jnp.where(bucket == t, table_ref[t, h], b)
            bias_ref[h] = b

    lane = lax.broadcasted_iota(jnp.int32, (BLOCK, 128), 1)
    keep_keys = jnp.logical_or(lax.broadcasted_iota(jnp.int32, (BLOCK, 2 * BLOCK), 1) >= BLOCK,
                               pl.program_id(2) > 0)
    ones = jnp.ones((2 * BLOCK, HEAD_DIM), BF16)
    contract_last = (((1,), (1,)), ((), ()))
    all_lanes = slice(None)

    def prev_and_cur(cur_ref, prev_ref, j, cols):
        prev = rows.load_prev(prev_ref, cols) if j == 0 else rows.load(cur_ref, j - 1, cols)
        return jnp.concatenate([prev, rows.load(cur_ref, j, cols)], axis=0)

    for j in range(sub_blocks):
        stats = jnp.zeros((BLOCK, 128), F32)
        for h in range(DIL_GROUP_HEADS):
            cols = slice(h * HEAD_DIM, (h + 1) * HEAD_DIM)
            q = rows.load(q_ref, j, cols)
            k = prev_and_cur(kc_ref, kp_ref, j, cols)
            v = prev_and_cur(vc_ref, vp_ref, j, cols)
            s = lax.dot_general(q, k, contract_last, preferred_element_type=F32) + bias_ref[h]
            if j == 0:
                s = jnp.where(keep_keys, s, NEG_INF)
            m = jnp.max(s, axis=-1, keepdims=True)
            p = jnp.exp(s - m).astype(BF16)
            ol = jnp.dot(p, jnp.concatenate([v, ones], axis=1), preferred_element_type=F32)
            o, l = ol[:, :HEAD_DIM], ol[:, HEAD_DIM:]
            rows.store(o_ref, j, cols, (o / l).astype(o_ref.dtype))
            stats = jnp.where(lane == h, m + jnp.log(l), stats)
        rows.store(lse_ref, j, all_lanes, stats)


def _dilated_group(qkv, rel_table, B, S, g, dil):
    W, C = DIL_WIDTH, qkv.shape[1]
    tiles = S // DIL_TILE
    col0 = g * 3
    if dil == 1:
        rows, n_res = _NaturalRows, 1
        blocks = min(DIL_BLOCKS_PER_STEP, S // BLOCK)
        steps = S // (blocks * BLOCK)
        views = [(B * S, C), (B * S, W), (B * S, 128)]
        cur_shape, prev_shape = (blocks * BLOCK,), (BLOCK,)
        cur_idx = lambda b, r, i: (b * steps + i,)
        prev_idx = lambda b, r, i: (b * (S // BLOCK) + jnp.maximum(i * blocks - 1, 0),)
    else:
        rows = _Residue16Rows if dil == DIL_RES else _Residue4Rows
        assert dil in (4, DIL_RES) and DIL_RES == 16
        per = rows.per
        t_step = min(DIL_BLOCKS_PER_STEP * per, tiles)
        blocks, steps = t_step // per, tiles // t_step
        if dil == DIL_RES:
            n_res = DIL_RES
            views = [(B * tiles, DIL_RES, DIL_CHUNK, c) for c in (C, W, 128)]
            cur_shape, prev_shape = (t_step, None, DIL_CHUNK), (per, None, DIL_CHUNK)
            cur_idx = lambda b, r, i: (b * steps + i, r, 0)
            prev_idx = lambda b, r, i: (b * (tiles // per) + jnp.maximum(i * blocks - 1, 0), r, 0)
        else:
            n_res = 4
            views = [(B * tiles, 4, 4, DIL_CHUNK, c) for c in (C, W, 128)]
            cur_shape, prev_shape = (t_step, 4, None, DIL_CHUNK), (per, 4, None, DIL_CHUNK)
            cur_idx = lambda b, r, i: (b * steps + i, 0, r, 0)
            prev_idx = lambda b, r, i: (b * (tiles // per) + jnp.maximum(i * blocks - 1, 0), 0, r, 0)

    def spec(shape, idx, width, col):
        return pl.BlockSpec(shape + (width,), lambda b, r, i: idx(b, r, i) + (col,))

    qkv_v = qkv.reshape(views[0])
    order = getattr(rows, "order", list(range(BLOCK)))
    o, lse = pl.pallas_call(
        functools.partial(_dil_kernel, sub_blocks=blocks, rows=rows),
        grid=(B, n_res, steps),
        in_specs=[
            pl.BlockSpec((BLOCK, 2 * BLOCK), lambda b, r, i: (0, 0)),
            pl.BlockSpec(memory_space=pltpu.SMEM),
            spec(cur_shape, cur_idx, W, col0),
            spec(cur_shape, cur_idx, W, col0 + 1), spec(prev_shape, prev_idx, W, col0 + 1),
            spec(cur_shape, cur_idx, W, col0 + 2), spec(prev_shape, prev_idx, W, col0 + 2),
        ],
        out_specs=[spec(cur_shape, cur_idx, W, 0), spec(cur_shape, cur_idx, 128, 0)],
        out_shape=[jax.ShapeDtypeStruct(views[1], BF16), jax.ShapeDtypeStruct(views[2], F32)],
        scratch_shapes=[pltpu.VMEM((DIL_GROUP_HEADS, BLOCK, 2 * BLOCK), F32)],
        compiler_params=_params(("arbitrary", "arbitrary", "arbitrary")),
        name=f"dilated_group{g}",
    )(_rel_bucket_table(dil, order), rel_table, qkv_v, qkv_v, qkv_v, qkv_v, qkv_v)
    return o.reshape(B * S, W), lse.reshape(B * S, 128)


def _bf16_pieces(x):
    hi = x.astype(BF16)
    rest = x - hi.astype(F32)
    mid = rest.astype(BF16)
    return [hi, mid, (rest - mid.astype(F32)).astype(BF16)]


def _combine_matmul_res_kernel(o0_ref, o1_ref, o2_ref, l0_ref, l1_ref, l2_ref, pt_ref, w_ref, r_ref, out_ref,
                               comb_ref):
    pt = pt_ref[...]
    for t in range(out_ref.shape[0] // DIL_TILE):
        rows = slice(t * DIL_TILE, (t + 1) * DIL_TILE)
        pieces = jnp.concatenate(_bf16_pieces(l1_ref[rows, :]) + _bf16_pieces(l2_ref[rows, :]), axis=1)
        lse = jnp.dot(pt, pieces, preferred_element_type=F32)
        l0 = l0_ref[rows, :]
        l1 = lse[:, 0:128] + lse[:, 128:256] + lse[:, 256:384]
        l2 = lse[:, 384:512] + lse[:, 512:640] + lse[:, 640:768]
        mx = jnp.maximum(jnp.maximum(l0, l1), l2)
        e0, e1, e2 = jnp.exp(l0 - mx), jnp.exp(l1 - mx), jnp.exp(l2 - mx)
        den = e0 + e1 + e2
        w0, w1, w2 = e0 / den, e1 / den, e2 / den
        o1 = jnp.dot(pt, o1_ref[rows, :], preferred_element_type=F32)
        o2 = jnp.dot(pt, o2_ref[rows, :], preferred_element_type=F32)
        for h in range(DIL_GROUP_HEADS):
            cols = slice(h * HEAD_DIM, (h + 1) * HEAD_DIM)
            c = (w0[:, h:h + 1] * o0_ref[rows, cols].astype(F32)
                 + w1[:, h:h + 1] * o1[:, cols] + w2[:, h:h + 1] * o2[:, cols])
            comb_ref[rows, cols] = c.astype(comb_ref.dtype)
    out_ref[...] = r_ref[...] + jnp.dot(comb_ref[...], w_ref[...], preferred_element_type=F32)


def _combine_matmul_res(os, lses, unperm, w, res, tm):
    M, K = os[0].shape
    N = w.shape[1]
    o_spec = pl.BlockSpec((tm, K), lambda i: (i, 0))
    l_spec = pl.BlockSpec((tm, 128), lambda i: (i, 0))
    row_block = pl.BlockSpec((tm, N), lambda i: (i, 0))
    return pl.pallas_call(
        _combine_matmul_res_kernel,
        grid=(M // tm,),
        in_specs=[o_spec, o_spec, o_spec, l_spec, l_spec, l_spec,
                  pl.BlockSpec((DIL_TILE, DIL_TILE), lambda i: (0, 0)),
                  pl.BlockSpec((K, N), lambda i: (0, 0), pipeline_mode=pl.Buffered(1)),
                  row_block],
        out_specs=row_block,
        out_shape=jax.ShapeDtypeStruct((M, N), F32),
        scratch_shapes=[pltpu.VMEM((tm, K), BF16)],
        compiler_params=_params(("parallel",)),
        name="combine_matmul_res",
    )(*os, *lses, unperm, w, res)


def _ffn_up_kernel(x_ref, g_ref, wg_ref, wv_ref, cw_ref, cb_ref, *refs, tiles_per_seq, n_side):
    side_src, (o_ref, *side_dst), (hn_even_ref, hn_odd_ref) = refs[:n_side], refs[n_side:-2], refs[-2:]
    i, j = pl.program_id(0), pl.program_id(1)
    _side_cast(side_src, side_dst)
    tm = hn_even_ref.shape[0] - CONV_HALO

    def gated(hn_ref):
        gate = jnp.dot(hn_ref[...], wg_ref[...], preferred_element_type=F32)
        val = jnp.dot(hn_ref[CONV_HALO:, :], wv_ref[...], preferred_element_type=F32)
        conv = cb_ref[...] + gate[CONV_HALO:] * cw_ref[CONV_WIDTH - 1:CONV_WIDTH, :]
        for back in range(1, CONV_WIDTH):
            tap = CONV_WIDTH - 1 - back
            conv = conv + pltpu.roll(gate, back, axis=0)[CONV_HALO:] * cw_ref[tap:tap + 1, :]
        act = conv / (1.0 + jnp.exp(-conv)) * val
        o_ref[...] = act.astype(o_ref.dtype)

    def row_step(filling_ref, finished_ref):
        tail = finished_ref[tm:tm + CONV_HALO, :]
        filling_ref[0:CONV_HALO, :] = jnp.where(i % tiles_per_seq == 0, jnp.zeros_like(tail), tail)
        _norm_share(x_ref, g_ref, filling_ref, CONV_HALO, j)
        gated(finished_ref)

    @pl.when(i == 0)
    def _():
        hn_even_ref[0:CONV_HALO, :] = jnp.zeros((CONV_HALO, hn_even_ref.shape[1]), hn_even_ref.dtype)
        _norm_share(x_ref, g_ref, hn_even_ref, CONV_HALO, j)

    @pl.when(i % 2 == 1)
    def _():
        row_step(hn_odd_ref, hn_even_ref)

    @pl.when(jnp.logical_and(i % 2 == 0, i > 0))
    def _():
        row_step(hn_even_ref, hn_odd_ref)


def _ffn_up(x, g, w_up, conv_w, conv_b, side_weights, S, tm, tf):
    M, K = x.shape
    d_ff = w_up.shape[1] // 2
    tiles, steps = M // tm, d_ff // tf

    def col(i, j):
        return jnp.where(i > 0, j, 0)

    side_in, side_out, side_shapes = _side_cast_specs(side_weights, tiles, steps)
    act, *side = pl.pallas_call(
        functools.partial(_ffn_up_kernel, tiles_per_seq=S // tm, n_side=len(side_weights)),
        grid=(tiles + 1, steps),
        in_specs=[
            _share_spec(tm, K, tiles, steps),
            pl.BlockSpec((1, K), lambda i, j: (0, 0)),
            pl.BlockSpec((K, tf), lambda i, j: (0, col(i, j))),
            pl.BlockSpec((K, tf), lambda i, j: (0, steps + col(i, j))),
            pl.BlockSpec((CONV_WIDTH, tf), lambda i, j: (0, col(i, j))),
            pl.BlockSpec((1, tf), lambda i, j: (0, col(i, j))),
        ] + side_in,
        out_specs=[pl.BlockSpec((tm, tf), lambda i, j: (jnp.maximum(i - 1, 0), col(i, j)))] + side_out,
        out_shape=[jax.ShapeDtypeStruct((M, d_ff), BF16)] + side_shapes,
        scratch_shapes=[pltpu.VMEM((tm + CONV_HALO, K), BF16), pltpu.VMEM((tm + CONV_HALO, K), BF16)],
        compiler_params=_params(("arbitrary", "arbitrary")),
        name="ffn_up",
    )(x, g.reshape(1, K), w_up, w_up, conv_w, conv_b.reshape(1, d_ff), *side_weights)
    return act, side


def _score_scale_on_q(n_cols, q_starts, q_width):
    cs = np.ones((n_cols,), np.float32)
    for start in q_starts:
        cs[start:start + q_width] = HEAD_DIM ** -0.5
    return jnp.asarray(cs)


def kernel(x, ln_mix, ln_ffn, ln_f, w_qkv_sb, w_o_sb, w_qkv_dil, w_o_dil, rel_bias, w_up, conv_w, conv_b, w_down):
    B, S, D = x.shape
    depth = ln_mix.shape[0]
    assert S % PROJ_ROWS == 0 and S % min(SB_ROWS, S) == 0 and S % (DIL_RES * BLOCK) == 0
    h = x.reshape(B * S, D)
    w_mix, w_up_i = w_qkv_sb[0].astype(BF16), None
    for i in range(depth):
        j = i // N_MIXERS
        if i % N_MIXERS == 0:
            q_scale = _score_scale_on_q(w_qkv_sb.shape[2], [0], SB_HEADS * HEAD_DIM)
            qkv, (w_o, w_up_i) = _norm_matmul_heads(h, ln_mix[i], w_mix, q_scale, [w_o_sb[j], w_up[i]],
                                                    tm=PROJ_ROWS, tn=QKV_COLS)
            o = _stick_breaking(qkv, B, S, tq=min(SB_ROWS, S), heads=SB_HEADS_PER_STEP)
            h = _matmul_res(o, w_o, h, tm=RES_ROWS)
        else:
            perm = _residue_major_permutation()
            q_scale = _score_scale_on_q(w_qkv_dil.shape[2], [g * 3 * DIL_WIDTH for g in range(DIL_GROUPS)], DIL_WIDTH)
            qkv = _norm_matmul_perm(h, ln_mix[i], w_mix, q_scale, jnp.asarray(perm, BF16),
                                    natural_cols=3 * DIL_WIDTH, tm=PROJ_ROWS, tn=QKV_COLS)
            os, lses = [], []
            for g, (window, dil) in enumerate(DIL_PATTERNS):
                assert window // dil == BLOCK and S % (dil * BLOCK) == 0
                table = rel_bias[:, g * DIL_GROUP_HEADS:(g + 1) * DIL_GROUP_HEADS]
                o, lse = _dilated_group(qkv, table, B, S, g, dil)
                os.append(o)
                lses.append(lse)
            h = _combine_matmul_res(os, lses, jnp.asarray(perm.T, BF16), w_o_dil[j].astype(BF16), h, tm=RES_ROWS)
        later = [w_down[i]]
        if i + 1 < depth:
            dilated_next = (i + 1) % N_MIXERS == 1
            later.append((w_qkv_dil if dilated_next else w_qkv_sb)[(i + 1) // N_MIXERS])
            if dilated_next:
                later.append(w_up[i + 1])
        act, (w_down_i, *cast_ahead) = _ffn_up(h, ln_ffn[i], w_up_i, conv_w[i], conv_b[i], later, S,
                                               tm=PROJ_ROWS, tf=FFN_COLS)
        if cast_ahead:
            w_mix, w_up_i = cast_ahead[0], (cast_ahead[1] if len(cast_ahead) > 1 else None)
        h = _matmul_res(act, w_down_i, h, tm=RES_ROWS, norm_gain=ln_f if i == depth - 1 else None)
    return h.reshape(B, S, D)
```

```python
import functools
import math

import jax
import jax.numpy as jnp
import numpy as np
from jax import lax
from jax.experimental import pallas as pl
from jax.experimental.pallas import tpu as pltpu

BLOCK = 128
SB_HEADS = 16
HEAD_DIM = 128
DIL_PATTERNS = ((128, 1), (512, 4), (2048, 16))
DIL_GROUPS = len(DIL_PATTERNS)
DIL_GROUP_HEADS = 8
DIL_WIDTH = DIL_GROUP_HEADS * HEAD_DIM
REL_BUCKETS = 32
REL_MAX_DISTANCE = 2048
CONV_WIDTH = 3
RMS_EPS = 1e-6
NEG_INF = -1e30
LOG2E = math.log2(math.e)
N_MIXERS = 2

SB_EXIT_LOG = -104.0
SB_STATIC_DEPTH = 3

BF16_ROWS = 16
DIL_RES = 16
DIL_CHUNK = BF16_ROWS
DIL_TILE = DIL_RES * DIL_CHUNK
CONV_HALO = BF16_ROWS
NORM_CHUNK = 64
V7X_VMEM_BYTES = 64 * 1024 * 1024
VMEM_LIMIT = V7X_VMEM_BYTES * 7 // 8

PROJ_ROWS = 1024
QKV_COLS = 1536
FFN_COLS = 1024
RES_ROWS = 512
SB_ROWS = 4096
SB_HEADS_PER_STEP = 2
DIL_BLOCKS_PER_STEP = 16

BF16 = jnp.bfloat16
F32 = jnp.float32


def _params(semantics):
    return pltpu.CompilerParams(dimension_semantics=semantics, vmem_limit_bytes=VMEM_LIMIT)


def _rms_rows(x, g):
    ms = jnp.mean(x * x, axis=-1, keepdims=True)
    return x * lax.rsqrt(ms + RMS_EPS) * g


def _norm_share(x_ref, g_ref, dst_ref, dst_row0, step):
    rows = x_ref.shape[0]
    shares = (dst_ref.shape[0] - dst_row0) // rows
    base = dst_row0 + jnp.minimum(step, shares - 1) * rows
    g = g_ref[...]
    for c in range(rows // NORM_CHUNK):
        y = _rms_rows(x_ref[c * NORM_CHUNK:(c + 1) * NORM_CHUNK, :], g)
        row = pl.multiple_of(base + c * NORM_CHUNK, BF16_ROWS)
        dst_ref[pl.ds(row, NORM_CHUNK), :] = y.astype(dst_ref.dtype)


def _row_shares(steps):
    return 1 << (steps.bit_length() - 1)


def _share_spec(tm, K, tiles, steps):
    shares = _row_shares(steps)
    return pl.BlockSpec((tm // shares, K),
                        lambda i, j: (jnp.minimum(i, tiles - 1) * shares + jnp.minimum(j, shares - 1), 0))


def _side_cast_specs(weights, tiles, steps):
    n = tiles * steps
    in_specs, out_specs, out_shapes = [], [], []
    for stacked, layer in weights:
        _, R, C = stacked.shape
        rows = R // n
        assert R % n == 0 and rows % BF16_ROWS == 0

        def slice_of_step(i, j):
            return jnp.minimum(i * steps + j, n - 1)

        in_specs.append(pl.BlockSpec((None, rows, C), lambda i, j, layer=layer: (layer, slice_of_step(i, j), 0)))
        out_specs.append(pl.BlockSpec((rows, C), lambda i, j: (slice_of_step(i, j), 0)))
        out_shapes.append(jax.ShapeDtypeStruct((R, C), BF16))
    return in_specs, out_specs, out_shapes, [stacked for stacked, _ in weights]


def _side_cast(src_refs, dst_refs):
    for src_ref, dst_ref in zip(src_refs, dst_refs):
        dst_ref[...] = src_ref[...].astype(dst_ref.dtype)


def _norm_matmul_kernel(x_ref, g_ref, w_ref, cs_ref, *refs, n_side):
    side_src, (o_ref, *side_dst), (hn_even_ref, hn_odd_ref) = refs[:n_side], refs[n_side:-2], refs[-2:]
    i, j = pl.program_id(0), pl.program_id(1)
    _side_cast(side_src, side_dst)

    def project(hn_ref):
        y = (jnp.dot(hn_ref[...], w_ref[...], preferred_element_type=F32) * cs_ref[...]).astype(o_ref.dtype)
        for c in range(o_ref.shape[0]):
            o_ref[c] = y[:, c * HEAD_DIM:(c + 1) * HEAD_DIM]

    @pl.when(i == 0)
    def _():
        _norm_share(x_ref, g_ref, hn_even_ref, 0, j)

    @pl.when(i % 2 == 1)
    def _():
        _norm_share(x_ref, g_ref, hn_odd_ref, 0, j)
        project(hn_even_ref)

    @pl.when(jnp.logical_and(i % 2 == 0, i > 0))
    def _():
        _norm_share(x_ref, g_ref, hn_even_ref, 0, j)
        project(hn_odd_ref)


def _norm_matmul_heads(x, g, w, col_scale, side_weights, tm, tn):
    M, K = x.shape
    N = w.shape[1]
    slabs = tn // HEAD_DIM
    tiles, steps = M // tm, N // tn

    def col(i, j):
        return jnp.where(i > 0, j, 0)

    side_in, side_out, side_shapes, side_args = _side_cast_specs(side_weights, tiles, steps)
    qkv, *side = pl.pallas_call(
        functools.partial(_norm_matmul_kernel, n_side=len(side_weights)),
        grid=(tiles + 1, steps),
        in_specs=[
            _share_spec(tm, K, tiles, steps),
            pl.BlockSpec((1, K), lambda i, j: (0, 0)),
            pl.BlockSpec((K, tn), lambda i, j: (0, col(i, j))),
            pl.BlockSpec((1, tn), lambda i, j: (0, col(i, j))),
        ] + side_in,
        out_specs=[pl.BlockSpec((slabs, tm, HEAD_DIM), lambda i, j: (col(i, j), jnp.maximum(i - 1, 0), 0))]
        + side_out,
        out_shape=[jax.ShapeDtypeStruct((N // HEAD_DIM, M, HEAD_DIM), BF16)] + side_shapes,
        scratch_shapes=[pltpu.VMEM((tm, K), BF16), pltpu.VMEM((tm, K), BF16)],
        compiler_params=_params(("arbitrary", "arbitrary")),
        name="norm_matmul",
    )(x, g.reshape(1, K), w, col_scale.reshape(1, N), *side_args)
    return qkv, side


def _matmul_res_kernel(a_ref, w_ref, r_ref, o_ref):
    o_ref[...] = r_ref[...] + jnp.dot(a_ref[...], w_ref[...], preferred_element_type=F32)


def _matmul_res_norm_kernel(a_ref, w_ref, r_ref, g_ref, o_ref):
    y = r_ref[...] + jnp.dot(a_ref[...], w_ref[...], preferred_element_type=F32)
    o_ref[...] = _rms_rows(y, g_ref[...])


def _matmul_res(a, w, res, tm, norm_gain=None):
    M, K = a.shape
    N = w.shape[1]
    in_specs = [
        pl.BlockSpec((tm, K), lambda i: (i, 0)),
        pl.BlockSpec((K, N), lambda i: (0, 0), pipeline_mode=pl.Buffered(1)),
        pl.BlockSpec((tm, N), lambda i: (i, 0)),
    ]
    args = [a, w, res]
    if norm_gain is not None:
        in_specs.append(pl.BlockSpec((1, N), lambda i: (0, 0)))
        args.append(norm_gain.reshape(1, N))
    return pl.pallas_call(
        _matmul_res_kernel if norm_gain is None else _matmul_res_norm_kernel,
        grid=(M // tm,),
        in_specs=in_specs,
        out_specs=pl.BlockSpec((tm, N), lambda i: (i, 0)),
        out_shape=jax.ShapeDtypeStruct((M, N), F32),
        compiler_params=_params(("parallel",)),
        name="matmul_res" if norm_gain is None else "matmul_res_norm",
    )(*args)


def _sb_kernel(q_ref, k_ref, v_ref, o_ref, acc_ref, spent_ref, *, chains, heads):
    row = lax.broadcasted_iota(jnp.int32, (BLOCK, BLOCK), 0)
    col = lax.broadcasted_iota(jnp.int32, (BLOCK, BLOCK), 1)
    causal = col < row
    suffix = (row >= col).astype(BF16)
    base = pl.program_id(2) * chains
    contract_last = (((1,), (1,)), ((), ()))

    def mask_first_block(x):
        head = jnp.where(causal, x[:BLOCK], 0.0)
        return head if x.shape[0] == BLOCK else jnp.concatenate([head, x[BLOCK:]], axis=0)

    def key_block(h, rows, kb, on_diagonal):
        start = pl.multiple_of(kb * BLOCK, BLOCK)
        k = k_ref[h, pl.ds(start, BLOCK), :]
        v = v_ref[h, pl.ds(start, BLOCK), :]
        z = lax.dot_general(q_ref[h, rows, :], k, contract_last, preferred_element_type=F32)
        drop = jnp.maximum(z, 0.0) + jnp.log(1.0 + jnp.exp2(jnp.abs(z) * -LOG2E))
        if on_diagonal:
            drop = mask_first_block(drop)
        upto = jnp.dot(drop.astype(BF16), suffix, preferred_element_type=F32)
        a = jnp.exp(z - (upto + spent_ref[h, rows, :]))
        if on_diagonal:
            a = mask_first_block(a)
        acc_ref[h, rows, :] += jnp.dot(a.astype(BF16), v, preferred_element_type=F32)
        spent_ref[h, rows, :] += upto[:, 0:1]

    def retire_if_no_keys(h, rows, kb):
        spent_ref[h, rows, :] = jnp.where(kb >= 0, spent_ref[h, rows, :], -NEG_INF)

    acc_ref[...] = jnp.zeros_like(acc_ref)
    spent_ref[...] = jnp.zeros_like(spent_ref)
    for first in range(chains - 1, -SB_STATIC_DEPTH, -1):
        lo, hi = max(first, 0), min(first + SB_STATIC_DEPTH - 1, chains - 1)
        rows = slice(lo * BLOCK, (hi + 1) * BLOCK)
        kb = base + first
        for h in range(heads):
            if first >= 0:
                key_block(h, rows, kb, True)
            else:
                retire_if_no_keys(h, rows, kb)
                key_block(h, rows, jnp.maximum(kb, 0), False)

    def cond(carry):
        _, least_spent = carry
        return least_spent < -SB_EXIT_LOG

    def body(carry):
        delta, _ = carry
        for c in range(chains):
            rows = slice(c * BLOCK, (c + 1) * BLOCK)
            kb = base + c - delta
            for h in range(heads):
                retire_if_no_keys(h, rows, kb)
                key_block(h, rows, jnp.maximum(kb, 0), False)
        return delta + 1, jnp.min(spent_ref[...])

    lax.while_loop(cond, body, (jnp.int32(SB_STATIC_DEPTH), jnp.min(spent_ref[...])))
    for h in range(heads):
        o_ref[:, h * HEAD_DIM:(h + 1) * HEAD_DIM] = acc_ref[h].astype(o_ref.dtype)


def _stick_breaking(qkv, B, S, tq, heads):
    H = SB_HEADS
    nq = S // tq
    hb = H // heads
    return pl.pallas_call(
        functools.partial(_sb_kernel, chains=tq // BLOCK, heads=heads),
        grid=(B, hb, nq),
        in_specs=[
            pl.BlockSpec((heads, tq, HEAD_DIM), lambda b, h, i: (h, b * nq + i, 0)),
            pl.BlockSpec((heads, S, HEAD_DIM), lambda b, h, i: (hb + h, b, 0)),
            pl.BlockSpec((heads, S, HEAD_DIM), lambda b, h, i: (2 * hb + h, b, 0)),
        ],
        out_specs=pl.BlockSpec((tq, heads * HEAD_DIM), lambda b, h, i: (b * nq + i, h)),
        out_shape=jax.ShapeDtypeStruct((B * S, H * HEAD_DIM), BF16),
        scratch_shapes=[pltpu.VMEM((heads, tq, HEAD_DIM), F32), pltpu.VMEM((heads, tq, BLOCK), F32)],
        compiler_params=_params(("parallel", "parallel", "arbitrary")),
        name="stick_breaking",
    )(qkv, qkv, qkv)


def _residue_major_permutation():
    i = np.arange(DIL_TILE)
    p = np.zeros((DIL_TILE, DIL_TILE), np.float32)
    p[i, (i % DIL_CHUNK) * DIL_RES + i // DIL_CHUNK] = 1.0
    return p


def _norm_matmul_perm_kernel(x_ref, g_ref, p_ref, w_ref, cs_ref, o_ref, hn_even_ref, hn_odd_ref, hnp_ref,
                             *, natural_tiles):
    i, j = pl.program_id(0), pl.program_id(1)

    def permute(hn_ref):
        for t in range(hn_ref.shape[0] // DIL_TILE):
            rows = slice(t * DIL_TILE, (t + 1) * DIL_TILE)
            hnp_ref[rows, :] = jnp.dot(p_ref[...], hn_ref[rows, :],
                                       preferred_element_type=F32).astype(hnp_ref.dtype)

    def project(lhs_ref):
        y = jnp.dot(lhs_ref[...], w_ref[...], preferred_element_type=F32) * cs_ref[...]
        o_ref[...] = y.astype(o_ref.dtype)

    def row_step(filling_ref, finished_ref):
        @pl.when(j == 0)
        def _():
            permute(finished_ref)

        @pl.when(j < natural_tiles)
        def _():
            _norm_share(x_ref, g_ref, filling_ref, 0, j)
            project(finished_ref)

        @pl.when(j >= natural_tiles)
        def _():
            _norm_share(x_ref, g_ref, filling_ref, 0, j)
            project(hnp_ref)

    @pl.when(i == 0)
    def _():
        _norm_share(x_ref, g_ref, hn_even_ref, 0, j)

    @pl.when(i % 2 == 1)
    def _():
        row_step(hn_odd_ref, hn_even_ref)

    @pl.when(jnp.logical_and(i % 2 == 0, i > 0))
    def _():
        row_step(hn_even_ref, hn_odd_ref)


def _norm_matmul_perm(x, g, w, col_scale, perm, natural_cols, tm, tn):
    M, K = x.shape
    N = w.shape[1]
    tiles, steps = M // tm, N // tn

    def col(i, j):
        return jnp.where(i > 0, j, 0)

    return pl.pallas_call(
        functools.partial(_norm_matmul_perm_kernel, natural_tiles=natural_cols // tn),
        grid=(tiles + 1, steps),
        in_specs=[
            _share_spec(tm, K, tiles, steps),
            pl.BlockSpec((1, K), lambda i, j: (0, 0)),
            pl.BlockSpec((DIL_TILE, DIL_TILE), lambda i, j: (0, 0)),
            pl.BlockSpec((K, tn), lambda i, j: (0, col(i, j))),
            pl.BlockSpec((1, tn), lambda i, j: (0, col(i, j))),
        ],
        out_specs=pl.BlockSpec((tm, tn), lambda i, j: (jnp.maximum(i - 1, 0), col(i, j))),
        out_shape=jax.ShapeDtypeStruct((M, N), BF16),
        scratch_shapes=[pltpu.VMEM((tm, K), BF16), pltpu.VMEM((tm, K), BF16), pltpu.VMEM((tm, K), BF16)],
        compiler_params=_params(("arbitrary", "arbitrary")),
        name="norm_matmul_perm",
    )(x, g.reshape(1, K), perm, w, col_scale.reshape(1, N))


def _rel_bucket_table(dil, order):
    order = np.asarray(order)
    qi = order[:, None]
    kc = np.concatenate([order, BLOCK + order])[None, :]
    rel = BLOCK + qi - kc
    n = jnp.asarray(np.maximum(rel, 0) * dil)
    max_exact = REL_BUCKETS // 2
    nf = jnp.maximum(n, 1).astype(F32)
    large = max_exact + (jnp.log(nf / max_exact) / math.log(REL_MAX_DISTANCE / max_exact)
                         * (REL_BUCKETS - max_exact)).astype(jnp.int32)
    large = jnp.minimum(large, REL_BUCKETS - 1)
    bucket = jnp.where(n < max_exact, n, large).astype(jnp.int32)
    return jnp.where(jnp.asarray((rel >= 0) & (rel <= BLOCK)), bucket, -1)


class _NaturalRows:
    @staticmethod
    def load(ref, j, cols):
        return ref[j * BLOCK:(j + 1) * BLOCK, cols]

    @staticmethod
    def load_prev(ref, cols):
        return ref[:, cols]

    @staticmethod
    def store(ref, j, cols, val):
        ref[j * BLOCK:(j + 1) * BLOCK, cols] = val


class _Residue16Rows:
    per = BLOCK // DIL_CHUNK

    @classmethod
    def load(cls, ref, j, cols):
        return jnp.concatenate([ref[cls.per * j + t, :, cols] for t in range(cls.per)], axis=0)

    @classmethod
    def load_prev(cls, ref, cols):
        return cls.load(ref, 0, cols)

    @classmethod
    def store(cls, ref, j, cols, val):
        for t in range(cls.per):
            ref[cls.per * j + t, :, cols] = val[t * DIL_CHUNK:(t + 1) * DIL_CHUNK]


class _Residue4Rows:
    per = BLOCK // 4 // DIL_CHUNK
    order = [4 * m + a for a in range(4) for m in range(BLOCK // 4)]

    @classmethod
    def load(cls, ref, j, cols):
        return jnp.concatenate([ref[cls.per * j + t, a, :, cols] for a in range(4) for t in range(cls.per)],
                               axis=0)

    @classmethod
    def load_prev(cls, ref, cols):
        return cls.load(ref, 0, cols)

    @classmethod
    def store(cls, ref, j, cols, val):
        for a in range(4):
            for t in range(cls.per):
                piece = (a * cls.per + t) * DIL_CHUNK
                ref[cls.per * j + t, a, :, cols] = val[piece:piece + DIL_CHUNK]


def _dil_kernel(bucket_ref, table_ref, q_ref, kc_ref, kp_ref, vc_ref, vp_ref, o_ref, lse_ref, bias_ref,
                *, sub_blocks, rows):
    first_step = jnp.logical_and(jnp.logical_and(pl.program_id(0) == 0, pl.program_id(1) == 0),
                                 pl.program_id(2) == 0)

    @pl.when(first_step)
    def _():
        bucket = bucket_ref[...]
        for h in range(DIL_GROUP_HEADS):
            b = jnp.full(bucket.shape, NEG_INF, F32)
            for t in range(REL_BUCKETS):
                b = jnp.where(bucket == t, table_ref[t, h], b)
            bias_ref[h] = b

    lane = lax.broadcasted_iota(jnp.int32, (BLOCK, 128), 1)
    keep_keys = jnp.logical_or(lax.broadcasted_iota(jnp.int32, (BLOCK, 2 * BLOCK), 1) >= BLOCK,
                               pl.program_id(2) > 0)
    ones = jnp.ones((2 * BLOCK, HEAD_DIM), BF16)
    contract_last = (((1,), (1,)), ((), ()))
    all_lanes = slice(None)

    def prev_and_cur(cur_ref, prev_ref, j, cols):
        prev = rows.load_prev(prev_ref, cols) if j == 0 else rows.load(cur_ref, j - 1, cols)
        return jnp.concatenate([prev, rows.load(cur_ref, j, cols)], axis=0)

    for j in range(sub_blocks):
        stats = jnp.zeros((BLOCK, 128), F32)
        for h in range(DIL_GROUP_HEADS):
            cols = slice(h * HEAD_DIM, (h + 1) * HEAD_DIM)
            q = rows.load(q_ref, j, cols)
            k = prev_and_cur(kc_ref, kp_ref, j, cols)
            v = prev_and_cur(vc_ref, vp_ref, j, cols)
            s = lax.dot_general(q, k, contract_last, preferred_element_type=F32) + bias_ref[h]
            if j == 0:
                s = jnp.where(keep_keys, s, NEG_INF)
            m = jnp.max(s, axis=-1, keepdims=True)
            p = jnp.exp(s - m).astype(BF16)
            ol = jnp.dot(p, jnp.concatenate([v, ones], axis=1), preferred_element_type=F32)
            o, l = ol[:, :HEAD_DIM], ol[:, HEAD_DIM:]
            rows.store(o_ref, j, cols, (o / l).astype(o_ref.dtype))
            stats = jnp.where(lane == h, m + jnp.log(l), stats)
        rows.store(lse_ref, j, all_lanes, stats)


def _dilated_group(qkv, rel_table, B, S, g, dil):
    W, C = DIL_WIDTH, qkv.shape[1]
    tiles = S // DIL_TILE
    col0 = g * 3
    if dil == 1:
        rows, n_res = _NaturalRows, 1
        blocks = min(DIL_BLOCKS_PER_STEP, S // BLOCK)
        steps = S // (blocks * BLOCK)
        views = [(B * S, C), (B * S, W), (B * S, 128)]
        cur_shape, prev_shape = (blocks * BLOCK,), (BLOCK,)
        cur_idx = lambda b, r, i: (b * steps + i,)
        prev_idx = lambda b, r, i: (b * (S // BLOCK) + jnp.maximum(i * blocks - 1, 0),)
    else:
        rows = _Residue16Rows if dil == DIL_RES else _Residue4Rows
        assert dil in (4, DIL_RES) and DIL_RES == 16
        per = rows.per
        t_step = min(DIL_BLOCKS_PER_STEP * per, tiles)
        blocks, steps = t_step // per, tiles // t_step
        if dil == DIL_RES:
            n_res = DIL_RES
            views = [(B * tiles, DIL_RES, DIL_CHUNK, c) for c in (C, W, 128)]
            cur_shape, prev_shape = (t_step, None, DIL_CHUNK), (per, None, DIL_CHUNK)
            cur_idx = lambda b, r, i: (b * steps + i, r, 0)
            prev_idx = lambda b, r, i: (b * (tiles // per) + jnp.maximum(i * blocks - 1, 0), r, 0)
        else:
            n_res = 4
            views = [(B * tiles, 4, 4, DIL_CHUNK, c) for c in (C, W, 128)]
            cur_shape, prev_shape = (t_step, 4, None, DIL_CHUNK), (per, 4, None, DIL_CHUNK)
            cur_idx = lambda b, r, i: (b * steps + i, 0, r, 0)
            prev_idx = lambda b, r, i: (b * (tiles // per) + jnp.maximum(i * blocks - 1, 0), 0, r, 0)

    def spec(shape, idx, width, col):
        return pl.BlockSpec(shape + (width,), lambda b, r, i: idx(b, r, i) + (col,))

    qkv_v = qkv.reshape(views[0])
    order = getattr(rows, "order", list(range(BLOCK)))
    o, lse = pl.pallas_call(
        functools.partial(_dil_kernel, sub_blocks=blocks, rows=rows),
        grid=(B, n_res, steps),
        in_specs=[
            pl.BlockSpec((BLOCK, 2 * BLOCK), lambda b, r, i: (0, 0)),
            pl.BlockSpec(memory_space=pltpu.SMEM),
            spec(cur_shape, cur_idx, W, col0),
            spec(cur_shape, cur_idx, W, col0 + 1), spec(prev_shape, prev_idx, W, col0 + 1),
            spec(cur_shape, cur_idx, W, col0 + 2), spec(prev_shape, prev_idx, W, col0 + 2),
        ],
        out_specs=[spec(cur_shape, cur_idx, W, 0), spec(cur_shape, cur_idx, 128, 0)],
        out_shape=[jax.ShapeDtypeStruct(views[1], BF16), jax.ShapeDtypeStruct(views[2], F32)],
        scratch_shapes=[pltpu.VMEM((DIL_GROUP_HEADS, BLOCK, 2 * BLOCK), F32)],
        compiler_params=_params(("arbitrary", "arbitrary", "arbitrary")),
        name=f"dilated_group{g}",
    )(_rel_bucket_table(dil, order), rel_table, qkv_v, qkv_v, qkv_v, qkv_v, qkv_v)
    return o.reshape(B * S, W), lse.reshape(B * S, 128)


def _bf16_pieces(x):
    hi = x.astype(BF16)
    rest = x - hi.astype(F32)
    mid = rest.astype(BF16)
    return [hi, mid, (rest - mid.astype(F32)).astype(BF16)]


def _combine_matmul_res_kernel(o0_ref, o1_ref, o2_ref, l0_ref, l1_ref, l2_ref, pt_ref, w_ref, r_ref, out_ref,
                               comb_ref):
    pt = pt_ref[...]
    for t in range(out_ref.shape[0] // DIL_TILE):
        rows = slice(t * DIL_TILE, (t + 1) * DIL_TILE)
        pieces = jnp.concatenate(_bf16_pieces(l1_ref[rows, :]) + _bf16_pieces(l2_ref[rows, :]), axis=1)
        lse = jnp.dot(pt, pieces, preferred_element_type=F32)
        l0 = l0_ref[rows, :]
        l1 = lse[:, 0:128] + lse[:, 128:256] + lse[:, 256:384]
        l2 = lse[:, 384:512] + lse[:, 512:640] + lse[:, 640:768]
        mx = jnp.maximum(jnp.maximum(l0, l1), l2)
        e0, e1, e2 = jnp.exp(l0 - mx), jnp.exp(l1 - mx), jnp.exp(l2 - mx)
        den = e0 + e1 + e2
        w0, w1, w2 = e0 / den, e1 / den, e2 / den
        o1 = jnp.dot(pt, o1_ref[rows, :], preferred_element_type=F32)
        o2 = jnp.dot(pt, o2_ref[rows, :], preferred_element_type=F32)
        for h in range(DIL_GROUP_HEADS):
            cols = slice(h * HEAD_DIM, (h + 1) * HEAD_DIM)
            c = (w0[:, h:h + 1] * o0_ref[rows, cols].astype(F32)
                 + w1[:, h:h + 1] * o1[:, cols] + w2[:, h:h + 1] * o2[:, cols])
            comb_ref[rows, cols] = c.astype(comb_ref.dtype)
    out_ref[...] = r_ref[...] + jnp.dot(comb_ref[...], w_ref[...], preferred_element_type=F32)


def _combine_matmul_res(os, lses, unperm, w, res, tm):
    M, K = os[0].shape
    N = w.shape[1]
    o_spec = pl.BlockSpec((tm, K), lambda i: (i, 0))
    l_spec = pl.BlockSpec((tm, 128), lambda i: (i, 0))
    row_block = pl.BlockSpec((tm, N), lambda i: (i, 0))
    return pl.pallas_call(
        _combine_matmul_res_kernel,
        grid=(M // tm,),
        in_specs=[o_spec, o_spec, o_spec, l_spec, l_spec, l_spec,
                  pl.BlockSpec((DIL_TILE, DIL_TILE), lambda i: (0, 0)),
                  pl.BlockSpec((K, N), lambda i: (0, 0), pipeline_mode=pl.Buffered(1)),
                  row_block],
        out_specs=row_block,
        out_shape=jax.ShapeDtypeStruct((M, N), F32),
        scratch_shapes=[pltpu.VMEM((tm, K), BF16)],
        compiler_params=_params(("parallel",)),
        name="combine_matmul_res",
    )(*os, *lses, unperm, w, res)


def _ffn_up_kernel(x_ref, g_ref, wg_ref, wv_ref, cw_ref, cb_ref, *refs, tiles_per_seq, n_side):
    side_src, (o_ref, *side_dst), (hn_even_ref, hn_odd_ref) = refs[:n_side], refs[n_side:-2], refs[-2:]
    i, j = pl.program_id(0), pl.program_id(1)
    _side_cast(side_src, side_dst)
    tm = hn_even_ref.shape[0] - CONV_HALO

    def gated(hn_ref):
        gate = jnp.dot(hn_ref[...], wg_ref[...], preferred_element_type=F32)
        val = jnp.dot(hn_ref[CONV_HALO:, :], wv_ref[...], preferred_element_type=F32)
        conv = cb_ref[...] + gate[CONV_HALO:] * cw_ref[CONV_WIDTH - 1:CONV_WIDTH, :]
        for back in range(1, CONV_WIDTH):
            tap = CONV_WIDTH - 1 - back
            conv = conv + pltpu.roll(gate, back, axis=0)[CONV_HALO:] * cw_ref[tap:tap + 1, :]
        act = conv / (1.0 + jnp.exp(-conv)) * val
        o_ref[...] = act.astype(o_ref.dtype)

    def row_step(filling_ref, finished_ref):
        tail = finished_ref[tm:tm + CONV_HALO, :]
        filling_ref[0:CONV_HALO, :] = jnp.where(i % tiles_per_seq == 0, jnp.zeros_like(tail), tail)
        _norm_share(x_ref, g_ref, filling_ref, CONV_HALO, j)
        gated(finished_ref)

    @pl.when(i == 0)
    def _():
        hn_even_ref[0:CONV_HALO, :] = jnp.zeros((CONV_HALO, hn_even_ref.shape[1]), hn_even_ref.dtype)
        _norm_share(x_ref, g_ref, hn_even_ref, CONV_HALO, j)

    @pl.when(i % 2 == 1)
    def _():
        row_step(hn_odd_ref, hn_even_ref)

    @pl.when(jnp.logical_and(i % 2 == 0, i > 0))
    def _():
        row_step(hn_even_ref, hn_odd_ref)


def _ffn_up(x, g, w_up, conv_w, conv_b, side_weights, S, tm, tf):
    M, K = x.shape
    d_ff = w_up.shape[1] // 2
    tiles, steps = M // tm, d_ff // tf

    def col(i, j):
        return jnp.where(i > 0, j, 0)

    side_in, side_out, side_shapes, side_args = _side_cast_specs(side_weights, tiles, steps)
    act, *side = pl.pallas_call(
        functools.partial(_ffn_up_kernel, tiles_per_seq=S // tm, n_side=len(side_weights)),
        grid=(tiles + 1, steps),
        in_specs=[
            _share_spec(tm, K, tiles, steps),
            pl.BlockSpec((1, K), lambda i, j: (0, 0)),
            pl.BlockSpec((K, tf), lambda i, j: (0, col(i, j))),
            pl.BlockSpec((K, tf), lambda i, j: (0, steps + col(i, j))),
            pl.BlockSpec((CONV_WIDTH, tf), lambda i, j: (0, col(i, j))),
            pl.BlockSpec((1, tf), lambda i, j: (0, col(i, j))),
        ] + side_in,
        out_specs=[pl.BlockSpec((tm, tf), lambda i, j: (jnp.maximum(i - 1, 0), col(i, j)))] + side_out,
        out_shape=[jax.ShapeDtypeStruct((M, d_ff), BF16)] + side_shapes,
        scratch_shapes=[pltpu.VMEM((tm + CONV_HALO, K), BF16), pltpu.VMEM((tm + CONV_HALO, K), BF16)],
        compiler_params=_params(("arbitrary", "arbitrary")),
        name="ffn_up",
    )(x, g.reshape(1, K), w_up, w_up, conv_w, conv_b.reshape(1, d_ff), *side_args)
    return act, side


def _score_scale_on_q(n_cols, q_starts, q_width):
    cs = np.ones((n_cols,), np.float32)
    for start in q_starts:
        cs[start:start + q_width] = HEAD_DIM ** -0.5
    return jnp.asarray(cs)


def kernel(x, ln_mix, ln_ffn, ln_f, w_qkv_sb, w_o_sb, w_qkv_dil, w_o_dil, rel_bias, w_up, conv_w, conv_b, w_down):
    B, S, D = x.shape
    depth = ln_mix.shape[0]
    assert S % PROJ_ROWS == 0 and S % min(SB_ROWS, S) == 0 and S % (DIL_RES * BLOCK) == 0
    h = x.reshape(B * S, D)
    w_mix, w_up_i = w_qkv_sb[0].astype(BF16), None
    for i in range(depth):
        j = i // N_MIXERS
        if i % N_MIXERS == 0:
            q_scale = _score_scale_on_q(w_qkv_sb.shape[2], [0], SB_HEADS * HEAD_DIM)
            qkv, (w_o, w_up_i) = _norm_matmul_heads(h, ln_mix[i], w_mix, q_scale, [(w_o_sb, j), (w_up, i)],
                                                    tm=PROJ_ROWS, tn=QKV_COLS)
            o = _stick_breaking(qkv, B, S, tq=min(SB_ROWS, S), heads=SB_HEADS_PER_STEP)
            h = _matmul_res(o, w_o, h, tm=RES_ROWS)
        else:
            perm = _residue_major_permutation()
            q_scale = _score_scale_on_q(w_qkv_dil.shape[2], [g * 3 * DIL_WIDTH for g in range(DIL_GROUPS)], DIL_WIDTH)
            qkv = _norm_matmul_perm(h, ln_mix[i], w_mix, q_scale, jnp.asarray(perm, BF16),
                                    natural_cols=3 * DIL_WIDTH, tm=PROJ_ROWS, tn=QKV_COLS)
            os, lses = [], []
            for g, (window, dil) in enumerate(DIL_PATTERNS):
                assert window // dil == BLOCK and S % (dil * BLOCK) == 0
                table = rel_bias[:, g * DIL_GROUP_HEADS:(g + 1) * DIL_GROUP_HEADS]
                o, lse = _dilated_group(qkv, table, B, S, g, dil)
                os.append(o)
                lses.append(lse)
            h = _combine_matmul_res(os, lses, jnp.asarray(perm.T, BF16), w_o_dil[j].astype(BF16), h, tm=RES_ROWS)
        later = [(w_down, i)]
        if i + 1 < depth:
            dilated_next = (i + 1) % N_MIXERS == 1
            later.append((w_qkv_dil if dilated_next else w_qkv_sb, (i + 1) // N_MIXERS))
            if dilated_next:
                later.append((w_up, i + 1))
        act, (w_down_i, *cast_ahead) = _ffn_up(h, ln_ffn[i], w_up_i, conv_w[i], conv_b[i], later, S,
                                               tm=PROJ_ROWS, tf=FFN_COLS)
        if cast_ahead:
            w_mix, w_up_i = cast_ahead[0], (cast_ahead[1] if len(cast_ahead) > 1 else None)
        h = _matmul_res(act, w_down_i, h, tm=RES_ROWS, norm_gain=ln_f if i == depth - 1 else None)
    return h.reshape(B, S, D)
```

```python
import functools
import math

import jax
import jax.numpy as jnp
import numpy as np
from jax import lax
from jax.experimental import pallas as pl
from jax.experimental.pallas import tpu as pltpu

BLOCK = 128
SB_HEADS = 16
HEAD_DIM = 128
DIL_PATTERNS = ((128, 1), (512, 4), (2048, 16))
DIL_GROUPS = len(DIL_PATTERNS)
DIL_GROUP_HEADS = 8
DIL_WIDTH = DIL_GROUP_HEADS * HEAD_DIM
REL_BUCKETS = 32
REL_MAX_DISTANCE = 2048
CONV_WIDTH = 3
RMS_EPS = 1e-6
NEG_INF = -1e30
LOG2E = math.log2(math.e)
N_MIXERS = 2

SB_EXIT_LOG = -104.0
SB_STATIC_DEPTH = 3

BF16_ROWS = 16
DIL_RES = 16
DIL_CHUNK = BF16_ROWS
DIL_TILE = DIL_RES * DIL_CHUNK
CONV_TAIL = 8
NORM_CHUNK = 64
V7X_VMEM_BYTES = 64 * 1024 * 1024
VMEM_LIMIT = V7X_VMEM_BYTES * 7 // 8

PROJ_ROWS = 1024
QKV_COLS = 1536
FFN_COLS = 1024
RES_ROWS = 512
SB_ROWS = 4096
SB_HEADS_PER_STEP = 2
DIL_BLOCKS_PER_STEP = 16

BF16 = jnp.bfloat16
F32 = jnp.float32


def _params(semantics):
    return pltpu.CompilerParams(dimension_semantics=semantics, vmem_limit_bytes=VMEM_LIMIT)


def _rms_rows(x, g):
    ms = jnp.mean(x * x, axis=-1, keepdims=True)
    return x * lax.rsqrt(ms + RMS_EPS) * g


def _norm_share(x_ref, g_ref, dst_ref, dst_row0, step):
    rows = x_ref.shape[0]
    shares = (dst_ref.shape[0] - dst_row0) // rows
    base = dst_row0 + jnp.minimum(step, shares - 1) * rows
    g = g_ref[...]
    for c in range(rows // NORM_CHUNK):
        y = _rms_rows(x_ref[c * NORM_CHUNK:(c + 1) * NORM_CHUNK, :], g)
        row = pl.multiple_of(base + c * NORM_CHUNK, BF16_ROWS)
        dst_ref[pl.ds(row, NORM_CHUNK), :] = y.astype(dst_ref.dtype)


def _row_shares(steps):
    return 1 << (steps.bit_length() - 1)


def _share_spec(tm, K, tiles, steps):
    shares = _row_shares(steps)
    return pl.BlockSpec((tm // shares, K),
                        lambda i, j: (jnp.minimum(i, tiles - 1) * shares + jnp.minimum(j, shares - 1), 0))


def _side_cast_specs(weights, tiles, steps):
    n = tiles * steps
    in_specs, out_specs, out_shapes = [], [], []
    for stacked, layer in weights:
        _, R, C = stacked.shape
        rows = R // n
        assert R % n == 0 and rows % BF16_ROWS == 0

        def slice_of_step(i, j):
            return jnp.minimum(i * steps + j, n - 1)

        in_specs.append(pl.BlockSpec((None, rows, C), lambda i, j, layer=layer: (layer, slice_of_step(i, j), 0)))
        out_specs.append(pl.BlockSpec((rows, C), lambda i, j: (slice_of_step(i, j), 0)))
        out_shapes.append(jax.ShapeDtypeStruct((R, C), BF16))
    return in_specs, out_specs, out_shapes, [stacked for stacked, _ in weights]


def _side_cast(src_refs, dst_refs):
    for src_ref, dst_ref in zip(src_refs, dst_refs):
        dst_ref[...] = src_ref[...].astype(dst_ref.dtype)


def _norm_matmul_kernel(x_ref, g_ref, w_ref, cs_ref, *refs, n_side):
    side_src, (o_ref, *side_dst), (hn_even_ref, hn_odd_ref) = refs[:n_side], refs[n_side:-2], refs[-2:]
    i, j = pl.program_id(0), pl.program_id(1)
    _side_cast(side_src, side_dst)

    def project(hn_ref):
        y = (jnp.dot(hn_ref[...], w_ref[...], preferred_element_type=F32) * cs_ref[...]).astype(o_ref.dtype)
        for c in range(o_ref.shape[0]):
            o_ref[c] = y[:, c * HEAD_DIM:(c + 1) * HEAD_DIM]

    @pl.when(i == 0)
    def _():
        _norm_share(x_ref, g_ref, hn_even_ref, 0, j)

    @pl.when(i % 2 == 1)
    def _():
        _norm_share(x_ref, g_ref, hn_odd_ref, 0, j)
        project(hn_even_ref)

    @pl.when(jnp.logical_and(i % 2 == 0, i > 0))
    def _():
        _norm_share(x_ref, g_ref, hn_even_ref, 0, j)
        project(hn_odd_ref)


def _norm_matmul_heads(x, g, w, col_scale, side_weights, tm, tn):
    M, K = x.shape
    N = w.shape[1]
    slabs = tn // HEAD_DIM
    tiles, steps = M // tm, N // tn

    def col(i, j):
        return jnp.where(i > 0, j, 0)

    side_in, side_out, side_shapes, side_args = _side_cast_specs(side_weights, tiles, steps)
    qkv, *side = pl.pallas_call(
        functools.partial(_norm_matmul_kernel, n_side=len(side_weights)),
        grid=(tiles + 1, steps),
        in_specs=[
            _share_spec(tm, K, tiles, steps),
            pl.BlockSpec((1, K), lambda i, j: (0, 0)),
            pl.BlockSpec((K, tn), lambda i, j: (0, col(i, j))),
            pl.BlockSpec((1, tn), lambda i, j: (0, col(i, j))),
        ] + side_in,
        out_specs=[pl.BlockSpec((slabs, tm, HEAD_DIM), lambda i, j: (col(i, j), jnp.maximum(i - 1, 0), 0))]
        + side_out,
        out_shape=[jax.ShapeDtypeStruct((N // HEAD_DIM, M, HEAD_DIM), BF16)] + side_shapes,
        scratch_shapes=[pltpu.VMEM((tm, K), BF16), pltpu.VMEM((tm, K), BF16)],
        compiler_params=_params(("arbitrary", "arbitrary")),
        name="norm_matmul",
    )(x, g.reshape(1, K), w, col_scale.reshape(1, N), *side_args)
    return qkv, side


def _matmul_res_kernel(a_ref, w_ref, r_ref, o_ref):
    o_ref[...] = r_ref[...] + jnp.dot(a_ref[...], w_ref[...], preferred_element_type=F32)


def _matmul_res_norm_kernel(a_ref, w_ref, r_ref, g_ref, o_ref):
    y = r_ref[...] + jnp.dot(a_ref[...], w_ref[...], preferred_element_type=F32)
    o_ref[...] = _rms_rows(y, g_ref[...])


def _matmul_res(a, w, res, tm, norm_gain=None):
    M, K = a.shape
    N = w.shape[1]
    in_specs = [
        pl.BlockSpec((tm, K), lambda i: (i, 0)),
        pl.BlockSpec((K, N), lambda i: (0, 0), pipeline_mode=pl.Buffered(1)),
        pl.BlockSpec((tm, N), lambda i: (i, 0)),
    ]
    args = [a, w, res]
    if norm_gain is not None:
        in_specs.append(pl.BlockSpec((1, N), lambda i: (0, 0)))
        args.append(norm_gain.reshape(1, N))
    return pl.pallas_call(
        _matmul_res_kernel if norm_gain is None else _matmul_res_norm_kernel,
        grid=(M // tm,),
        in_specs=in_specs,
        out_specs=pl.BlockSpec((tm, N), lambda i: (i, 0)),
        out_shape=jax.ShapeDtypeStruct((M, N), F32),
        compiler_params=_params(("parallel",)),
        name="matmul_res" if norm_gain is None else "matmul_res_norm",
    )(*args)


def _sb_kernel(q_ref, k_ref, v_ref, o_ref, acc_ref, spent_ref, *, chains, heads):
    row = lax.broadcasted_iota(jnp.int32, (BLOCK, BLOCK), 0)
    col = lax.broadcasted_iota(jnp.int32, (BLOCK, BLOCK), 1)
    causal = col < row
    suffix = (row >= col).astype(BF16)
    base = pl.program_id(2) * chains
    contract_last = (((1,), (1,)), ((), ()))

    def mask_first_block(x):
        head = jnp.where(causal, x[:BLOCK], 0.0)
        return head if x.shape[0] == BLOCK else jnp.concatenate([head, x[BLOCK:]], axis=0)

    def key_block(h, rows, kb, on_diagonal):
        start = pl.multiple_of(kb * BLOCK, BLOCK)
        k = k_ref[h, pl.ds(start, BLOCK), :]
        v = v_ref[h, pl.ds(start, BLOCK), :]
        z = lax.dot_general(q_ref[h, rows, :], k, contract_last, preferred_element_type=F32)
        drop = jnp.maximum(z, 0.0) + jnp.log(1.0 + jnp.exp2(jnp.abs(z) * -LOG2E))
        if on_diagonal:
            drop = mask_first_block(drop)
        upto = jnp.dot(drop.astype(BF16), suffix, preferred_element_type=F32)
        a = jnp.exp(z - (upto + spent_ref[h, rows, :]))
        if on_diagonal:
            a = mask_first_block(a)
        acc_ref[h, rows, :] += jnp.dot(a.astype(BF16), v, preferred_element_type=F32)
        spent_ref[h, rows, :] += upto[:, 0:1]

    def retire_if_no_keys(h, rows, kb):
        spent_ref[h, rows, :] = jnp.where(kb >= 0, spent_ref[h, rows, :], -NEG_INF)

    acc_ref[...] = jnp.zeros_like(acc_ref)
    spent_ref[...] = jnp.zeros_like(spent_ref)
    for first in range(chains - 1, -SB_STATIC_DEPTH, -1):
        lo, hi = max(first, 0), min(first + SB_STATIC_DEPTH - 1, chains - 1)
        rows = slice(lo * BLOCK, (hi + 1) * BLOCK)
        kb = base + first
        for h in range(heads):
            if first >= 0:
                key_block(h, rows, kb, True)
            else:
                retire_if_no_keys(h, rows, kb)
                key_block(h, rows, jnp.maximum(kb, 0), False)

    def cond(carry):
        _, least_spent = carry
        return least_spent < -SB_EXIT_LOG

    def body(carry):
        delta, _ = carry
        for c in range(chains):
            rows = slice(c * BLOCK, (c + 1) * BLOCK)
            kb = base + c - delta
            for h in range(heads):
                retire_if_no_keys(h, rows, kb)
                key_block(h, rows, jnp.maximum(kb, 0), False)
        return delta + 1, jnp.min(spent_ref[...])

    lax.while_loop(cond, body, (jnp.int32(SB_STATIC_DEPTH), jnp.min(spent_ref[...])))
    for h in range(heads):
        o_ref[:, h * HEAD_DIM:(h + 1) * HEAD_DIM] = acc_ref[h].astype(o_ref.dtype)


def _stick_breaking(qkv, B, S, tq, heads):
    H = SB_HEADS
    nq = S // tq
    hb = H // heads
    return pl.pallas_call(
        functools.partial(_sb_kernel, chains=tq // BLOCK, heads=heads),
        grid=(B, hb, nq),
        in_specs=[
            pl.BlockSpec((heads, tq, HEAD_DIM), lambda b, h, i: (h, b * nq + i, 0)),
            pl.BlockSpec((heads, S, HEAD_DIM), lambda b, h, i: (hb + h, b, 0)),
            pl.BlockSpec((heads, S, HEAD_DIM), lambda b, h, i: (2 * hb + h, b, 0)),
        ],
        out_specs=pl.BlockSpec((tq, heads * HEAD_DIM), lambda b, h, i: (b * nq + i, h)),
        out_shape=jax.ShapeDtypeStruct((B * S, H * HEAD_DIM), BF16),
        scratch_shapes=[pltpu.VMEM((heads, tq, HEAD_DIM), F32), pltpu.VMEM((heads, tq, BLOCK), F32)],
        compiler_params=_params(("parallel", "parallel", "arbitrary")),
        name="stick_breaking",
    )(qkv, qkv, qkv)


def _residue_major_permutation():
    i = np.arange(DIL_TILE)
    p = np.zeros((DIL_TILE, DIL_TILE), np.float32)
    p[i, (i % DIL_CHUNK) * DIL_RES + i // DIL_CHUNK] = 1.0
    return p


def _norm_matmul_perm_kernel(x_ref, g_ref, p_ref, w_ref, cs_ref, o_ref, hn_even_ref, hn_odd_ref, hnp_ref,
                             *, natural_tiles):
    i, j = pl.program_id(0), pl.program_id(1)

    def permute(hn_ref):
        for t in range(hn_ref.shape[0] // DIL_TILE):
            rows = slice(t * DIL_TILE, (t + 1) * DIL_TILE)
            hnp_ref[rows, :] = jnp.dot(p_ref[...], hn_ref[rows, :],
                                       preferred_element_type=F32).astype(hnp_ref.dtype)

    def project(lhs_ref):
        y = jnp.dot(lhs_ref[...], w_ref[...], preferred_element_type=F32) * cs_ref[...]
        o_ref[...] = y.astype(o_ref.dtype)

    def row_step(filling_ref, finished_ref):
        @pl.when(j == 0)
        def _():
            permute(finished_ref)

        @pl.when(j < natural_tiles)
        def _():
            _norm_share(x_ref, g_ref, filling_ref, 0, j)
            project(finished_ref)

        @pl.when(j >= natural_tiles)
        def _():
            _norm_share(x_ref, g_ref, filling_ref, 0, j)
            project(hnp_ref)

    @pl.when(i == 0)
    def _():
        _norm_share(x_ref, g_ref, hn_even_ref, 0, j)

    @pl.when(i % 2 == 1)
    def _():
        row_step(hn_odd_ref, hn_even_ref)

    @pl.when(jnp.logical_and(i % 2 == 0, i > 0))
    def _():
        row_step(hn_even_ref, hn_odd_ref)


def _norm_matmul_perm(x, g, w, col_scale, perm, natural_cols, tm, tn):
    M, K = x.shape
    N = w.shape[1]
    tiles, steps = M // tm, N // tn

    def col(i, j):
        return jnp.where(i > 0, j, 0)

    return pl.pallas_call(
        functools.partial(_norm_matmul_perm_kernel, natural_tiles=natural_cols // tn),
        grid=(tiles + 1, steps),
        in_specs=[
            _share_spec(tm, K, tiles, steps),
            pl.BlockSpec((1, K), lambda i, j: (0, 0)),
            pl.BlockSpec((DIL_TILE, DIL_TILE), lambda i, j: (0, 0)),
            pl.BlockSpec((K, tn), lambda i, j: (0, col(i, j))),
            pl.BlockSpec((1, tn), lambda i, j: (0, col(i, j))),
        ],
        out_specs=pl.BlockSpec((tm, tn), lambda i, j: (jnp.maximum(i - 1, 0), col(i, j))),
        out_shape=jax.ShapeDtypeStruct((M, N), BF16),
        scratch_shapes=[pltpu.VMEM((tm, K), BF16), pltpu.VMEM((tm, K), BF16), pltpu.VMEM((tm, K), BF16)],
        compiler_params=_params(("arbitrary", "arbitrary")),
        name="norm_matmul_perm",
    )(x, g.reshape(1, K), perm, w, col_scale.reshape(1, N))


def _rel_bucket_table(dil, order):
    order = np.asarray(order)
    qi = order[:, None]
    kc = np.concatenate([order, BLOCK + order])[None, :]
    rel = BLOCK + qi - kc
    n = jnp.asarray(np.maximum(rel, 0) * dil)
    max_exact = REL_BUCKETS // 2
    nf = jnp.maximum(n, 1).astype(F32)
    large = max_exact + (jnp.log(nf / max_exact) / math.log(REL_MAX_DISTANCE / max_exact)
                         * (REL_BUCKETS - max_exact)).astype(jnp.int32)
    large = jnp.minimum(large, REL_BUCKETS - 1)
    bucket = jnp.where(n < max_exact, n, large).astype(jnp.int32)
    return jnp.where(jnp.asarray((rel >= 0) & (rel <= BLOCK)), bucket, -1)


class _NaturalRows:
    @staticmethod
    def load(ref, j, cols):
        return ref[j * BLOCK:(j + 1) * BLOCK, cols]

    @staticmethod
    def load_prev(ref, cols):
        return ref[:, cols]

    @staticmethod
    def store(ref, j, cols, val):
        ref[j * BLOCK:(j + 1) * BLOCK, cols] = val


class _Residue16Rows:
    per = BLOCK // DIL_CHUNK

    @classmethod
    def load(cls, ref, j, cols):
        return jnp.concatenate([ref[cls.per * j + t, :, cols] for t in range(cls.per)], axis=0)

    @classmethod
    def load_prev(cls, ref, cols):
        return cls.load(ref, 0, cols)

    @classmethod
    def store(cls, ref, j, cols, val):
        for t in range(cls.per):
            ref[cls.per * j + t, :, cols] = val[t * DIL_CHUNK:(t + 1) * DIL_CHUNK]


class _Residue4Rows:
    per = BLOCK // 4 // DIL_CHUNK
    order = [4 * m + a for a in range(4) for m in range(BLOCK // 4)]

    @classmethod
    def load(cls, ref, j, cols):
        return jnp.concatenate([ref[cls.per * j + t, a, :, cols] for a in range(4) for t in range(cls.per)],
                               axis=0)

    @classmethod
    def load_prev(cls, ref, cols):
        return cls.load(ref, 0, cols)

    @classmethod
    def store(cls, ref, j, cols, val):
        for a in range(4):
            for t in range(cls.per):
                piece = (a * cls.per + t) * DIL_CHUNK
                ref[cls.per * j + t, a, :, cols] = val[piece:piece + DIL_CHUNK]


def _dil_kernel(bucket_ref, table_ref, q_ref, kc_ref, kp_ref, vc_ref, vp_ref, o_ref, lse_ref, bias_ref,
                *, sub_blocks, rows):
    first_step = jnp.logical_and(jnp.logical_and(pl.program_id(0) == 0, pl.program_id(1) == 0),
                                 pl.program_id(2) == 0)

    @pl.when(first_step)
    def _():
        bucket = bucket_ref[...]
        for h in range(DIL_GROUP_HEADS):
            b = jnp.full(bucket.shape, NEG_INF, F32)
            for t in range(REL_BUCKETS):
                b = jnp.where(bucket == t, table_ref[t, h], b)
            bias_ref[h] = b

    lane = lax.broadcasted_iota(jnp.int32, (BLOCK, 128), 1)
    keep_keys = jnp.logical_or(lax.broadcasted_iota(jnp.int32, (BLOCK, 2 * BLOCK), 1) >= BLOCK,
                               pl.program_id(2) > 0)
    ones = jnp.ones((2 * BLOCK, HEAD_DIM), BF16)
    contract_last = (((1,), (1,)), ((), ()))
    all_lanes = slice(None)

    def prev_and_cur(cur_ref, prev_ref, j, cols):
        prev = rows.load_prev(prev_ref, cols) if j == 0 else rows.load(cur_ref, j - 1, cols)
        return jnp.concatenate([prev, rows.load(cur_ref, j, cols)], axis=0)

    for j in range(sub_blocks):
        stats = jnp.zeros((BLOCK, 128), F32)
        for h in range(DIL_GROUP_HEADS):
            cols = slice(h * HEAD_DIM, (h + 1) * HEAD_DIM)
            q = rows.load(q_ref, j, cols)
            k = prev_and_cur(kc_ref, kp_ref, j, cols)
            v = prev_and_cur(vc_ref, vp_ref, j, cols)
            s = lax.dot_general(q, k, contract_last, preferred_element_type=F32) + bias_ref[h]
            if j == 0:
                s = jnp.where(keep_keys, s, NEG_INF)
            m = jnp.max(s, axis=-1, keepdims=True)
            p = jnp.exp(s - m).astype(BF16)
            ol = jnp.dot(p, jnp.concatenate([v, ones], axis=1), preferred_element_type=F32)
            o, l = ol[:, :HEAD_DIM], ol[:, HEAD_DIM:]
            rows.store(o_ref, j, cols, (o / l).astype(o_ref.dtype))
            stats = jnp.where(lane == h, m + jnp.log(l), stats)
        rows.store(lse_ref, j, all_lanes, stats)


def _dilated_group(qkv, rel_table, B, S, g, dil):
    W, C = DIL_WIDTH, qkv.shape[1]
    tiles = S // DIL_TILE
    col0 = g * 3
    if dil == 1:
        rows, n_res = _NaturalRows, 1
        blocks = min(DIL_BLOCKS_PER_STEP, S // BLOCK)
        steps = S // (blocks * BLOCK)
        views = [(B * S, C), (B * S, W), (B * S, 128)]
        cur_shape, prev_shape = (blocks * BLOCK,), (BLOCK,)
        cur_idx = lambda b, r, i: (b * steps + i,)
        prev_idx = lambda b, r, i: (b * (S // BLOCK) + jnp.maximum(i * blocks - 1, 0),)
    else:
        rows = _Residue16Rows if dil == DIL_RES else _Residue4Rows
        assert dil in (4, DIL_RES) and DIL_RES == 16
        per = rows.per
        t_step = min(DIL_BLOCKS_PER_STEP * per, tiles)
        blocks, steps = t_step // per, tiles // t_step
        if dil == DIL_RES:
            n_res = DIL_RES
            views = [(B * tiles, DIL_RES, DIL_CHUNK, c) for c in (C, W, 128)]
            cur_shape, prev_shape = (t_step, None, DIL_CHUNK), (per, None, DIL_CHUNK)
            cur_idx = lambda b, r, i: (b * steps + i, r, 0)
            prev_idx = lambda b, r, i: (b * (tiles // per) + jnp.maximum(i * blocks - 1, 0), r, 0)
        else:
            n_res = 4
            views = [(B * tiles, 4, 4, DIL_CHUNK, c) for c in (C, W, 128)]
            cur_shape, prev_shape = (t_step, 4, None, DIL_CHUNK), (per, 4, None, DIL_CHUNK)
            cur_idx = lambda b, r, i: (b * steps + i, 0, r, 0)
            prev_idx = lambda b, r, i: (b * (tiles // per) + jnp.maximum(i * blocks - 1, 0), 0, r, 0)

    def spec(shape, idx, width, col):
        return pl.BlockSpec(shape + (width,), lambda b, r, i: idx(b, r, i) + (col,))

    qkv_v = qkv.reshape(views[0])
    order = getattr(rows, "order", list(range(BLOCK)))
    o, lse = pl.pallas_call(
        functools.partial(_dil_kernel, sub_blocks=blocks, rows=rows),
        grid=(B, n_res, steps),
        in_specs=[
            pl.BlockSpec((BLOCK, 2 * BLOCK), lambda b, r, i: (0, 0)),
            pl.BlockSpec(memory_space=pltpu.SMEM),
            spec(cur_shape, cur_idx, W, col0),
            spec(cur_shape, cur_idx, W, col0 + 1), spec(prev_shape, prev_idx, W, col0 + 1),
            spec(cur_shape, cur_idx, W, col0 + 2), spec(prev_shape, prev_idx, W, col0 + 2),
        ],
        out_specs=[spec(cur_shape, cur_idx, W, 0), spec(cur_shape, cur_idx, 128, 0)],
        out_shape=[jax.ShapeDtypeStruct(views[1], BF16), jax.ShapeDtypeStruct(views[2], F32)],
        scratch_shapes=[pltpu.VMEM((DIL_GROUP_HEADS, BLOCK, 2 * BLOCK), F32)],
        compiler_params=_params(("arbitrary", "arbitrary", "arbitrary")),
        name=f"dilated_group{g}",
    )(_rel_bucket_table(dil, order), rel_table, qkv_v, qkv_v, qkv_v, qkv_v, qkv_v)
    return o.reshape(B * S, W), lse.reshape(B * S, 128)


def _bf16_pieces(x):
    hi = x.astype(BF16)
    rest = x - hi.astype(F32)
    mid = rest.astype(BF16)
    return [hi, mid, (rest - mid.astype(F32)).astype(BF16)]


def _combine_matmul_res_kernel(o0_ref, o1_ref, o2_ref, l0_ref, l1_ref, l2_ref, pt_ref, w_ref, r_ref, out_ref,
                               comb_ref):
    pt = pt_ref[...]
    for t in range(out_ref.shape[0] // DIL_TILE):
        rows = slice(t * DIL_TILE, (t + 1) * DIL_TILE)
        pieces = jnp.concatenate(_bf16_pieces(l1_ref[rows, :]) + _bf16_pieces(l2_ref[rows, :]), axis=1)
        lse = jnp.dot(pt, pieces, preferred_element_type=F32)
        l0 = l0_ref[rows, :]
        l1 = lse[:, 0:128] + lse[:, 128:256] + lse[:, 256:384]
        l2 = lse[:, 384:512] + lse[:, 512:640] + lse[:, 640:768]
        mx = jnp.maximum(jnp.maximum(l0, l1), l2)
        e0, e1, e2 = jnp.exp(l0 - mx), jnp.exp(l1 - mx), jnp.exp(l2 - mx)
        den = e0 + e1 + e2
        w0, w1, w2 = e0 / den, e1 / den, e2 / den
        o1 = jnp.dot(pt, o1_ref[rows, :], preferred_element_type=F32)
        o2 = jnp.dot(pt, o2_ref[rows, :], preferred_element_type=F32)
        for h in range(DIL_GROUP_HEADS):
            cols = slice(h * HEAD_DIM, (h + 1) * HEAD_DIM)
            c = (w0[:, h:h + 1] * o0_ref[rows, cols].astype(F32)
                 + w1[:, h:h + 1] * o1[:, cols] + w2[:, h:h + 1] * o2[:, cols])
            comb_ref[rows, cols] = c.astype(comb_ref.dtype)
    out_ref[...] = r_ref[...] + jnp.dot(comb_ref[...], w_ref[...], preferred_element_type=F32)


def _combine_matmul_res(os, lses, unperm, w, res, tm):
    M, K = os[0].shape
    N = w.shape[1]
    o_spec = pl.BlockSpec((tm, K), lambda i: (i, 0))
    l_spec = pl.BlockSpec((tm, 128), lambda i: (i, 0))
    row_block = pl.BlockSpec((tm, N), lambda i: (i, 0))
    return pl.pallas_call(
        _combine_matmul_res_kernel,
        grid=(M // tm,),
        in_specs=[o_spec, o_spec, o_spec, l_spec, l_spec, l_spec,
                  pl.BlockSpec((DIL_TILE, DIL_TILE), lambda i: (0, 0)),
                  pl.BlockSpec((K, N), lambda i: (0, 0), pipeline_mode=pl.Buffered(1)),
                  row_block],
        out_specs=row_block,
        out_shape=jax.ShapeDtypeStruct((M, N), F32),
        scratch_shapes=[pltpu.VMEM((tm, K), BF16)],
        compiler_params=_params(("parallel",)),
        name="combine_matmul_res",
    )(*os, *lses, unperm, w, res)


def _ffn_up_kernel(x_ref, g_ref, wg_ref, wv_ref, cw_ref, cb_ref, *refs, tiles_per_seq, n_side):
    side_src, (o_ref, *side_dst) = refs[:n_side], refs[n_side:-3]
    hn_even_ref, hn_odd_ref, tail_ref = refs[-3:]
    i, j = pl.program_id(0), pl.program_id(1)
    _side_cast(side_src, side_dst)
    tm = hn_even_ref.shape[0]

    def gated(hn_ref):
        gate = jnp.dot(hn_ref[...], wg_ref[...], preferred_element_type=F32)
        val = jnp.dot(hn_ref[...], wv_ref[...], preferred_element_type=F32)
        before = jnp.where((i - 1) % tiles_per_seq == 0, 0.0, tail_ref[j])
        tail_ref[j] = gate[tm - CONV_TAIL:, :]
        extended = jnp.concatenate([before, gate], axis=0)
        conv = cb_ref[...] + gate * cw_ref[CONV_WIDTH - 1:CONV_WIDTH, :]
        for back in range(1, CONV_WIDTH):
            tap = CONV_WIDTH - 1 - back
            conv = conv + pltpu.roll(extended, back, axis=0)[CONV_TAIL:] * cw_ref[tap:tap + 1, :]
        act = conv / (1.0 + jnp.exp(-conv)) * val
        o_ref[...] = act.astype(o_ref.dtype)

    @pl.when(i == 0)
    def _():
        tail_ref[j] = jnp.zeros(tail_ref.shape[1:], tail_ref.dtype)
        _norm_share(x_ref, g_ref, hn_even_ref, 0, j)

    @pl.when(i % 2 == 1)
    def _():
        _norm_share(x_ref, g_ref, hn_odd_ref, 0, j)
        gated(hn_even_ref)

    @pl.when(jnp.logical_and(i % 2 == 0, i > 0))
    def _():
        _norm_share(x_ref, g_ref, hn_even_ref, 0, j)
        gated(hn_odd_ref)


def _ffn_up(x, g, w_up, conv_w, conv_b, side_weights, S, tm, tf):
    M, K = x.shape
    d_ff = w_up.shape[1] // 2
    tiles, steps = M // tm, d_ff // tf

    def col(i, j):
        return jnp.where(i > 0, j, 0)

    side_in, side_out, side_shapes, side_args = _side_cast_specs(side_weights, tiles, steps)
    act, *side = pl.pallas_call(
        functools.partial(_ffn_up_kernel, tiles_per_seq=S // tm, n_side=len(side_weights)),
        grid=(tiles + 1, steps),
        in_specs=[
            _share_spec(tm, K, tiles, steps),
            pl.BlockSpec((1, K), lambda i, j: (0, 0)),
            pl.BlockSpec((K, tf), lambda i, j: (0, col(i, j))),
            pl.BlockSpec((K, tf), lambda i, j: (0, steps + col(i, j))),
            pl.BlockSpec((CONV_WIDTH, tf), lambda i, j: (0, col(i, j))),
            pl.BlockSpec((1, tf), lambda i, j: (0, col(i, j))),
        ] + side_in,
        out_specs=[pl.BlockSpec((tm, tf), lambda i, j: (jnp.maximum(i - 1, 0), col(i, j)))] + side_out,
        out_shape=[jax.ShapeDtypeStruct((M, d_ff), BF16)] + side_shapes,
        scratch_shapes=[pltpu.VMEM((tm, K), BF16), pltpu.VMEM((tm, K), BF16),
                        pltpu.VMEM((steps, CONV_TAIL, tf), F32)],
        compiler_params=_params(("arbitrary", "arbitrary")),
        name="ffn_up",
    )(x, g.reshape(1, K), w_up, w_up, conv_w, conv_b.reshape(1, d_ff), *side_args)
    return act, side


def _score_scale_on_q(n_cols, q_starts, q_width):
    cs = np.ones((n_cols,), np.float32)
    for start in q_starts:
        cs[start:start + q_width] = HEAD_DIM ** -0.5
    return jnp.asarray(cs)


def kernel(x, ln_mix, ln_ffn, ln_f, w_qkv_sb, w_o_sb, w_qkv_dil, w_o_dil, rel_bias, w_up, conv_w, conv_b, w_down):
    B, S, D = x.shape
    depth = ln_mix.shape[0]
    assert S % PROJ_ROWS == 0 and S % min(SB_ROWS, S) == 0 and S % (DIL_RES * BLOCK) == 0
    h = x.reshape(B * S, D)
    w_mix, w_up_i = w_qkv_sb[0].astype(BF16), None
    for i in range(depth):
        j = i // N_MIXERS
        if i % N_MIXERS == 0:
            q_scale = _score_scale_on_q(w_qkv_sb.shape[2], [0], SB_HEADS * HEAD_DIM)
            qkv, (w_o, w_up_i) = _norm_matmul_heads(h, ln_mix[i], w_mix, q_scale, [(w_o_sb, j), (w_up, i)],
                                                    tm=PROJ_ROWS, tn=QKV_COLS)
            o = _stick_breaking(qkv, B, S, tq=min(SB_ROWS, S), heads=SB_HEADS_PER_STEP)
            h = _matmul_res(o, w_o, h, tm=RES_ROWS)
        else:
            perm = _residue_major_permutation()
            q_scale = _score_scale_on_q(w_qkv_dil.shape[2], [g * 3 * DIL_WIDTH for g in range(DIL_GROUPS)], DIL_WIDTH)
            qkv = _norm_matmul_perm(h, ln_mix[i], w_mix, q_scale, jnp.asarray(perm, BF16),
                                    natural_cols=3 * DIL_WIDTH, tm=PROJ_ROWS, tn=QKV_COLS)
            os, lses = [], []
            for g, (window, dil) in enumerate(DIL_PATTERNS):
                assert window // dil == BLOCK and S % (dil * BLOCK) == 0
                table = rel_bias[:, g * DIL_GROUP_HEADS:(g + 1) * DIL_GROUP_HEADS]
                o, lse = _dilated_group(qkv, table, B, S, g, dil)
                os.append(o)
                lses.append(lse)
            h = _combine_matmul_res(os, lses, jnp.asarray(perm.T, BF16), w_o_dil[j].astype(BF16), h, tm=RES_ROWS)
        later = [(w_down, i)]
        if i + 1 < depth:
            dilated_next = (i + 1) % N_MIXERS == 1
            later.append((w_qkv_dil if dilated_next else w_qkv_sb, (i + 1) // N_MIXERS))
            if dilated_next:
                later.append((w_up, i + 1))
        act, (w_down_i, *cast_ahead) = _ffn_up(h, ln_ffn[i], w_up_i, conv_w[i], conv_b[i], later, S,
                                               tm=PROJ_ROWS, tf=FFN_COLS)
        if cast_ahead:
            w_mix, w_up_i = cast_ahead[0], (cast_ahead[1] if len(cast_ahead) > 1 else None)
        h = _matmul_res(act, w_down_i, h, tm=RES_ROWS, norm_gain=ln_f if i == depth - 1 else None)
    return h.reshape(B, S, D)
```
